```python
import jax
import jax.numpy as jnp
from jax import lax
import numpy as np

D_MODEL = 4096
BATCH = 4
SEQ = 2048
DEPTH = 2
DEC_BATCH = 8
DEC_SEQ = 1
PAST_LEN = 16384
PAGE_SIZE = 128

N_EVEN = (DEPTH + 1) // 2
N_ODD = DEPTH // 2
N_PAGES = PAST_LEN // PAGE_SIZE
N_PHYS = (5 * DEC_BATCH * N_PAGES) // 4

HEAD_DIM = 128
NSA_WIDTH = D_MODEL // 2
NSA_HEADS = NSA_WIDTH // HEAD_DIM
NSA_KV_GROUPS = 4
NSA_GROUP_SIZE = NSA_HEADS // NSA_KV_GROUPS
N_KV_STREAMS = 4
CMP_BLOCK = 32
CMP_STRIDE = 16
CMP_HIDDEN = HEAD_DIM
SEL_BLOCK = 64
SEL_TOP_N = 16
WINDOW = 512
SLC_Q_BLOCK = 32
WIN_Q_BLOCK = 128
FORCE_SCORE = 1e9
NEG = -1e30
ROPE_THETA = 10000.0

RNN_WIDTH = D_MODEL - NSA_WIDTH
RNN_BLOCKS = 16
RNN_BLOCK_W = RNN_WIDTH // RNN_BLOCKS
CONV_WIDTH = 4
LRU_C = 8.0

POOL_WINDOWS = (2, 4, 8, 16)
POOL_GROUPS = len(POOL_WINDOWS)
POOL_GROUP_W = D_MODEL // POOL_GROUPS
POOL_BUF = max(POOL_WINDOWS) - 1

MEM_LEN = 256
MEM_HEADS = 4
MEM_WIDTH = D_MODEL // 4
MEM_HEAD_DIM = MEM_WIDTH // MEM_HEADS

MOE_GROUPS = 4
MOE_EXPERTS_PER_GROUP = 8
MOE_EXPERTS = MOE_GROUPS * MOE_EXPERTS_PER_GROUP
MOE_TOP_K = 2
MOE_FF = D_MODEL // 4
MOE_ROW_BLOCK = 128

DN_ALPHA = (2 * DEPTH) ** 0.25
DN_BETA = (8 * DEPTH) ** -0.25
LN_EPS = 1e-5

Q_COLS = NSA_HEADS * HEAD_DIM
KV_COLS = NSA_KV_GROUPS * HEAD_DIM
GATE_COLS = 3 * NSA_HEADS
_S1 = Q_COLS
_S2 = _S1 + N_KV_STREAMS * KV_COLS
_S3 = _S2 + 2 * KV_COLS
_S4 = _S3 + GATE_COLS
_S5 = _S4 + RNN_WIDTH
IN_SPLITS = (_S1, _S2, _S3, _S4, _S5)
IN_COLS = _S5 + RNN_WIDTH

kernel_name = 'nsa_rglru_pool_hmoe_decoder_step'


def layer_norm(x, g, b):
    xf = x.astype(jnp.float32)
    mu = xf.mean(-1, keepdims=True)
    var = jnp.square(xf - mu).mean(-1, keepdims=True)
    return ((xf - mu) * lax.rsqrt(var + LN_EPS) * g + b).astype(x.dtype)


def rope(x, pos):
    half = x.shape[-1] // 2
    inv_freq = ROPE_THETA ** (-jnp.arange(half, dtype=jnp.float32) / half)
    ang = pos.astype(jnp.float32)[:, None] * inv_freq[None, :]
    shp = (1, pos.shape[0]) + (1,) * (x.ndim - 3) + (half,)
    cos = jnp.cos(ang).reshape(shp)
    sin = jnp.sin(ang).reshape(shp)
    xf = x.astype(jnp.float32)
    x1, x2 = xf[..., :half], xf[..., half:]
    return jnp.concatenate([x1 * cos - x2 * sin, x2 * cos + x1 * sin], -1).astype(x.dtype)


def masked_softmax(s, mask):
    s = jnp.where(mask, s.astype(jnp.float32), NEG)
    m = jnp.max(s, -1, keepdims=True)
    p = jnp.exp(s - m) * mask
    return p / jnp.maximum(p.sum(-1, keepdims=True), 1e-30)


def compress(kv, pe, w1, w2):
    B, L, G, d = kv.shape
    n_chunks = L // CMP_STRIDE
    c = kv[:, :n_chunks * CMP_STRIDE].reshape(B, n_chunks, CMP_STRIDE, G, d)
    blocks = jnp.concatenate([c[:, :-1], c[:, 1:]], axis=2) + pe[None, None, :, None, :]
    h = jax.nn.gelu(jnp.einsum('bnlgd,ldh->bngh', blocks, w1))
    return jnp.einsum('bngh,hd->bngd', h, w2)


def nsa_cmp_slc(q, ctx, q_pos, cmp_w1, cmp_w2, cmp_pe):
    B, Sq, G, R, d = q.shape
    L = ctx.shape[1]
    scale = HEAD_DIM ** -0.5
    k_cmp = compress(ctx[:, :, 0], cmp_pe[0], cmp_w1[0], cmp_w2[0])
    v_cmp = compress(ctx[:, :, 1], cmp_pe[1], cmp_w1[1], cmp_w2[1])
    n_cmp = k_cmp.shape[1]
    cmp_end = jnp.arange(n_cmp) * CMP_STRIDE + CMP_BLOCK - 1
    s = jnp.einsum('bqgrd,bngd->bgrqn', q, k_cmp) * scale
    p_cmp = masked_softmax(s, cmp_end[None, :] <= q_pos[:, None])
    o_cmp = jnp.einsum('bgrqn,bngd->bqgrd', p_cmp.astype(v_cmp.dtype), v_cmp)
    n_sel = -(-L // SEL_BLOCK)
    ci = jnp.arange(n_cmp)[:, None] * CMP_STRIDE
    sj = jnp.arange(n_sel)[None, :] * SEL_BLOCK
    overlap = ((ci < sj + SEL_BLOCK) & (ci + CMP_BLOCK > sj)).astype(jnp.float32)
    imp = jnp.einsum('bgrqn,nj->bgqj', p_cmp, overlap)
    blk = jnp.arange(n_sel)[None, :]
    cur = (q_pos // SEL_BLOCK)[:, None]
    valid = blk <= cur
    forced = (blk == 0) | (blk == cur) | (blk == cur - 1)
    score = jnp.where(forced, FORCE_SCORE, jnp.where(valid, imp, NEG))
    n_top = min(SEL_TOP_N, n_sel)
    _, idx = lax.top_k(score, n_top)
    pad = n_sel * SEL_BLOCK - L
    kv_s = jnp.pad(ctx[:, :, 2:4], ((0, 0), (0, pad), (0, 0), (0, 0), (0, 0)))
    kv_s = kv_s.reshape(B, n_sel, SEL_BLOCK, 2, G, d).transpose(0, 4, 1, 2, 3, 5)
    kv_s = kv_s.reshape(B * G, n_sel, SEL_BLOCK, 2, d)
    cq = min(SLC_Q_BLOCK, Sq)
    nq = Sq // cq
    q_c = q.reshape(B, nq, cq, G, R, d).transpose(1, 0, 3, 2, 4, 5)
    idx_c = idx.reshape(B, G, nq, cq, n_top).transpose(2, 0, 1, 3, 4)
    pos_c = q_pos.reshape(nq, cq)
    tok_off = jnp.arange(SEL_BLOCK)

    def one_block(args):
        qb, ib, pb = args
        g_kv = jax.vmap(lambda t, i: t[i])(kv_s, ib.reshape(B * G, cq * n_top))
        g_kv = g_kv.reshape(B, G, cq, n_top * SEL_BLOCK, 2, d)
        kpos = (ib[..., None] * SEL_BLOCK + tok_off).reshape(B, G, cq, n_top * SEL_BLOCK)
        mask = kpos <= pb[None, None, :, None]
        sc = jnp.einsum('bgcrd,bgckd->bgrck', qb, g_kv[..., 0, :]) * scale
        pr = masked_softmax(sc, mask[:, :, None])
        return jnp.einsum('bgrck,bgckd->bgcrd', pr.astype(g_kv.dtype), g_kv[..., 1, :])

    o = lax.map(one_block, (q_c, idx_c, pos_c))
    o_slc = o.transpose(1, 0, 3, 2, 4, 5).reshape(B, Sq, G, R, d)
    return o_cmp, o_slc


def window_banded(q, kw, vw):
    B, T, G, R, d = q.shape
    nb = T // WIN_Q_BLOCK
    nw = WINDOW // WIN_Q_BLOCK
    band = jnp.arange(nb)[:, None] + jnp.arange(nw + 1)[None, :]

    def bands(t):
        tp = jnp.pad(t, ((0, 0), (WINDOW, 0), (0, 0), (0, 0))).reshape(B, nb + nw, WIN_Q_BLOCK, G, d)
        return tp[:, band].reshape(B, nb, (nw + 1) * WIN_Q_BLOCK, G, d)

    kb, vb = bands(kw), bands(vw)
    qb = q.reshape(B, nb, WIN_Q_BLOCK, G, R, d)
    qpos = jnp.arange(T).reshape(nb, WIN_Q_BLOCK)
    kpos = jnp.arange(nb)[:, None] * WIN_Q_BLOCK - WINDOW + jnp.arange((nw + 1) * WIN_Q_BLOCK)[None, :]
    kp, qp = kpos[:, None, :], qpos[:, :, None]
    mask = (kp >= 0) & (kp <= qp) & (kp > qp - WINDOW)
    s = jnp.einsum('bnqgrd,bnkgd->bngrqk', qb, kb) * HEAD_DIM ** -0.5
    p = masked_softmax(s, mask[None, :, None, None])
    o = jnp.einsum('bngrqk,bnkgd->bnqgrd', p.astype(vb.dtype), vb)
    return o.reshape(B, T, G, R, d)


def window_dense(q, kw, vw, q_pos, k_pos):
    s = jnp.einsum('bqgrd,bkgd->bgrqk', q, kw) * HEAD_DIM ** -0.5
    kp, qp = k_pos[None, :], q_pos[:, None]
    mask = (kp >= 0) & (kp <= qp) & (kp > qp - WINDOW)
    p = masked_softmax(s, mask)
    return jnp.einsum('bgrqk,bkgd->bqgrd', p.astype(vw.dtype), vw)


def _lru_combine(left, right):
    a1, b1 = left
    a2, b2 = right
    return a1 * a2, a2 * b1 + b2


def even_mixer(x, t0, past_kv, win_buf, h0, conv0, w_in, w_out, cmp_w1, cmp_w2, cmp_pe,
               conv_w, conv_b, lru_wa, lru_ba, lru_wx, lru_bx, lru_lambda):
    B, T, _ = x.shape
    G, R, d = NSA_KV_GROUPS, NSA_GROUP_SIZE, HEAD_DIM
    pos = t0 + jnp.arange(T)
    proj = x @ w_in
    q, kv, win, gates, u, gb = jnp.split(proj, list(IN_SPLITS), axis=-1)
    q = rope(q.reshape(B, T, NSA_HEADS, d), pos).reshape(B, T, G, R, d)
    kv = kv.reshape(B, T, N_KV_STREAMS, G, d)
    k_rot = rope(kv[:, :, 0::2], pos)
    new_kv = jnp.stack([k_rot[:, :, 0], kv[:, :, 1], k_rot[:, :, 1], kv[:, :, 3]], axis=2)
    win = win.reshape(B, T, 2, G, d)
    new_win = jnp.stack([rope(win[:, :, 0], pos), win[:, :, 1]], axis=2)
    if past_kv is None:
        ctx = new_kv
    else:
        ctx = jnp.concatenate([past_kv, new_kv.astype(past_kv.dtype)], axis=1)
    o_cmp, o_slc = nsa_cmp_slc(q, ctx, pos, cmp_w1, cmp_w2, cmp_pe)
    if win_buf is None:
        win_all = new_win
        o_win = window_banded(q, new_win[:, :, 0], new_win[:, :, 1])
    else:
        win_all = jnp.concatenate([win_buf, new_win.astype(win_buf.dtype)], axis=1)
        k_pos = t0 - win_buf.shape[1] + jnp.arange(win_all.shape[1])
        o_win = window_dense(q, win_all[:, :, 0], win_all[:, :, 1], pos, k_pos)
    g = jax.nn.sigmoid(gates.astype(jnp.float32)).reshape(B, T, G, R, 3)
    o_nsa = (g[..., 0:1] * o_cmp + g[..., 1:2] * o_slc + g[..., 2:3] * o_win).astype(x.dtype)
    o_nsa = o_nsa.reshape(B, T, NSA_WIDTH)
    up = jnp.concatenate([conv0.astype(u.dtype), u], axis=1)
    xc = lax.conv_general_dilated(up, conv_w[:, None, :].astype(u.dtype), window_strides=(1,), padding='VALID',
                                  dimension_numbers=('NWC', 'WIO', 'NWC'), feature_group_count=RNN_WIDTH) + conv_b
    xb = xc.reshape(B, T, RNN_BLOCKS, RNN_BLOCK_W)
    r = jax.nn.sigmoid((jnp.einsum('btnc,ncd->btnd', xb, lru_wa) + lru_ba).astype(jnp.float32)).reshape(B, T, RNN_WIDTH)
    i = jax.nn.sigmoid((jnp.einsum('btnc,ncd->btnd', xb, lru_wx) + lru_bx).astype(jnp.float32)).reshape(B, T, RNN_WIDTH)
    log_a = -LRU_C * r * jax.nn.softplus(-lru_lambda.astype(jnp.float32))
    a = jnp.exp(log_a)
    b = jnp.sqrt(-jnp.expm1(2.0 * log_a)) * (i * xc.astype(jnp.float32))
    b = b.at[:, 0].add(a[:, 0] * h0.astype(jnp.float32))
    _, h = lax.associative_scan(_lru_combine, (a, b), axis=1)
    y_rnn = h.astype(x.dtype) * jax.nn.gelu(gb)
    out = jnp.concatenate([o_nsa, y_rnn], axis=-1) @ w_out
    keep = min(WINDOW, t0 + T)
    return (out, new_kv, win_all[:, -keep:], h[:, -1].astype(x.dtype), up[:, -(CONV_WIDTH - 1):])


def pool_mixer(x, buf, t0, w_pool, scale):
    B, T, D = x.shape
    xf = jnp.concatenate([buf.astype(jnp.float32), x.astype(jnp.float32)], axis=1)
    cs = jnp.concatenate([jnp.zeros((B, 1, D), jnp.float32), jnp.cumsum(xf, axis=1)], axis=1)
    pos = t0 + jnp.arange(T)
    hi = cs[:, POOL_BUF + 1:POOL_BUF + 1 + T]
    cur = xf[:, POOL_BUF:]
    outs = []
    for gi, w in enumerate(POOL_WINDOWS):
        sl = slice(gi * POOL_GROUP_W, (gi + 1) * POOL_GROUP_W)
        lo = cs[:, POOL_BUF + 1 - w:POOL_BUF + 1 - w + T, sl]
        cnt = jnp.minimum(w, pos + 1).astype(jnp.float32)[None, :, None]
        outs.append((hi[..., sl] - lo) / cnt - cur[..., sl])
    pooled = jnp.stack(outs, axis=2).astype(x.dtype)
    y = jnp.einsum('btgc,gcd->btgd', pooled, w_pool).reshape(B, T, D) * scale
    return y, xf[:, -POOL_BUF:].astype(x.dtype)


def mem_kv_proj(mem, wk, wv):
    return jnp.stack([mem @ wk, mem @ wv], axis=2)


def mem_cross_attn(x, mem_kv, wq, wo):
    B, T, _ = x.shape
    M = mem_kv.shape[1]
    q = (x @ wq).reshape(B, T, MEM_HEADS, MEM_HEAD_DIM)
    k = mem_kv[:, :, 0].reshape(B, M, MEM_HEADS, MEM_HEAD_DIM)
    v = mem_kv[:, :, 1].reshape(B, M, MEM_HEADS, MEM_HEAD_DIM)
    s = jnp.einsum('bthd,bmhd->bhtm', q, k).astype(jnp.float32) * MEM_HEAD_DIM ** -0.5
    p = jax.nn.softmax(s, axis=-1).astype(v.dtype)
    o = jnp.einsum('bhtm,bmhd->bthd', p, v).reshape(B, T, MEM_WIDTH)
    return o @ wo


def grouped_experts(xf, eidx, gate, w_gate, w_up, w_down):
    N, D = xf.shape
    A = N * MOE_TOP_K
    rows = max(1, min(MOE_ROW_BLOCK, A // MOE_EXPERTS))
    n_blocks = -(-A // rows) + MOE_EXPERTS
    e = eidx.reshape(A)
    order = jnp.argsort(e)
    e_sorted = e[order]
    counts = jnp.bincount(e, length=MOE_EXPERTS)
    padded = (counts + rows - 1) // rows * rows
    pad_end = jnp.cumsum(padded)
    pad_start = pad_end - padded
    start = jnp.cumsum(counts) - counts
    dest = pad_start[e_sorted] + jnp.arange(A) - start[e_sorted]
    tok = jnp.full((n_blocks * rows,), N, jnp.int32).at[dest].set((order // MOE_TOP_K).astype(jnp.int32))
    block_expert = jnp.minimum(jnp.searchsorted(pad_end, jnp.arange(n_blocks) * rows, side='right'), MOE_EXPERTS - 1)
    x_pad = jnp.concatenate([xf, jnp.zeros((1, D), xf.dtype)], axis=0)
    xb = x_pad[tok].reshape(n_blocks, rows, D)

    def expert_block(args):
        xr, ei = args
        hid = jax.nn.silu(xr @ w_gate[ei]) * (xr @ w_up[ei])
        return hid @ w_down[ei]

    yb = lax.map(expert_block, (xb, block_expert)).reshape(n_blocks * rows, D)
    y_assign = jnp.zeros((A, D), yb.dtype).at[order].set(yb[dest])
    return jnp.einsum('nkd,nk->nd', y_assign.reshape(N, MOE_TOP_K, D), gate.astype(yb.dtype))


def hier_moe(x, w_coarse, b_coarse, w_fine, b_fine, w_gate, w_up, w_down):
    B, T, D = x.shape
    N = B * T
    xf = x.reshape(N, D)
    pc = jax.nn.softmax((xf @ w_coarse).astype(jnp.float32) + b_coarse, axis=-1)
    pg, grp = lax.top_k(pc, 1)
    lf = ((xf @ w_fine).astype(jnp.float32) + b_fine).reshape(N, MOE_GROUPS, MOE_EXPERTS_PER_GROUP)
    lf = jnp.take_along_axis(lf, jnp.broadcast_to(grp[:, :, None], (N, 1, MOE_EXPERTS_PER_GROUP)), axis=1)[:, 0]
    pf = jax.nn.softmax(lf, axis=-1)
    pk, jk = lax.top_k(pf, MOE_TOP_K)
    gate = pg * pk / pk.sum(-1, keepdims=True)
    eidx = grp * MOE_EXPERTS_PER_GROUP + jk
    return grouped_experts(xf, eidx, gate, w_gate, w_up, w_down).reshape(B, T, D).astype(x.dtype)


def setup_inputs(seed: int = 0) -> dict:
    key = jax.random.key(seed)
    ks = iter(jax.random.split(key, 64))

    def nrm(shape, scale):
        return jax.random.normal(next(ks), shape, jnp.float32) * scale

    G, d = NSA_KV_GROUPS, HEAD_DIM
    perm = jax.random.permutation(next(ks), N_PHYS)[:DEC_BATCH * N_PAGES]
    ua = jax.random.uniform(next(ks), (N_EVEN, RNN_WIDTH), jnp.float32, 0.9, 0.999)
    a_base = ua ** (1.0 / LRU_C)
    return {
        'x_prompt': nrm((BATCH, SEQ, D_MODEL), 1.0),
        'x_sample': nrm((DEC_BATCH, DEC_SEQ, D_MODEL), 1.0),
        'mem_prompt': nrm((BATCH, MEM_LEN, D_MODEL), 1.0),
        'cache_nsa_kv': nrm((N_EVEN, N_PHYS, PAGE_SIZE, N_KV_STREAMS, G, d), 1.0),
        'cache_nsa_win': nrm((N_EVEN, DEC_BATCH, min(WINDOW, PAST_LEN), 2, G, d), 1.0),
        'state_rglru_h': nrm((N_EVEN, DEC_BATCH, RNN_WIDTH), 0.5),
        'state_rglru_conv': nrm((N_EVEN, DEC_BATCH, CONV_WIDTH - 1, RNN_WIDTH), 1.0),
        'state_pool': nrm((N_ODD, DEC_BATCH, POOL_BUF, D_MODEL), 1.0),
        'cache_mem_kv': nrm((DEPTH, DEC_BATCH, MEM_LEN, 2, MEM_WIDTH), 1.0),
        'page_table': perm.reshape(DEC_BATCH, N_PAGES).astype(jnp.int32),
        'w_in': nrm((N_EVEN, D_MODEL, IN_COLS), D_MODEL ** -0.5),
        'w_out': nrm((N_EVEN, D_MODEL, D_MODEL), DN_BETA * D_MODEL ** -0.5),
        'cmp_w1': nrm((N_EVEN, 2, CMP_BLOCK, HEAD_DIM, CMP_HIDDEN), (CMP_BLOCK * HEAD_DIM) ** -0.5),
        'cmp_w2': nrm((N_EVEN, 2, CMP_HIDDEN, HEAD_DIM), CMP_HIDDEN ** -0.5),
        'cmp_pe': nrm((N_EVEN, 2, CMP_BLOCK, HEAD_DIM), 0.02),
        'conv_w': nrm((N_EVEN, CONV_WIDTH, RNN_WIDTH), CONV_WIDTH ** -0.5),
        'conv_b': nrm((N_EVEN, RNN_WIDTH), 0.01),
        'lru_wa': nrm((N_EVEN, RNN_BLOCKS, RNN_BLOCK_W, RNN_BLOCK_W), RNN_BLOCK_W ** -0.5),
        'lru_ba': nrm((N_EVEN, RNN_BLOCKS, RNN_BLOCK_W), 0.01),
        'lru_wx': nrm((N_EVEN, RNN_BLOCKS, RNN_BLOCK_W, RNN_BLOCK_W), RNN_BLOCK_W ** -0.5),
        'lru_bx': nrm((N_EVEN, RNN_BLOCKS, RNN_BLOCK_W), 0.01),
        'lru_lambda': jnp.log(a_base) - jnp.log1p(-a_base),
        'pool_w': nrm((N_ODD, POOL_GROUPS, POOL_GROUP_W, POOL_GROUP_W), DN_BETA * POOL_GROUP_W ** -0.5),
        'pool_scale': 1.0 + nrm((N_ODD, D_MODEL), 0.02),
        'mem_wq': nrm((DEPTH, D_MODEL, MEM_WIDTH), D_MODEL ** -0.5),
        'mem_wk': nrm((DEPTH, D_MODEL, MEM_WIDTH), D_MODEL ** -0.5),
        'mem_wv': nrm((DEPTH, D_MODEL, MEM_WIDTH), D_MODEL ** -0.5),
        'mem_wo': nrm((DEPTH, MEM_WIDTH, D_MODEL), DN_BETA * MEM_WIDTH ** -0.5),
        'ln_g': 1.0 + nrm((DEPTH, 3, D_MODEL), 0.02),
        'ln_b': nrm((DEPTH, 3, D_MODEL), 0.02),
        'moe_w_coarse': nrm((DEPTH, D_MODEL, MOE_GROUPS), D_MODEL ** -0.5),
        'moe_b_coarse': nrm((DEPTH, MOE_GROUPS), 0.01),
        'moe_w_fine': nrm((DEPTH, D_MODEL, MOE_EXPERTS), D_MODEL ** -0.5),
        'moe_b_fine': nrm((DEPTH, MOE_EXPERTS), 0.01),
        'moe_w_gate': nrm((DEPTH, MOE_EXPERTS, D_MODEL, MOE_FF), D_MODEL ** -0.5),
        'moe_w_up': nrm((DEPTH, MOE_EXPERTS, D_MODEL, MOE_FF), D_MODEL ** -0.5),
        'moe_w_down': nrm((DEPTH, MOE_EXPERTS, MOE_FF, D_MODEL), DN_BETA * MOE_FF ** -0.5),
    }


def reference(x_prompt, x_sample, mem_prompt, cache_nsa_kv, cache_nsa_win, state_rglru_h, state_rglru_conv,
              state_pool, cache_mem_kv, page_table, w_in, w_out, cmp_w1, cmp_w2, cmp_pe, conv_w, conv_b,
              lru_wa, lru_ba, lru_wx, lru_bx, lru_lambda, pool_w, pool_scale, mem_wq, mem_wk, mem_wv, mem_wo,
              ln_g, ln_b, moe_w_coarse, moe_b_coarse, moe_w_fine, moe_b_fine, moe_w_gate, moe_w_up, moe_w_down):
    n_pages = page_table.shape[1]
    past_len = n_pages * PAGE_SIZE
    B = x_prompt.shape[0]
    DB = x_sample.shape[0]
    G, d = NSA_KV_GROUPS, HEAD_DIM
    xp, xs = x_prompt, x_sample
    kv_p, kv_s, win_p, win_s = [], [], [], []
    h_p, h_s, cv_p, cv_s = [], [], [], []
    pl_p, pl_s, mem_p = [], [], []
    for layer in range(DEPTH):
        if layer % 2 == 0:
            e = layer // 2
            ew = (w_in[e], w_out[e], cmp_w1[e], cmp_w2[e], cmp_pe[e], conv_w[e], conv_b[e],
                  lru_wa[e], lru_ba[e], lru_wx[e], lru_bx[e], lru_lambda[e])
            past = cache_nsa_kv[e][page_table].reshape(DB, past_len, N_KV_STREAMS, G, d)
            mix_p, a0, a1, a2, a3 = even_mixer(xp, 0, None, None, jnp.zeros((B, RNN_WIDTH), xp.dtype),
                                               jnp.zeros((B, CONV_WIDTH - 1, RNN_WIDTH), xp.dtype), *ew)
            mix_s, b0, b1, b2, b3 = even_mixer(xs, past_len, past, cache_nsa_win[e], state_rglru_h[e],
                                               state_rglru_conv[e], *ew)
            kv_p.append(a0)
            win_p.append(a1)
            h_p.append(a2)
            cv_p.append(a3)
            kv_s.append(b0)
            win_s.append(b1)
            h_s.append(b2)
            cv_s.append(b3)
        else:
            o = layer // 2
            mix_p, bp = pool_mixer(xp, jnp.zeros((B, POOL_BUF, D_MODEL), xp.dtype), 0, pool_w[o], pool_scale[o])
            mix_s, bs = pool_mixer(xs, state_pool[o], past_len, pool_w[o], pool_scale[o])
            pl_p.append(bp)
            pl_s.append(bs)
        xp = layer_norm(DN_ALPHA * xp + mix_p, ln_g[layer, 0], ln_b[layer, 0])
        xs = layer_norm(DN_ALPHA * xs + mix_s, ln_g[layer, 0], ln_b[layer, 0])
        mkv = mem_kv_proj(mem_prompt, mem_wk[layer], mem_wv[layer])
        mem_p.append(mkv)
        xp = layer_norm(DN_ALPHA * xp + mem_cross_attn(xp, mkv, mem_wq[layer], mem_wo[layer]),
                        ln_g[layer, 1], ln_b[layer, 1])
        xs = layer_norm(DN_ALPHA * xs + mem_cross_attn(xs, cache_mem_kv[layer], mem_wq[layer], mem_wo[layer]),
                        ln_g[layer, 1], ln_b[layer, 1])
        mw = (moe_w_coarse[layer], moe_b_coarse[layer], moe_w_fine[layer], moe_b_fine[layer],
              moe_w_gate[layer], moe_w_up[layer], moe_w_down[layer])
        xp = layer_norm(DN_ALPHA * xp + hier_moe(xp, *mw), ln_g[layer, 2], ln_b[layer, 2])
        xs = layer_norm(DN_ALPHA * xs + hier_moe(xs, *mw), ln_g[layer, 2], ln_b[layer, 2])
    return (xp, xs, jnp.stack(kv_p), jnp.stack(kv_s), jnp.stack(win_p), jnp.stack(win_s),
            jnp.stack(h_p), jnp.stack(h_s), jnp.stack(cv_p), jnp.stack(cv_s),
            jnp.stack(pl_p), jnp.stack(pl_s), jnp.stack(mem_p))
```

```python
import functools
import math

import jax
import jax.numpy as jnp
from jax import lax
from jax.experimental import pallas as pl
from jax.experimental.pallas import tpu as pltpu

F32 = jnp.float32
BF16 = jnp.bfloat16
I32 = jnp.int32

HEAD_DIM = 128
NSA_G = 4
NSA_R = 4
N_KV_STREAMS = 4
CMP_BLOCK = 32
CMP_STRIDE = 16
SEL_BLOCK = 64
SEL_TOP_N = 16
WINDOW = 512
FORCE_SCORE = 1e9
NEG = -1e30
ROPE_THETA = 10000.0
CONV_WIDTH = 4
LRU_C = 8.0
RNN_BLOCK_W = 128
POOL_WINDOWS = (2, 4, 8, 16)
POOL_BUF = 15
MEM_HEADS = 4
MOE_GROUPS = 4
MOE_EPG = 8
MOE_EXPERTS = MOE_GROUPS * MOE_EPG
MOE_TOP_K = 2
LN_EPS = 1e-5
PAGE_SIZE = 128

LANE = 128
SUBLANE = 8
VMEM_LIMIT = 56 * 1024 * 1024
SAMPLE_ROWS = 16


def _cp(sem, vmem=VMEM_LIMIT):
    return pltpu.CompilerParams(dimension_semantics=sem, vmem_limit_bytes=vmem)


def _split3(x):
    h = x.astype(BF16)
    r = x - h.astype(F32)
    m = r.astype(BF16)
    l = (r - m.astype(F32)).astype(BF16)
    return h, m, l


def _mxu(a, b, nt):
    if nt:
        return lax.dot_general(a, b, (((1,), (1,)), ((), ())), preferred_element_type=F32)
    return jnp.dot(a, b, preferred_element_type=F32)


def _dot(a, b, nt=False):
    return _mxu(a.astype(BF16), b.astype(BF16), nt)


def _bf16r(x):
    return x.astype(BF16).astype(F32)


def _sigmoid(x):
    return 1.0 / (1.0 + jnp.exp(-x))


def _gelu(x):
    return 0.5 * x * (1.0 + jnp.tanh(0.7978845608028654 * (x + 0.044715 * (x * x * x))))


def _msoftmax(s, mask):
    s = jnp.where(mask, s, NEG)
    m = jnp.max(s, axis=-1, keepdims=True)
    p = jnp.where(mask, jnp.exp(s - m), 0.0)
    return p / jnp.maximum(jnp.sum(p, axis=-1, keepdims=True), 1e-30)


def _layer_norm(v, g, b):
    mu = jnp.mean(v, axis=-1, keepdims=True)
    c = v - mu
    var = jnp.mean(c * c, axis=-1, keepdims=True)
    return c * lax.rsqrt(var + LN_EPS) * g + b


def matmul(x, w, *, layer=0, col_off=0, n_out=None, x2=None, tm=1024, tn=1024, tk=512, rope=None):
    if w.ndim == 2:
        w = w[None]
    M, K1 = x.shape
    K2 = 0 if x2 is None else x2.shape[1]
    K = K1 + K2
    assert w.shape[1] == K
    n_out = w.shape[2] - col_off if n_out is None else n_out
    tm, tn, tk = min(tm, M), min(tn, n_out), min(tk, K1)
    assert M % tm == 0 and n_out % tn == 0 and K1 % tk == 0 and K2 % tk == 0 and col_off % tn == 0
    nk1, nk = K1 // tk, K // tk
    joff = col_off // tn
    grid = (M // tm, n_out // tn, nk)

    in_specs = [pl.BlockSpec((tm, tk), lambda i, j, k: (i, jnp.minimum(k, nk1 - 1)))]
    args = [x]
    if x2 is not None:
        in_specs.append(pl.BlockSpec((tm, tk), lambda i, j, k: (i, jnp.maximum(k - nk1, 0))))
        args.append(x2)
    in_specs.append(pl.BlockSpec((None, tk, tn), lambda i, j, k: (layer, k, j + joff)))
    args.append(w)
    rope_blocks = ()
    if rope is not None:
        cos, sin, rope_blocks = rope
        period = cos.shape[0] // tm
        assert cos.shape[0] % tm == 0
        for t in (cos, sin):
            in_specs.append(pl.BlockSpec((tm, LANE), lambda i, j, k: (i % period, 0)))
            args.append(t)

    def kern(*refs):
        refs = list(refs)
        x_ref = refs.pop(0)
        x2_ref = refs.pop(0) if x2 is not None else None
        w_ref = refs.pop(0)
        cos_ref, sin_ref = (refs.pop(0), refs.pop(0)) if rope is not None else (None, None)
        o_ref, acc = refs
        j, k = pl.program_id(1), pl.program_id(2)

        @pl.when(k == 0)
        def _():
            acc[...] = jnp.zeros_like(acc)

        def accum(xr):
            acc[...] += _dot(xr[...], w_ref[...])

        if x2 is None:
            accum(x_ref)
        else:
            pl.when(k < nk1)(lambda: accum(x_ref))
            pl.when(k >= nk1)(lambda: accum(x2_ref))

        @pl.when(k == nk - 1)
        def _():
            if rope is None:
                o_ref[...] = acc[...]
                return
            rot = functools.reduce(jnp.logical_or, [j == c for c in rope_blocks])

            @pl.when(rot)
            def _():
                c, s = cos_ref[...], sin_ref[...]
                for h in range(tn // LANE):
                    seg = acc[:, h * LANE:(h + 1) * LANE]
                    o_ref[:, h * LANE:(h + 1) * LANE] = seg * c + pltpu.roll(seg, LANE // 2, 1) * s

            @pl.when(jnp.logical_not(rot))
            def _():
                o_ref[...] = acc[...]

    return pl.pallas_call(
        kern,
        grid=grid,
        in_specs=in_specs,
        out_specs=pl.BlockSpec((tm, tn), lambda i, j, k: (i, j)),
        out_shape=jax.ShapeDtypeStruct((M, n_out), F32),
        scratch_shapes=[pltpu.VMEM((tm, tn), F32)],
        compiler_params=_cp(("parallel", "parallel", "arbitrary")),
        name="matmul",
    )(*args)


def add_ln(x, f, ln_g, ln_b, idx, alpha, tm=256):
    M, D = x.shape
    tm = min(tm, M)

    def kern(x_ref, f_ref, g_ref, b_ref, o_ref):
        o_ref[...] = _layer_norm(alpha * x_ref[...] + f_ref[...], g_ref[...], b_ref[...])

    row = pl.BlockSpec((tm, D), lambda i: (i, 0))
    par = pl.BlockSpec((None, 1, D), lambda i: (idx, 0, 0))
    return pl.pallas_call(
        kern, grid=(M // tm,), in_specs=[row, row, par, par], out_specs=row,
        out_shape=jax.ShapeDtypeStruct((M, D), F32), compiler_params=_cp(("parallel",)),
        name="add_ln",
    )(x, f, ln_g, ln_b)


CMP_PAGES = 8


def compress_ab(pages, page_ids, cmp_w1, cmp_pe, e, n_batch, pages_per_seq):
    P = CMP_PAGES
    assert pages_per_seq % P == 0
    cpp = PAGE_SIZE // CMP_STRIDE
    steps = pages_per_seq // P
    d = HEAD_DIM
    n_chunks = pages_per_seq * cpp
    rows = NSA_G * P * cpp

    def kern(pid_ref, *refs):
        page_refs, w_ref, pe_ref, o_ref = refs[:P], refs[P], refs[P + 1], refs[P + 2]
        for s in range(2):
            acc_a = jnp.zeros((rows, d), F32)
            acc_b = jnp.zeros((rows, d), F32)
            for l in range(CMP_STRIDE):
                lhs = jnp.concatenate(
                    [page_refs[i][pl.ds(l, cpp, stride=CMP_STRIDE), s, g, :]
                     for g in range(NSA_G) for i in range(P)], axis=0)
                acc_a = acc_a + _dot(lhs + pe_ref[s, l:l + 1, :], w_ref[s, l])
                lb = CMP_STRIDE + l
                acc_b = acc_b + _dot(lhs + pe_ref[s, lb:lb + 1, :], w_ref[s, lb])
            for g in range(NSA_G):
                c0 = (s * NSA_G + g) * 2 * d
                o_ref[:, c0:c0 + d] = acc_a[g * P * cpp:(g + 1) * P * cpp]
                o_ref[:, c0 + d:c0 + 2 * d] = acc_b[g * P * cpp:(g + 1) * P * cpp]

    def page_spec(i):
        return pl.BlockSpec((None, PAGE_SIZE, 2, NSA_G, d),
                            lambda b, st, pid: (pid[(b * steps + st) * P + i], 0, 0, 0, 0))

    gs = pltpu.PrefetchScalarGridSpec(
        num_scalar_prefetch=1,
        grid=(n_batch, steps),
        in_specs=[page_spec(i) for i in range(P)]
        + [pl.BlockSpec((None, 2, CMP_BLOCK, d, d), lambda b, st, pid: (e, 0, 0, 0, 0)),
           pl.BlockSpec((None, 2, CMP_BLOCK, d), lambda b, st, pid: (e, 0, 0, 0))],
        out_specs=pl.BlockSpec((None, P * cpp, 2 * NSA_G * 2 * d), lambda b, st, pid: (b, st, 0)),
    )
    return pl.pallas_call(
        kern, grid_spec=gs,
        out_shape=jax.ShapeDtypeStruct((n_batch, n_chunks, 2 * NSA_G * 2 * d), F32),
        compiler_params=_cp(("parallel", "parallel")), name="compress_ab",
    )(page_ids, *([pages] * P), cmp_w1, cmp_pe)


def compress_fin(ab, cmp_w2, e):
    nb, nch, _ = ab.shape
    d = HEAD_DIM

    def kern(ab_ref, w2_ref, o_ref):
        for g in range(NSA_G):
            a = ab_ref[:, g * 2 * d:g * 2 * d + d]
            bn = ab_ref[:, g * 2 * d + d:(g + 1) * 2 * d]
            h = a + pltpu.roll(bn, nch - 1, 0)
            o_ref[g] = _dot(_gelu(h), w2_ref[...])

    return pl.pallas_call(
        kern, grid=(nb, 2),
        in_specs=[pl.BlockSpec((None, nch, NSA_G * 2 * d), lambda b, s: (b, 0, s)),
                  pl.BlockSpec((None, None, d, d), lambda b, s: (e, s, 0, 0))],
        out_specs=pl.BlockSpec((None, None, NSA_G, nch, d), lambda b, s: (s, b, 0, 0, 0)),
        out_shape=jax.ShapeDtypeStruct((2, nb, NSA_G, nch, d), F32),
        compiler_params=_cp(("parallel", "parallel")), name="compress_fin",
    )(ab, cmp_w2)


def _overlap_matrix(n_rows, n_cols):
    ci = jnp.arange(n_rows)[:, None] * CMP_STRIDE
    sj = jnp.arange(n_cols)[None, :] * SEL_BLOCK
    return ((ci < sj + SEL_BLOCK) & (ci + CMP_BLOCK > sj)).astype(BF16)


NSA_TQ = 128


def nsa_prompt(q, kv, win, kvcmp, gates, B, T):
    d, G, R, tq = HEAD_DIM, NSA_G, NSA_R, NSA_TQ
    nq = T // tq
    ncp = kvcmp.shape[3]
    n_cmp = T // CMP_STRIDE - 1
    n_sel = -(-T // SEL_BLOCK)
    n_top = min(SEL_TOP_N, n_sel)
    wk = min(T, WINDOW + tq)
    assert n_sel <= LANE and T % tq == 0
    scale = d ** -0.5
    overlap = _overlap_matrix(ncp, LANE)
    expand = (jnp.arange(LANE)[:, None] == (jnp.arange(T) // SEL_BLOCK)[None, :]).astype(BF16)

    def kern(q_ref, ks_ref, vs_ref, kw_ref, vw_ref, kc_ref, vc_ref, g_ref, ov_ref, ex_ref, o_ref,
             ksb, vsb, kwb, vwb, kcb, vcb):
        qi = pl.program_id(2)

        @pl.when(qi == 0)
        def _():
            ksb[...] = ks_ref[...].astype(BF16)
            vsb[...] = vs_ref[...].astype(BF16)
            kwb[...] = kw_ref[...].astype(BF16)
            vwb[...] = vw_ref[...].astype(BF16)
            kcb[...] = kc_ref[...].astype(BF16)
            vcb[...] = vc_ref[...].astype(BF16)

        t0 = qi * tq
        qb = q_ref[...]
        qs = jnp.concatenate([qb[:, r * d:(r + 1) * d] for r in range(R)], axis=0).astype(BF16)
        qpos_s = t0 + (lax.broadcasted_iota(I32, (R * tq, 1), 0) & (tq - 1))

        s = _mxu(qs, kcb[...], True) * scale
        n_idx = lax.broadcasted_iota(I32, (1, ncp), 1)
        cmask = (n_idx * CMP_STRIDE + (CMP_BLOCK - 1) <= qpos_s) & (n_idx < n_cmp)
        p = _msoftmax(s, cmask)
        pb = p.astype(BF16)
        o_cmp = _mxu(pb, vcb[...], False)
        imp_r = _mxu(pb, ov_ref[...], False)
        imp = imp_r[0:tq]
        for r in range(1, R):
            imp = imp + imp_r[r * tq:(r + 1) * tq]

        blk = lax.broadcasted_iota(I32, (tq, LANE), 1)
        qpos = t0 + lax.broadcasted_iota(I32, (tq, LANE), 0)
        cur = qpos // SEL_BLOCK
        valid = blk <= cur
        forced = (blk == 0) | (blk == cur) | (blk == cur - 1)
        score = jnp.where(forced, FORCE_SCORE, jnp.where(valid, imp, NEG))
        cnt = jnp.zeros((tq, LANE), I32)
        for i in range(n_sel):
            col = score[:, i:i + 1]
            beats = (col > score) | ((col == score) & (blk > i))
            cnt = cnt + beats.astype(I32)
        sel = ((cnt < n_top) & valid).astype(BF16)
        selx = _mxu(sel, ex_ref[...], False)

        kpos = lax.broadcasted_iota(I32, (tq, T), 1)
        qpos_t = t0 + lax.broadcasted_iota(I32, (tq, T), 0)
        smask = (selx > 0.5) & (kpos <= qpos_t)

        kstart = pl.multiple_of(jnp.clip(t0 - WINDOW, 0, T - wk), tq)
        kwin = kwb[pl.ds(kstart, wk), :]
        vwin = vwb[pl.ds(kstart, wk), :]
        s_w = _mxu(qs, kwin, True) * scale
        kp = kstart + lax.broadcasted_iota(I32, (1, wk), 1)
        wmask = (kp <= qpos_s) & (kp > qpos_s - WINDOW)
        p_w = _msoftmax(s_w, wmask)
        o_win = _mxu(p_w.astype(BF16), vwin, False)

        gt = _sigmoid(g_ref[...])
        for r in range(R):
            s_r = _mxu(qs[r * tq:(r + 1) * tq], ksb[...], True) * scale
            p_r = _msoftmax(s_r, smask)
            o_slc = _mxu(p_r.astype(BF16), vsb[...], False)
            o_r = (gt[:, 3 * r:3 * r + 1] * o_cmp[r * tq:(r + 1) * tq]
                   + gt[:, 3 * r + 1:3 * r + 2] * o_slc
                   + gt[:, 3 * r + 2:3 * r + 3] * o_win[r * tq:(r + 1) * tq])
            o_ref[:, r * d:(r + 1) * d] = o_r

    seq = lambda blkcol: pl.BlockSpec((T, d), lambda b, g, qi: (b, blkcol(g)))
    cmp_spec = lambda s: pl.BlockSpec((None, None, None, ncp, d), lambda b, g, qi: (s, b, g, 0, 0))
    return pl.pallas_call(
        kern, grid=(B, G, nq),
        in_specs=[pl.BlockSpec((tq, R * d), lambda b, g, qi: (b * nq + qi, g)),
                  seq(lambda g: 2 * G + g), seq(lambda g: 3 * G + g),
                  seq(lambda g: g), seq(lambda g: G + g),
                  cmp_spec(0), cmp_spec(1),
                  pl.BlockSpec((tq, LANE), lambda b, g, qi: (b * nq + qi, g)),
                  pl.BlockSpec((ncp, LANE), lambda b, g, qi: (0, 0)),
                  pl.BlockSpec((LANE, T), lambda b, g, qi: (0, 0))],
        out_specs=pl.BlockSpec((tq, R * d), lambda b, g, qi: (b * nq + qi, g)),
        out_shape=jax.ShapeDtypeStruct((B * T, G * R * d), F32),
        scratch_shapes=[pltpu.VMEM((T, d), BF16)] * 4 + [pltpu.VMEM((ncp, d), BF16)] * 2,
        compiler_params=_cp(("parallel", "parallel", "arbitrary")), name="nsa_prompt",
    )(q, kv, kv, win, win, kvcmp, kvcmp, gates, overlap, expand)


def nsa_sample_cmp(q8, kvcmp, q_pos):
    B = q8.shape[0]
    d, G, R = HEAD_DIM, NSA_G, NSA_R
    nch = kvcmp.shape[3]
    n_cmp = nch - 1
    n_sel = -(-(q_pos + 1) // SEL_BLOCK)
    nsp = -(-n_sel // LANE) * LANE
    overlap = _overlap_matrix(nch, nsp)
    scale = d ** -0.5

    def kern(q_ref, k_ref, v_ref, ov_ref, o_ref, imp_ref):
        n_idx = lax.broadcasted_iota(I32, (1, nch), 1)
        cmask = (n_idx * CMP_STRIDE + (CMP_BLOCK - 1) <= q_pos) & (n_idx < n_cmp)
        rows = []
        for g in range(G):
            s = _dot(q_ref[g], k_ref[g], nt=True) * scale
            pb = _msoftmax(s, cmask).astype(BF16)
            o_ref[g] = _dot(pb, v_ref[g])
            rows.append(jnp.sum(_mxu(pb, ov_ref[...], False)[0:R], axis=0, keepdims=True))
        imp_ref[...] = jnp.concatenate(rows + [jnp.zeros((SUBLANE - G, nsp), F32)], axis=0)

    qspec = pl.BlockSpec((None, G, SUBLANE, d), lambda b: (b, 0, 0, 0))
    cspec = lambda s: pl.BlockSpec((None, None, G, nch, d), lambda b: (s, b, 0, 0, 0))
    return pl.pallas_call(
        kern, grid=(B,),
        in_specs=[qspec, cspec(0), cspec(1), pl.BlockSpec((nch, nsp), lambda b: (0, 0))],
        out_specs=[qspec, pl.BlockSpec((None, SUBLANE, nsp), lambda b: (b, 0, 0))],
        out_shape=[jax.ShapeDtypeStruct((B, G, SUBLANE, d), F32),
                   jax.ShapeDtypeStruct((B, SUBLANE, nsp), F32)],
        compiler_params=_cp(("parallel",)), name="nsa_sample_cmp",
    )(q8, kvcmp, kvcmp, overlap)


def nsa_sample_topk(imp, q_pos):
    nr, nsp = imp.shape
    assert nr == LANE
    n_sel = -(-(q_pos + 1) // SEL_BLOCK)
    n_top = min(SEL_TOP_N, n_sel)
    cur = q_pos // SEL_BLOCK

    def kern(imp_ref, o_ref, sct):
        blk = lax.broadcasted_iota(I32, (nr, nsp), 1)
        valid = blk <= cur
        forced = (blk == 0) | (blk == cur) | (blk == cur - 1)
        score = jnp.where(forced, FORCE_SCORE, jnp.where(valid, imp_ref[...], NEG))
        eye = (lax.broadcasted_iota(I32, (nsp, nsp), 0) == lax.broadcasted_iota(I32, (nsp, nsp), 1)).astype(BF16)
        h, m, l = _split3(score)
        sct[...] = (_mxu(eye, h, True) + _mxu(eye, m, True)) + _mxu(eye, l, True)
        sc = sct[...]
        jidx = lax.broadcasted_iota(I32, (nsp, nr), 0)

        def body(i, cnt):
            row = sct[pl.ds(i, 1), :]
            beats = (row > sc) | ((row == sc) & (jidx > i))
            return cnt + beats.astype(I32)

        cnt = lax.fori_loop(0, n_sel, body, jnp.zeros((nsp, nr), I32))
        for p in range(n_top):
            o_ref[p:p + 1, :] = jnp.sum(jnp.where(cnt == p, jidx, 0), axis=0, keepdims=True)

    return pl.pallas_call(
        kern, grid=(1,),
        in_specs=[pl.BlockSpec((nr, nsp), lambda i: (0, 0))],
        out_specs=pl.BlockSpec((n_top, nr), lambda i: (0, 0)),
        out_shape=jax.ShapeDtypeStruct((n_top, nr), I32),
        scratch_shapes=[pltpu.VMEM((nsp, nr), F32)],
        compiler_params=_cp(("arbitrary",)), name="nsa_sample_topk",
    )(imp)


def nsa_sample_attend(q8, o_cmp, gates, idx, page_ids, pages, new_kv, cwin, new_win, e, q_pos):
    B = q8.shape[0]
    d, G, R = HEAD_DIM, NSA_G, NSA_R
    n_top = idx.shape[0] // (B * G)
    n_pages = page_ids.shape[0] // B
    n_cache_blocks = n_pages * (PAGE_SIZE // SEL_BLOCK)
    bpp = PAGE_SIZE // SEL_BLOCK
    w0 = cwin.shape[2]
    scale = d ** -0.5
    nk = n_top * SEL_BLOCK

    def kern(idx_ref, pid_ref, q_ref, oc_ref, g_ref, pages_ref, nkv_ref, cw_ref, nw_ref, o_ref,
             kvbuf, sem):
        b = pl.program_id(0)

        def slot_copy(slot, j):
            page = pid_ref[b * n_pages + j // bpp]
            r0 = (j % bpp) * SEL_BLOCK
            return pltpu.make_async_copy(
                pages_ref.at[page, pl.ds(r0, SEL_BLOCK), pl.ds(2, 2)], kvbuf.at[slot], sem.at[0])

        for slot in range(G * n_top):
            j = idx_ref[b * G * n_top + slot]

            @pl.when(j < n_cache_blocks)
            def _():
                slot_copy(slot, j).start()

            @pl.when(j >= n_cache_blocks)
            def _():
                kvbuf[slot] = jnp.zeros((SEL_BLOCK, 2, G, d), F32)

            @pl.when(j * SEL_BLOCK == q_pos)
            def _():
                kvbuf[slot, 0] = nkv_ref[2:4]

        for slot in range(G * n_top):
            j = idx_ref[b * G * n_top + slot]

            @pl.when(j < n_cache_blocks)
            def _():
                slot_copy(slot, j).wait()

        lane_k = lax.broadcasted_iota(I32, (1, nk), 1)
        row8 = lax.broadcasted_iota(I32, (SUBLANE, 1), 0)
        for g in range(G):
            qg = q_ref[g]
            kk = jnp.concatenate([kvbuf[g * n_top + p, :, 0, g, :] for p in range(n_top)], axis=0)
            vv = jnp.concatenate([kvbuf[g * n_top + p, :, 1, g, :] for p in range(n_top)], axis=0)
            kpos = lane_k & (SEL_BLOCK - 1)
            for p in range(n_top):
                j = idx_ref[(b * G + g) * n_top + p]
                kpos = kpos + jnp.where(lane_k // SEL_BLOCK == p, j * SEL_BLOCK, 0)
            s = _dot(qg, kk, nt=True) * scale
            p_s = _msoftmax(s, kpos <= q_pos)
            o_slc = _dot(p_s, vv)
            kw = cw_ref[:, 0, g, :]
            vw = cw_ref[:, 1, g, :]
            s_w = _dot(qg, kw, nt=True) * scale
            wpos = (q_pos - w0) + lax.broadcasted_iota(I32, (1, w0), 1)
            wmask = (wpos >= 0) & (wpos <= q_pos) & (wpos > q_pos - WINDOW)
            s_w = jnp.where(wmask, s_w, NEG)
            s_n = jnp.sum(_bf16r(qg) * _bf16r(nw_ref[0, g:g + 1, :]), axis=-1, keepdims=True) * scale
            m = jnp.maximum(jnp.max(s_w, axis=-1, keepdims=True), s_n)
            p_w = jnp.where(wmask, jnp.exp(s_w - m), 0.0)
            p_n = jnp.exp(s_n - m)
            den = jnp.sum(p_w, axis=-1, keepdims=True) + p_n
            o_win = _dot(p_w / den, vw) + _bf16r(p_n / den) * _bf16r(nw_ref[1, g:g + 1, :])
            gt = _sigmoid(g_ref[g])
            gc = [jnp.zeros((SUBLANE, 1), F32)] * 3
            for r in range(R):
                for c in range(3):
                    gc[c] = jnp.where(row8 == r, gt[:, 3 * r + c:3 * r + c + 1], gc[c])
            o_ref[g] = gc[0] * oc_ref[g] + gc[1] * o_slc + gc[2] * o_win

    qspec = pl.BlockSpec((None, G, SUBLANE, d), lambda b, i, p: (b, 0, 0, 0))
    gs = pltpu.PrefetchScalarGridSpec(
        num_scalar_prefetch=2, grid=(B,),
        in_specs=[qspec, qspec,
                  pl.BlockSpec((None, G, 1, LANE), lambda b, i, p: (b, 0, 0, 0)),
                  pl.BlockSpec(memory_space=pl.ANY),
                  pl.BlockSpec((None, N_KV_STREAMS, G, d), lambda b, i, p: (b, 0, 0, 0)),
                  pl.BlockSpec((None, None, w0, 2, G, d), lambda b, i, p: (e, b, 0, 0, 0, 0)),
                  pl.BlockSpec((None, 2, G, d), lambda b, i, p: (b, 0, 0, 0))],
        out_specs=qspec,
        scratch_shapes=[pltpu.VMEM((G * n_top, SEL_BLOCK, 2, G, d), F32),
                        pltpu.SemaphoreType.DMA((1,))],
    )
    return pl.pallas_call(
        kern, grid_spec=gs, out_shape=jax.ShapeDtypeStruct((B, G, SUBLANE, d), F32),
        compiler_params=_cp(("arbitrary",)), name="nsa_sample_attend",
    )(idx, page_ids, q8, o_cmp, gates, pages, new_kv, cwin, new_win)


def rglru(u, gb, conv0, h0, conv_w, conv_b, lru_wa, lru_ba, lru_wx, lru_bx, lru_lambda, e, B, T):
    W = u.shape[1]
    cw = RNN_BLOCK_W
    nblk = W // cw
    tc = min(T, 256)
    nt = T // tc
    assert T % tc == 0 and tc % SUBLANE == 0

    def kern(u_ref, gb_ref, c0_ref, h0_ref, cw_ref, cb_ref, wa_ref, ba_ref, wx_ref, bx_ref, lam_ref,
             y_ref, h_ref, prev, hprev):
        ti = pl.program_id(2)

        @pl.when(ti == 0)
        def _():
            prev[...] = c0_ref[...]
            hprev[...] = h0_ref[...]

        uu = u_ref[...]
        ext = _bf16r(jnp.concatenate([prev[...], uu], axis=0))
        prev[...] = uu[tc - SUBLANE:tc]
        cwr = _bf16r(cw_ref[...])
        xc = cwr[0:1, :] * ext[5:5 + tc]
        for j in range(1, CONV_WIDTH):
            xc = xc + cwr[j:j + 1, :] * ext[5 + j:5 + j + tc]
        xc = xc + cb_ref[...]
        r = _sigmoid(_dot(xc, wa_ref[...]) + ba_ref[...])
        i = _sigmoid(_dot(xc, wx_ref[...]) + bx_ref[...])
        nl = -lam_ref[...]
        softplus = jnp.maximum(nl, 0.0) + jnp.log1p(jnp.exp(-jnp.abs(nl)))
        log_a = -LRU_C * r * softplus
        a = jnp.exp(log_a)
        bt = jnp.sqrt(-jnp.tanh(log_a) * (a * a + 1.0)) * (i * xc)
        rows = lax.broadcasted_iota(I32, (tc, 1), 0)
        step = 1
        while step < tc:
            keep = rows >= step
            bt = jnp.where(keep, a * pltpu.roll(bt, step, 0) + bt, bt)
            a = jnp.where(keep, a * pltpu.roll(a, step, 0), a)
            step *= 2
        h = a * hprev[...] + bt
        hprev[...] = h[tc - 1:tc]
        h_ref[...] = h
        y_ref[...] = h * _gelu(gb_ref[...])

    tile = pl.BlockSpec((tc, cw), lambda b, c, t: (b * nt + t, c))
    vec = lambda: pl.BlockSpec((None, 1, cw), lambda b, c, t: (e, 0, c))
    mat = lambda: pl.BlockSpec((None, None, cw, cw), lambda b, c, t: (e, c, 0, 0))
    bvec = lambda: pl.BlockSpec((None, None, 1, cw), lambda b, c, t: (e, c, 0, 0))
    return pl.pallas_call(
        kern, grid=(B, nblk, nt),
        in_specs=[tile, tile,
                  pl.BlockSpec((None, SUBLANE, cw), lambda b, c, t: (b, 0, c)),
                  pl.BlockSpec((None, 1, cw), lambda b, c, t: (b, 0, c)),
                  pl.BlockSpec((None, CONV_WIDTH, cw), lambda b, c, t: (e, 0, c)),
                  vec(), mat(), bvec(), mat(), bvec(), vec()],
        out_specs=[tile, tile],
        out_shape=[jax.ShapeDtypeStruct((B * T, W), F32)] * 2,
        scratch_shapes=[pltpu.VMEM((SUBLANE, cw), F32), pltpu.VMEM((1, cw), F32)],
        compiler_params=_cp(("parallel", "parallel", "arbitrary")), name="rglru",
    )(u, gb, conv0, h0, conv_w, conv_b[:, None, :], lru_wa, lru_ba[:, :, None, :], lru_wx,
      lru_bx[:, :, None, :], lru_lambda[:, None, :])


def pool_mixer(x, buf16, pool_w, pool_scale, o, B, T, t0):
    D = x.shape[1]
    ng = len(POOL_WINDOWS)
    gw = D // ng
    tt = min(T, 256)
    nt = T // tt
    halo = 2 * SUBLANE
    assert T % tt == 0 and tt >= halo and POOL_WINDOWS == (2, 4, 8, 16)

    def kern(x_ref, buf_ref, w_ref, sc_ref, o_ref, prev):
        g, ti = pl.program_id(0), pl.program_id(2)

        @pl.when(ti == 0)
        def _():
            prev[...] = buf_ref[...]

        xx = x_ref[...]
        ext = jnp.concatenate([prev[...], xx], axis=0)
        prev[...] = xx[tt - halo:tt]
        s2 = ext + pltpu.roll(ext, 1, 0)
        s4 = s2 + pltpu.roll(s2, 2, 0)
        s8 = s4 + pltpu.roll(s4, 4, 0)
        s16 = s8 + pltpu.roll(s8, 8, 0)
        sw = jnp.where(g == 0, s2, jnp.where(g == 1, s4, jnp.where(g == 2, s8, s16)))[halo:]
        wlen = jnp.left_shift(2, g)
        pos = t0 + ti * tt + lax.broadcasted_iota(I32, (tt, 1), 0)
        cnt = jnp.minimum(wlen, pos + 1).astype(F32)
        pooled = sw / cnt - xx
        o_ref[...] = _dot(pooled, w_ref[...]) * sc_ref[...]

    return pl.pallas_call(
        kern, grid=(ng, B, nt),
        in_specs=[pl.BlockSpec((tt, gw), lambda g, b, t: (b * nt + t, g)),
                  pl.BlockSpec((None, halo, gw), lambda g, b, t: (b, 0, g)),
                  pl.BlockSpec((None, None, gw, gw), lambda g, b, t: (o, g, 0, 0)),
                  pl.BlockSpec((None, 1, gw), lambda g, b, t: (o, 0, g))],
        out_specs=pl.BlockSpec((tt, gw), lambda g, b, t: (b * nt + t, g)),
        out_shape=jax.ShapeDtypeStruct((B * T, D), F32),
        scratch_shapes=[pltpu.VMEM((halo, gw), F32)],
        compiler_params=_cp(("parallel", "parallel", "arbitrary")), name="pool_mixer",
    )(x, buf16, pool_w, pool_scale[:, None, :])


def mem_attn(q, mkv, B, T, kv_off=0):
    MW = q.shape[1]
    M = mkv.shape[1]
    dh = MW // MEM_HEADS
    tq = min(T, 256)
    nq = T // tq
    scale = dh ** -0.5

    def kern(q_ref, kv_ref, o_ref):
        for h in range(MEM_HEADS):
            sl = slice(h * dh, (h + 1) * dh)
            s = _dot(q_ref[:, sl], kv_ref[:, 0, sl], nt=True) * scale
            m = jnp.max(s, axis=-1, keepdims=True)
            p = jnp.exp(s - m)
            p = p / jnp.sum(p, axis=-1, keepdims=True)
            o_ref[:, sl] = _dot(p, kv_ref[:, 1, sl])

    return pl.pallas_call(
        kern, grid=(B, nq),
        in_specs=[pl.BlockSpec((tq, MW), lambda b, i: (b * nq + i, 0)),
                  pl.BlockSpec((None, M, 2, MW), lambda b, i: (kv_off + b, 0, 0, 0))],
        out_specs=pl.BlockSpec((tq, MW), lambda b, i: (b * nq + i, 0)),
        out_shape=jax.ShapeDtypeStruct((B * T, MW), F32),
        compiler_params=_cp(("parallel", "parallel")), name="mem_attn",
    )(q, mkv)


ROUTER_ROWS = 40


def moe_router(x, w_t, bias, tm=512):
    N, D = x.shape
    tm = min(tm, N)
    assert N % tm == 0 and tm % LANE == 0

    def kern(x_ref, w_ref, b_ref, e_ref, g_ref):
        lt = _dot(w_ref[...], x_ref[...], nt=True) + b_ref[...]
        c = [lt[i:i + 1] for i in range(MOE_GROUPS)]
        m = functools.reduce(jnp.maximum, c)
        ex = [jnp.exp(ci - m) for ci in c]
        tot = functools.reduce(jnp.add, ex)
        pc = [ei / tot for ei in ex]
        pg = functools.reduce(jnp.maximum, pc)
        grp = jnp.full(pg.shape, MOE_GROUPS - 1, I32)
        for i in range(MOE_GROUPS - 2, -1, -1):
            grp = jnp.where(pc[i] == pg, i, grp)
        lf = []
        for j in range(MOE_EPG):
            v = lt[MOE_GROUPS + j:MOE_GROUPS + j + 1]
            for gi in range(1, MOE_GROUPS):
                r0 = MOE_GROUPS + gi * MOE_EPG + j
                v = jnp.where(grp == gi, lt[r0:r0 + 1], v)
            lf.append(v)
        m = functools.reduce(jnp.maximum, lf)
        ex = [jnp.exp(v - m) for v in lf]
        tot = functools.reduce(jnp.add, ex)
        pf = [ei / tot for ei in ex]

        def first_max(vals):
            best = functools.reduce(jnp.maximum, vals)
            arg = jnp.full(best.shape, len(vals) - 1, I32)
            for j in range(len(vals) - 2, -1, -1):
                arg = jnp.where(vals[j] == best, j, arg)
            return best, arg

        p1, j1 = first_max(pf)
        p2, j2 = first_max([jnp.where(j1 == j, -1.0, pf[j]) for j in range(MOE_EPG)])
        den = p1 + p2
        e_ref[0:1, :] = grp * MOE_EPG + j1
        e_ref[1:2, :] = grp * MOE_EPG + j2
        g_ref[0:1, :] = pg * p1 / den
        g_ref[1:2, :] = pg * p2 / den

    out = pl.BlockSpec((MOE_TOP_K, tm), lambda i: (0, i))
    return pl.pallas_call(
        kern, grid=(N // tm,),
        in_specs=[pl.BlockSpec((tm, D), lambda i: (i, 0)),
                  pl.BlockSpec((ROUTER_ROWS, D), lambda i: (0, 0)),
                  pl.BlockSpec((ROUTER_ROWS, 1), lambda i: (0, 0))],
        out_specs=[out, out],
        out_shape=[jax.ShapeDtypeStruct((MOE_TOP_K, N), I32), jax.ShapeDtypeStruct((MOE_TOP_K, N), F32)],
        compiler_params=_cp(("parallel",)), name="moe_router",
    )(x, w_t, bias)


ROW_COPY_CHUNK = 256


def row_copy(src, n_out, si, di, zero_lo=None, zero_hi=None, fill_rows=SUBLANE, fill_from=None):
    n = si.shape[0]
    D = src.shape[1]
    ch = min(ROW_COPY_CHUNK, n)
    with_zero = zero_lo is not None
    assert n % ch == 0 and (n_out % fill_rows == 0 or not with_zero)
    if not with_zero:
        zero_lo = zero_hi = fill_from = jnp.zeros((1,), I32)
    nz = zero_lo.shape[0]
    n_fill_blocks = n_out // fill_rows
    zeros = jnp.zeros((fill_rows, D), F32)

    def kern(si_ref, di_ref, zl_ref, zh_ref, ff_ref, src_ref, z_ref, dst_ref, sem):
        c = pl.program_id(0)

        def row(i):
            k = c * ch + i
            return pltpu.make_async_copy(src_ref.at[pl.ds(si_ref[k], 1)], dst_ref.at[pl.ds(di_ref[k], 1)], sem.at[0])

        def zero(r):
            return pltpu.make_async_copy(z_ref.at[pl.ds(0, 1)], dst_ref.at[pl.ds(r, 1)], sem.at[0])

        def fill(blk):
            return pltpu.make_async_copy(z_ref, dst_ref.at[pl.ds(blk * fill_rows, fill_rows)], sem.at[0])

        def each(fn, lo, hi):
            lax.fori_loop(lo, hi, lambda i, carry: (fn(i), carry)[1], 0)

        each(lambda i: row(i).start(), 0, ch)
        each(lambda i: row(i).wait(), 0, ch)
        if with_zero:
            @pl.when(c == 0)
            def _():
                for e in range(nz):
                    each(lambda r: zero(r).start(), zl_ref[e], zh_ref[e])
                for e in range(nz):
                    each(lambda r: zero(r).wait(), zl_ref[e], zh_ref[e])
                each(lambda blk: fill(blk).start(), ff_ref[0], n_fill_blocks)
                each(lambda blk: fill(blk).wait(), ff_ref[0], n_fill_blocks)

    gs = pltpu.PrefetchScalarGridSpec(
        num_scalar_prefetch=5, grid=(n // ch,),
        in_specs=[pl.BlockSpec(memory_space=pl.ANY), pl.BlockSpec(memory_space=pl.ANY)],
        out_specs=pl.BlockSpec(memory_space=pl.ANY),
        scratch_shapes=[pltpu.SemaphoreType.DMA((1,))],
    )
    return pl.pallas_call(
        kern, grid_spec=gs, out_shape=jax.ShapeDtypeStruct((n_out, D), F32),
        compiler_params=_cp(("arbitrary",)), name="row_copy",
    )(si, di, zero_lo, zero_hi, fill_from, src, zeros)


def moe_experts(xs, bexp, nused, w_gate, w_up, w_down, layer, rows, tf=256):
    n_rows, D = xs.shape
    n_blocks = n_rows // rows
    FF = w_gate.shape[3]
    nf = FF // tf

    def kern(bexp_ref, nused_ref, x_ref, wg_ref, wu_ref, wd_ref, o_ref):
        blk, f = pl.program_id(0), pl.program_id(1)

        @pl.when(blk < nused_ref[0])
        def _():
            @pl.when(f == 0)
            def _():
                o_ref[...] = jnp.zeros_like(o_ref)

            x = x_ref[...]
            hg = _dot(x, wg_ref[...])
            hu = _dot(x, wu_ref[...])
            hid = hg * _sigmoid(hg) * hu
            o_ref[...] += _dot(hid, wd_ref[...])

        @pl.when((blk >= nused_ref[0]) & (f == 0))
        def _():
            o_ref[...] = jnp.zeros_like(o_ref)

    def live(b, nu):
        return jnp.maximum(jnp.minimum(b, nu[0] - 1), 0)

    def feff(b, f, nu):
        return jnp.where(b < nu[0], f, nf - 1)

    xspec = pl.BlockSpec((rows, D), lambda b, f, be, nu: (live(b, nu), 0))
    gs = pltpu.PrefetchScalarGridSpec(
        num_scalar_prefetch=2, grid=(n_blocks, nf),
        in_specs=[xspec,
                  pl.BlockSpec((None, None, D, tf), lambda b, f, be, nu: (layer, be[live(b, nu)], 0, feff(b, f, nu))),
                  pl.BlockSpec((None, None, D, tf), lambda b, f, be, nu: (layer, be[live(b, nu)], 0, feff(b, f, nu))),
                  pl.BlockSpec((None, None, tf, D), lambda b, f, be, nu: (layer, be[live(b, nu)], feff(b, f, nu), 0))],
        out_specs=pl.BlockSpec((rows, D), lambda b, f, be, nu: (b, 0)),
    )
    return pl.pallas_call(
        kern, grid_spec=gs, out_shape=jax.ShapeDtypeStruct((n_rows, D), F32),
        compiler_params=_cp(("arbitrary", "arbitrary")), name="moe_experts",
    )(bexp, nused, xs, w_gate, w_up, w_down)


def moe_combine_ln(x, ytok, gate, ln_g, ln_b, idx, alpha, n_tok, tm=128):
    D = x.shape[1]
    tm = min(tm, n_tok)

    def kern(x_ref, y_ref, gt_ref, g_ref, b_ref, o_ref):
        gt = _bf16r(gt_ref[...])
        mix = gt[:, 0:1] * _bf16r(y_ref[:, 0:D]) + gt[:, 1:2] * _bf16r(y_ref[:, D:2 * D])
        o_ref[...] = _layer_norm(alpha * x_ref[...] + mix, g_ref[...], b_ref[...])

    row = pl.BlockSpec((tm, D), lambda i: (i, 0))
    par = pl.BlockSpec((None, 1, D), lambda i: (idx, 0, 0))
    return pl.pallas_call(
        kern, grid=(n_tok // tm,),
        in_specs=[row, pl.BlockSpec((tm, 2 * D), lambda i: (i, 0)),
                  pl.BlockSpec((tm, MOE_TOP_K), lambda i: (i, 0)), par, par],
        out_specs=row, out_shape=jax.ShapeDtypeStruct((n_tok, D), F32),
        compiler_params=_cp(("parallel",)), name="moe_combine_ln",
    )(x, ytok, gate, ln_g, ln_b)


def hier_moe_ln(x, n_tok, w_rt, b_rt, w_gate, w_up, w_down, layer, ln_g, ln_b, alpha, rows):
    N, D = x.shape
    xr = x if N % LANE == 0 else jnp.pad(x, ((0, LANE - N % LANE), (0, 0)))
    eidx, gate = moe_router(xr, w_rt, b_rt)
    eidx, gate = eidx[:, :n_tok], gate[:, :n_tok]
    A = n_tok * MOE_TOP_K
    e = eidx.T.reshape(A)
    onehot = (e[:, None] == jnp.arange(MOE_EXPERTS, dtype=I32)[None, :]).astype(I32)
    csum = jnp.cumsum(onehot, axis=0)
    counts = csum[-1]
    pos = jnp.sum(csum * onehot, axis=1) - 1
    padded = (counts + rows - 1) // rows * rows
    pad_end = jnp.cumsum(padded)
    pad_start = pad_end - padded
    dest = (jnp.sum(onehot * pad_start[None, :], axis=1) + pos).astype(I32)
    n_blocks = min(A // rows + MOE_EXPERTS, A)
    bexp = jnp.minimum(jnp.searchsorted(pad_end, jnp.arange(n_blocks, dtype=I32) * rows, side="right"),
                       MOE_EXPERTS - 1).astype(I32)
    nused = (pad_end[-1:] // rows).astype(I32)
    tok = jnp.arange(A, dtype=I32) // MOE_TOP_K
    xs = row_copy(x, n_blocks * rows, tok, dest, (pad_start + counts).astype(I32), pad_end.astype(I32),
                  fill_rows=rows, fill_from=nused)
    ys = moe_experts(xs, bexp, nused, w_gate, w_up, w_down, layer, rows)
    ytok = row_copy(ys, A, dest, jnp.arange(A, dtype=I32)).reshape(n_tok, MOE_TOP_K * D)
    return moe_combine_ln(x, ytok, gate.T, ln_g, ln_b, layer * 3 + 2, alpha, n_tok)


def _rope_tables(pos):
    half = HEAD_DIM // 2
    inv_freq = ROPE_THETA ** (-jnp.arange(half, dtype=F32) / half)
    ang = pos.astype(F32)[:, None] * inv_freq[None, :]
    cos, sin = jnp.cos(ang), jnp.sin(ang)
    return jnp.concatenate([cos, cos], axis=1), jnp.concatenate([-sin, sin], axis=1)


def _even_projections(x, pos_rows, w_in, w_rest, w_gates, e, tm):
    G, d = NSA_G, HEAD_DIM
    q_cols = G * NSA_R * d
    kv_cols = N_KV_STREAMS * G * d
    cos, sin = _rope_tables(pos_rows)
    tn = G * d
    mm = functools.partial(matmul, x, tm=tm)
    q = mm(w_in, layer=e, col_off=0, n_out=q_cols, tn=tn, rope=(cos, sin, tuple(range(q_cols // tn))))
    kv = mm(w_in, layer=e, col_off=q_cols, n_out=kv_cols, tn=tn, rope=(cos, sin, (0, 2)))
    win = mm(w_in, layer=e, col_off=q_cols + kv_cols, n_out=2 * G * d, tn=tn, rope=(cos, sin, (0,)))
    gates = mm(w_gates, tn=G * LANE)
    rw = w_rest.shape[1] // 2
    u = mm(w_rest, n_out=rw)
    gb = mm(w_rest, col_off=rw, n_out=rw)
    return q, kv, win, gates, u, gb


def _pad_rows(a, n):
    return jnp.pad(a, ((0, n - a.shape[0]),) + ((0, 0),) * (a.ndim - 1))


def kernel(x_prompt, x_sample, mem_prompt, cache_nsa_kv, cache_nsa_win, state_rglru_h, state_rglru_conv,
           state_pool, cache_mem_kv, page_table, w_in, w_out, cmp_w1, cmp_w2, cmp_pe, conv_w, conv_b,
           lru_wa, lru_ba, lru_wx, lru_bx, lru_lambda, pool_w, pool_scale, mem_wq, mem_wk, mem_wv, mem_wo,
           ln_g, ln_b, moe_w_coarse, moe_b_coarse, moe_w_fine, moe_b_fine, moe_w_gate, moe_w_up, moe_w_down):
    B, T, D = x_prompt.shape
    DB = x_sample.shape[0]
    depth = ln_g.shape[0]
    n_pages = page_table.shape[1]
    past_len = n_pages * PAGE_SIZE
    n_phys = cache_nsa_kv.shape[1]
    G, R, d = NSA_G, NSA_R, HEAD_DIM
    alpha = (2 * depth) ** 0.25
    q_cols, kv_cols, win_cols, gate_cols = G * R * d, N_KV_STREAMS * G * d, 2 * G * d, 3 * G * R
    rest_off = q_cols + kv_cols + win_cols + gate_cols
    RW = (w_in.shape[2] - rest_off) // 2
    MW = mem_wq.shape[2]
    M = mem_prompt.shape[1]
    SR = SAMPLE_ROWS
    TS = SUBLANE

    xp = x_prompt.reshape(B * T, D)
    xs = _pad_rows(x_sample.reshape(DB, D), SR)
    lng = ln_g.reshape(depth * 3, 1, D)
    lnb = ln_b.reshape(depth * 3, 1, D)
    memf = mem_prompt.reshape(B * M, D)

    kv_p, kv_s, win_p, win_s, h_p, h_s, cv_p, cv_s, pl_p, pl_s, mem_p = ([] for _ in range(11))
    for layer in range(depth):
        if layer % 2 == 0:
            e = layer // 2
            wg = w_in[e, :, q_cols + kv_cols + win_cols:rest_off].reshape(D, G, 3 * R)
            wg = jnp.pad(wg, ((0, 0), (0, 0), (0, LANE - 3 * R))).reshape(D, G * LANE)
            w_rest = w_in[e, :, rest_off:]

            q, kv, win, gates, u, gb = _even_projections(
                xp, jnp.arange(T), w_in, w_rest, wg, e, 1024)
            new_kv = kv.reshape(B, T, N_KV_STREAMS, G, d)
            pages_p = new_kv.reshape(B * T // PAGE_SIZE, PAGE_SIZE, N_KV_STREAMS, G, d)
            ab = compress_ab(pages_p, jnp.arange(B * T // PAGE_SIZE, dtype=I32), cmp_w1, cmp_pe, e, B,
                             T // PAGE_SIZE)
            kvcmp = compress_fin(ab, cmp_w2, e)
            o_nsa = nsa_prompt(q, kv, win, kvcmp, gates, B, T)
            y_rnn, h_all = rglru(u, gb, jnp.zeros((B, SUBLANE, RW), F32), jnp.zeros((B, 1, RW), F32),
                                 conv_w, conv_b, lru_wa, lru_ba, lru_wx, lru_bx, lru_lambda, e, B, T)
            mix_p = matmul(o_nsa, w_out, layer=e, x2=y_rnn)
            keep = min(WINDOW, T)
            kv_p.append(new_kv)
            win_p.append(win.reshape(B, T, 2, G, d)[:, T - keep:])
            h_p.append(h_all.reshape(B, T, RW)[:, T - 1])
            cv_p.append(u.reshape(B, T, RW)[:, T - (CONV_WIDTH - 1):])

            q, kv, win, gates, u, gb = _even_projections(
                xs, jnp.full((SR,), past_len), w_in, w_rest, wg, e, SR)
            new_kv_s = kv[:DB].reshape(DB, N_KV_STREAMS, G, d)
            new_win_s = win[:DB].reshape(DB, 2, G, d)
            pages_s = cache_nsa_kv.reshape(cache_nsa_kv.shape[0] * n_phys, PAGE_SIZE, N_KV_STREAMS, G, d)
            pid = (page_table.reshape(DB * n_pages) + e * n_phys).astype(I32)
            ab = compress_ab(pages_s, pid, cmp_w1, cmp_pe, e, DB, n_pages)
            kvcmp = compress_fin(ab, cmp_w2, e)
            q8 = jnp.pad(q[:DB].reshape(DB, G, R, d), ((0, 0), (0, 0), (0, SUBLANE - R), (0, 0)))
            o_cmp, imp = nsa_sample_cmp(q8, kvcmp, past_len)
            idx_t = nsa_sample_topk(_pad_rows(imp.reshape(DB * SUBLANE, -1), LANE), past_len)
            n_top = idx_t.shape[0]
            idx = idx_t[:, :DB * SUBLANE].T.reshape(DB, SUBLANE, n_top)[:, :G].reshape(DB * G * n_top)
            o8 = nsa_sample_attend(q8, o_cmp, gates[:DB].reshape(DB, G, 1, LANE), idx, pid, pages_s,
                                   new_kv_s, cache_nsa_win, new_win_s, e, past_len)
            o_nsa = _pad_rows(o8[:, :, :R].reshape(DB, G * R * d), SR)
            conv0 = state_rglru_conv[e]
            u_t = jnp.pad(u[:DB, None, :], ((0, 0), (0, TS - 1), (0, 0))).reshape(DB * TS, RW)
            gb_t = jnp.pad(gb[:DB, None, :], ((0, 0), (0, TS - 1), (0, 0))).reshape(DB * TS, RW)
            c0 = jnp.pad(conv0, ((0, 0), (SUBLANE - (CONV_WIDTH - 1), 0), (0, 0)))
            y_t, h_t = rglru(u_t, gb_t, c0, state_rglru_h[e][:, None, :], conv_w, conv_b, lru_wa, lru_ba,
                             lru_wx, lru_bx, lru_lambda, e, DB, TS)
            y_rnn = _pad_rows(y_t.reshape(DB, TS, RW)[:, 0], SR)
            mix_s = matmul(o_nsa, w_out, layer=e, x2=y_rnn, tm=SR)
            keep = min(WINDOW, past_len + 1)
            win_all = jnp.concatenate([cache_nsa_win[e], new_win_s[:, None]], axis=1)
            kv_s.append(new_kv_s[:, None])
            win_s.append(win_all[:, win_all.shape[1] - keep:])
            h_s.append(h_t.reshape(DB, TS, RW)[:, 0])
            cv_s.append(jnp.concatenate([conv0, u[:DB, None, :]], axis=1)[:, 1:])
        else:
            o = layer // 2
            mix_p = pool_mixer(xp, jnp.zeros((B, 2 * SUBLANE, D), F32), pool_w, pool_scale, o, B, T, 0)
            pl_p.append(xp.reshape(B, T, D)[:, T - POOL_BUF:])
            tsp = 2 * SUBLANE
            x_t = jnp.pad(xs[:DB, None, :], ((0, 0), (0, tsp - 1), (0, 0))).reshape(DB * tsp, D)
            buf = jnp.pad(state_pool[o], ((0, 0), (tsp - POOL_BUF, 0), (0, 0)))
            mix_t = pool_mixer(x_t, buf, pool_w, pool_scale, o, DB, tsp, past_len)
            mix_s = _pad_rows(mix_t.reshape(DB, tsp, D)[:, 0], SR)
            pl_s.append(jnp.concatenate([state_pool[o], xs[:DB, None, :]], axis=1)[:, 1:])

        xp = add_ln(xp, mix_p, lng, lnb, layer * 3, alpha)
        xs = add_ln(xs, mix_s, lng, lnb, layer * 3, alpha)

        w_kv = jnp.concatenate([mem_wk[layer], mem_wv[layer]], axis=1)
        mkv = matmul(memf, w_kv).reshape(B, M, 2, MW)
        mem_p.append(mkv)
        qm = matmul(xp, mem_wq, layer=layer)
        att = mem_attn(qm, mkv, B, T)
        xp = add_ln(xp, matmul(att, mem_wo, layer=layer), lng, lnb, layer * 3 + 1, alpha)
        qm = matmul(xs, mem_wq, layer=layer, tm=SR)
        qm_t = jnp.pad(qm[:DB, None, :], ((0, 0), (0, TS - 1), (0, 0))).reshape(DB * TS, MW)
        att_t = mem_attn(qm_t, cache_mem_kv.reshape((-1,) + cache_mem_kv.shape[2:]), DB, TS,
                         kv_off=layer * DB)
        att = _pad_rows(att_t.reshape(DB, TS, MW)[:, 0], SR)
        xs = add_ln(xs, matmul(att, mem_wo, layer=layer, tm=SR), lng, lnb, layer * 3 + 1, alpha)

        w_rt = _pad_rows(jnp.concatenate([moe_w_coarse[layer], moe_w_fine[layer]], axis=1).T, ROUTER_ROWS)
        b_rt = _pad_rows(jnp.concatenate([moe_b_coarse[layer], moe_b_fine[layer]])[:, None], ROUTER_ROWS)
        xp = hier_moe_ln(xp, B * T, w_rt, b_rt, moe_w_gate, moe_w_up, moe_w_down, layer, lng, lnb, alpha,
                         rows=256)
        xs = _pad_rows(hier_moe_ln(xs, DB, w_rt, b_rt, moe_w_gate, moe_w_up, moe_w_down, layer, lng, lnb,
                                   alpha, rows=SUBLANE), SR)

    return (xp.reshape(B, T, D), xs[:DB].reshape(DB, 1, D), jnp.stack(kv_p), jnp.stack(kv_s),
            jnp.stack(win_p), jnp.stack(win_s), jnp.stack(h_p), jnp.stack(h_s), jnp.stack(cv_p),
            jnp.stack(cv_s), jnp.stack(pl_p), jnp.stack(pl_s), jnp.stack(mem_p))
```

```python
import functools
import math

import jax
import jax.numpy as jnp
from jax import lax
from jax.experimental import pallas as pl
from jax.experimental.pallas import tpu as pltpu

F32 = jnp.float32
BF16 = jnp.bfloat16
I32 = jnp.int32

HEAD_DIM = 128
NSA_G = 4
NSA_R = 4
N_KV_STREAMS = 4
CMP_BLOCK = 32
CMP_STRIDE = 16
SEL_BLOCK = 64
SEL_TOP_N = 16
WINDOW = 512
FORCE_SCORE = 1e9
NEG = -1e30
ROPE_THETA = 10000.0
CONV_WIDTH = 4
LRU_C = 8.0
RNN_BLOCK_W = 128
POOL_WINDOWS = (2, 4, 8, 16)
POOL_BUF = 15
MEM_HEADS = 4
MOE_GROUPS = 4
MOE_EPG = 8
MOE_EXPERTS = MOE_GROUPS * MOE_EPG
MOE_TOP_K = 2
LN_EPS = 1e-5
PAGE_SIZE = 128

LANE = 128
SUBLANE = 8
VMEM_LIMIT = 56 * 1024 * 1024
SAMPLE_ROWS = 16


def _cp(sem, vmem=VMEM_LIMIT):
    return pltpu.CompilerParams(dimension_semantics=sem, vmem_limit_bytes=vmem)


def _split3(x):
    h = x.astype(BF16)
    r = x - h.astype(F32)
    m = r.astype(BF16)
    l = (r - m.astype(F32)).astype(BF16)
    return h, m, l


def _mxu(a, b, nt):
    if nt:
        return lax.dot_general(a, b, (((1,), (1,)), ((), ())), preferred_element_type=F32)
    return jnp.dot(a, b, preferred_element_type=F32)


def _dot(a, b, nt=False):
    return _mxu(a.astype(BF16), b.astype(BF16), nt)


def _bf16r(x):
    return x.astype(BF16).astype(F32)


def _sigmoid(x):
    return 1.0 / (1.0 + jnp.exp(-x))


def _gelu(x):
    return 0.5 * x * (1.0 + jnp.tanh(0.7978845608028654 * (x + 0.044715 * (x * x * x))))


def _msoftmax(s, mask):
    s = jnp.where(mask, s, NEG)
    m = jnp.max(s, axis=-1, keepdims=True)
    p = jnp.where(mask, jnp.exp(s - m), 0.0)
    return p / jnp.maximum(jnp.sum(p, axis=-1, keepdims=True), 1e-30)


def _layer_norm(v, g, b):
    mu = jnp.mean(v, axis=-1, keepdims=True)
    c = v - mu
    var = jnp.mean(c * c, axis=-1, keepdims=True)
    return c * lax.rsqrt(var + LN_EPS) * g + b


def matmul(x, w, *, layer=0, col_off=0, n_out=None, x2=None, tm=1024, tn=1024, tk=512, rope=None):
    if w.ndim == 2:
        w = w[None]
    M, K1 = x.shape
    K2 = 0 if x2 is None else x2.shape[1]
    K = K1 + K2
    assert w.shape[1] == K
    n_out = w.shape[2] - col_off if n_out is None else n_out
    tm, tn, tk = min(tm, M), min(tn, n_out), min(tk, K1)
    assert M % tm == 0 and n_out % tn == 0 and K1 % tk == 0 and K2 % tk == 0 and col_off % tn == 0
    nk1, nk = K1 // tk, K // tk
    joff = col_off // tn
    grid = (M // tm, n_out // tn, nk)

    in_specs = [pl.BlockSpec((tm, tk), lambda i, j, k: (i, jnp.minimum(k, nk1 - 1)))]
    args = [x]
    if x2 is not None:
        in_specs.append(pl.BlockSpec((tm, tk), lambda i, j, k: (i, jnp.maximum(k - nk1, 0))))
        args.append(x2)
    in_specs.append(pl.BlockSpec((None, tk, tn), lambda i, j, k: (layer, k, j + joff)))
    args.append(w)
    rope_blocks = ()
    if rope is not None:
        cos, sin, rope_blocks = rope
        period = cos.shape[0] // tm
        assert cos.shape[0] % tm == 0
        for t in (cos, sin):
            in_specs.append(pl.BlockSpec((tm, LANE), lambda i, j, k: (i % period, 0)))
            args.append(t)

    def kern(*refs):
        refs = list(refs)
        x_ref = refs.pop(0)
        x2_ref = refs.pop(0) if x2 is not None else None
        w_ref = refs.pop(0)
        cos_ref, sin_ref = (refs.pop(0), refs.pop(0)) if rope is not None else (None, None)
        o_ref, acc = refs
        j, k = pl.program_id(1), pl.program_id(2)

        @pl.when(k == 0)
        def _():
            acc[...] = jnp.zeros_like(acc)

        def accum(xr):
            acc[...] += _dot(xr[...], w_ref[...])

        if x2 is None:
            accum(x_ref)
        else:
            pl.when(k < nk1)(lambda: accum(x_ref))
            pl.when(k >= nk1)(lambda: accum(x2_ref))

        @pl.when(k == nk - 1)
        def _():
            if rope is None:
                o_ref[...] = acc[...]
                return
            rot = functools.reduce(jnp.logical_or, [j == c for c in rope_blocks])

            @pl.when(rot)
            def _():
                c, s = cos_ref[...], sin_ref[...]
                for h in range(tn // LANE):
                    seg = acc[:, h * LANE:(h + 1) * LANE]
                    o_ref[:, h * LANE:(h + 1) * LANE] = seg * c + pltpu.roll(seg, LANE // 2, 1) * s

            @pl.when(jnp.logical_not(rot))
            def _():
                o_ref[...] = acc[...]

    return pl.pallas_call(
        kern,
        grid=grid,
        in_specs=in_specs,
        out_specs=pl.BlockSpec((tm, tn), lambda i, j, k: (i, j)),
        out_shape=jax.ShapeDtypeStruct((M, n_out), F32),
        scratch_shapes=[pltpu.VMEM((tm, tn), F32)],
        compiler_params=_cp(("parallel", "parallel", "arbitrary")),
        name="matmul",
    )(*args)


def _store_token_major(o3_ref, v):
    for c in range(v.shape[1] // LANE):
        o3_ref[:, c, :] = v[:, c * LANE:(c + 1) * LANE]


def add_ln(x, f, ln_g, ln_b, idx, alpha, tm=256, token_major=False):
    M, D = x.shape
    tm = min(tm, M)

    def kern(x_ref, f_ref, g_ref, b_ref, o_ref, *o3_ref):
        y = _layer_norm(alpha * x_ref[...] + f_ref[...], g_ref[...], b_ref[...])
        o_ref[...] = y
        if token_major:
            _store_token_major(o3_ref[0], y)

    row = pl.BlockSpec((tm, D), lambda i: (i, 0))
    par = pl.BlockSpec((None, 1, D), lambda i: (idx, 0, 0))
    out_specs, out_shape = [row], [jax.ShapeDtypeStruct((M, D), F32)]
    if token_major:
        out_specs.append(pl.BlockSpec((tm, D // LANE, LANE), lambda i: (i, 0, 0)))
        out_shape.append(jax.ShapeDtypeStruct((M, D // LANE, LANE), F32))
    out = pl.pallas_call(
        kern, grid=(M // tm,), in_specs=[row, row, par, par], out_specs=out_specs,
        out_shape=out_shape, compiler_params=_cp(("parallel",)), name="add_ln",
    )(x, f, ln_g, ln_b)
    return out if token_major else out[0]


CMP_PAGES = 8


def compress_ab(pages, page_ids, cmp_w1, cmp_pe, e, n_batch, pages_per_seq):
    P = CMP_PAGES
    assert pages_per_seq % P == 0
    cpp = PAGE_SIZE // CMP_STRIDE
    steps = pages_per_seq // P
    d = HEAD_DIM
    n_chunks = pages_per_seq * cpp
    rows = NSA_G * P * cpp

    def kern(pid_ref, *refs):
        page_refs, w_ref, pe_ref, o_ref = refs[:P], refs[P], refs[P + 1], refs[P + 2]
        for s in range(2):
            acc_a = jnp.zeros((rows, d), F32)
            acc_b = jnp.zeros((rows, d), F32)
            for l in range(CMP_STRIDE):
                lhs = jnp.concatenate(
                    [page_refs[i][pl.ds(l, cpp, stride=CMP_STRIDE), s, g, :]
                     for g in range(NSA_G) for i in range(P)], axis=0)
                acc_a = acc_a + _dot(lhs + pe_ref[s, l:l + 1, :], w_ref[s, l])
                lb = CMP_STRIDE + l
                acc_b = acc_b + _dot(lhs + pe_ref[s, lb:lb + 1, :], w_ref[s, lb])
            for g in range(NSA_G):
                c0 = (s * NSA_G + g) * 2 * d
                o_ref[:, c0:c0 + d] = acc_a[g * P * cpp:(g + 1) * P * cpp]
                o_ref[:, c0 + d:c0 + 2 * d] = acc_b[g * P * cpp:(g + 1) * P * cpp]

    def page_spec(i):
        return pl.BlockSpec((None, PAGE_SIZE, 2, NSA_G, d),
                            lambda b, st, pid: (pid[(b * steps + st) * P + i], 0, 0, 0, 0))

    gs = pltpu.PrefetchScalarGridSpec(
        num_scalar_prefetch=1,
        grid=(n_batch, steps),
        in_specs=[page_spec(i) for i in range(P)]
        + [pl.BlockSpec((None, 2, CMP_BLOCK, d, d), lambda b, st, pid: (e, 0, 0, 0, 0)),
           pl.BlockSpec((None, 2, CMP_BLOCK, d), lambda b, st, pid: (e, 0, 0, 0))],
        out_specs=pl.BlockSpec((None, P * cpp, 2 * NSA_G * 2 * d), lambda b, st, pid: (b, st, 0)),
    )
    return pl.pallas_call(
        kern, grid_spec=gs,
        out_shape=jax.ShapeDtypeStruct((n_batch, n_chunks, 2 * NSA_G * 2 * d), F32),
        compiler_params=_cp(("parallel", "parallel")), name="compress_ab",
    )(page_ids, *([pages] * P), cmp_w1, cmp_pe)


def compress_fin(ab, cmp_w2, e):
    nb, nch, _ = ab.shape
    d = HEAD_DIM

    def kern(ab_ref, w2_ref, o_ref):
        for g in range(NSA_G):
            a = ab_ref[:, g * 2 * d:g * 2 * d + d]
            bn = ab_ref[:, g * 2 * d + d:(g + 1) * 2 * d]
            h = a + pltpu.roll(bn, nch - 1, 0)
            o_ref[g] = _dot(_gelu(h), w2_ref[...])

    return pl.pallas_call(
        kern, grid=(nb, 2),
        in_specs=[pl.BlockSpec((None, nch, NSA_G * 2 * d), lambda b, s: (b, 0, s)),
                  pl.BlockSpec((None, None, d, d), lambda b, s: (e, s, 0, 0))],
        out_specs=pl.BlockSpec((None, None, NSA_G, nch, d), lambda b, s: (s, b, 0, 0, 0)),
        out_shape=jax.ShapeDtypeStruct((2, nb, NSA_G, nch, d), F32),
        compiler_params=_cp(("parallel", "parallel")), name="compress_fin",
    )(ab, cmp_w2)


def _overlap_matrix(n_rows, n_cols):
    ci = jnp.arange(n_rows)[:, None] * CMP_STRIDE
    sj = jnp.arange(n_cols)[None, :] * SEL_BLOCK
    return ((ci < sj + SEL_BLOCK) & (ci + CMP_BLOCK > sj)).astype(BF16)


NSA_TQ = 128


def nsa_prompt(q, kv, win, kvcmp, gates, B, T):
    d, G, R, tq = HEAD_DIM, NSA_G, NSA_R, NSA_TQ
    nq = T // tq
    ncp = kvcmp.shape[3]
    n_cmp = T // CMP_STRIDE - 1
    n_sel = -(-T // SEL_BLOCK)
    n_top = min(SEL_TOP_N, n_sel)
    wk = min(T, WINDOW + tq)
    assert n_sel <= LANE and T % tq == 0
    scale = d ** -0.5
    overlap = _overlap_matrix(ncp, LANE)
    expand = (jnp.arange(LANE)[:, None] == (jnp.arange(T) // SEL_BLOCK)[None, :]).astype(BF16)

    def kern(q_ref, ks_ref, vs_ref, kw_ref, vw_ref, kc_ref, vc_ref, g_ref, ov_ref, ex_ref, o_ref,
             ksb, vsb, kwb, vwb, kcb, vcb):
        qi = pl.program_id(2)

        @pl.when(qi == 0)
        def _():
            ksb[...] = ks_ref[...].astype(BF16)
            vsb[...] = vs_ref[...].astype(BF16)
            kwb[...] = kw_ref[...].astype(BF16)
            vwb[...] = vw_ref[...].astype(BF16)
            kcb[...] = kc_ref[...].astype(BF16)
            vcb[...] = vc_ref[...].astype(BF16)

        t0 = qi * tq
        qb = q_ref[...]
        qs = jnp.concatenate([qb[:, r * d:(r + 1) * d] for r in range(R)], axis=0).astype(BF16)
        qpos_s = t0 + (lax.broadcasted_iota(I32, (R * tq, 1), 0) & (tq - 1))

        s = _mxu(qs, kcb[...], True) * scale
        n_idx = lax.broadcasted_iota(I32, (1, ncp), 1)
        cmask = (n_idx * CMP_STRIDE + (CMP_BLOCK - 1) <= qpos_s) & (n_idx < n_cmp)
        p = _msoftmax(s, cmask)
        pb = p.astype(BF16)
        o_cmp = _mxu(pb, vcb[...], False)
        imp_r = _mxu(pb, ov_ref[...], False)
        imp = imp_r[0:tq]
        for r in range(1, R):
            imp = imp + imp_r[r * tq:(r + 1) * tq]

        blk = lax.broadcasted_iota(I32, (tq, LANE), 1)
        qpos = t0 + lax.broadcasted_iota(I32, (tq, LANE), 0)
        cur = qpos // SEL_BLOCK
        valid = blk <= cur
        forced = (blk == 0) | (blk == cur) | (blk == cur - 1)
        score = jnp.where(forced, FORCE_SCORE, jnp.where(valid, imp, NEG))
        cnt = jnp.zeros((tq, LANE), I32)
        for i in range(n_sel):
            col = score[:, i:i + 1]
            beats = (col > score) | ((col == score) & (blk > i))
            cnt = cnt + beats.astype(I32)
        sel = ((cnt < n_top) & valid).astype(BF16)
        selx = _mxu(sel, ex_ref[...], False)

        kpos = lax.broadcasted_iota(I32, (tq, T), 1)
        qpos_t = t0 + lax.broadcasted_iota(I32, (tq, T), 0)
        smask = (selx > 0.5) & (kpos <= qpos_t)

        kstart = pl.multiple_of(jnp.clip(t0 - WINDOW, 0, T - wk), tq)
        kwin = kwb[pl.ds(kstart, wk), :]
        vwin = vwb[pl.ds(kstart, wk), :]
        s_w = _mxu(qs, kwin, True) * scale
        kp = kstart + lax.broadcasted_iota(I32, (1, wk), 1)
        wmask = (kp <= qpos_s) & (kp > qpos_s - WINDOW)
        p_w = _msoftmax(s_w, wmask)
        o_win = _mxu(p_w.astype(BF16), vwin, False)

        gt = _sigmoid(g_ref[...])
        for r in range(R):
            s_r = _mxu(qs[r * tq:(r + 1) * tq], ksb[...], True) * scale
            p_r = _msoftmax(s_r, smask)
            o_slc = _mxu(p_r.astype(BF16), vsb[...], False)
            o_r = (gt[:, 3 * r:3 * r + 1] * o_cmp[r * tq:(r + 1) * tq]
                   + gt[:, 3 * r + 1:3 * r + 2] * o_slc
                   + gt[:, 3 * r + 2:3 * r + 3] * o_win[r * tq:(r + 1) * tq])
            o_ref[:, r * d:(r + 1) * d] = o_r

    seq = lambda blkcol: pl.BlockSpec((T, d), lambda b, g, qi: (b, blkcol(g)))
    cmp_spec = lambda s: pl.BlockSpec((None, None, None, ncp, d), lambda b, g, qi: (s, b, g, 0, 0))
    return pl.pallas_call(
        kern, grid=(B, G, nq),
        in_specs=[pl.BlockSpec((tq, R * d), lambda b, g, qi: (b * nq + qi, g)),
                  seq(lambda g: 2 * G + g), seq(lambda g: 3 * G + g),
                  seq(lambda g: g), seq(lambda g: G + g),
                  cmp_spec(0), cmp_spec(1),
                  pl.BlockSpec((tq, LANE), lambda b, g, qi: (b * nq + qi, g)),
                  pl.BlockSpec((ncp, LANE), lambda b, g, qi: (0, 0)),
                  pl.BlockSpec((LANE, T), lambda b, g, qi: (0, 0))],
        out_specs=pl.BlockSpec((tq, R * d), lambda b, g, qi: (b * nq + qi, g)),
        out_shape=jax.ShapeDtypeStruct((B * T, G * R * d), F32),
        scratch_shapes=[pltpu.VMEM((T, d), BF16)] * 4 + [pltpu.VMEM((ncp, d), BF16)] * 2,
        compiler_params=_cp(("parallel", "parallel", "arbitrary")), name="nsa_prompt",
    )(q, kv, kv, win, win, kvcmp, kvcmp, gates, overlap, expand)


def nsa_sample_cmp(q8, kvcmp, q_pos):
    B = q8.shape[0]
    d, G, R = HEAD_DIM, NSA_G, NSA_R
    nch = kvcmp.shape[3]
    n_cmp = nch - 1
    n_sel = -(-(q_pos + 1) // SEL_BLOCK)
    nsp = -(-n_sel // LANE) * LANE
    overlap = _overlap_matrix(nch, nsp)
    scale = d ** -0.5

    def kern(q_ref, k_ref, v_ref, ov_ref, o_ref, imp_ref):
        n_idx = lax.broadcasted_iota(I32, (1, nch), 1)
        cmask = (n_idx * CMP_STRIDE + (CMP_BLOCK - 1) <= q_pos) & (n_idx < n_cmp)
        rows = []
        for g in range(G):
            s = _dot(q_ref[g], k_ref[g], nt=True) * scale
            pb = _msoftmax(s, cmask).astype(BF16)
            o_ref[g] = _dot(pb, v_ref[g])
            rows.append(jnp.sum(_mxu(pb, ov_ref[...], False)[0:R], axis=0, keepdims=True))
        imp_ref[...] = jnp.concatenate(rows + [jnp.zeros((SUBLANE - G, nsp), F32)], axis=0)

    qspec = pl.BlockSpec((None, G, SUBLANE, d), lambda b: (b, 0, 0, 0))
    cspec = lambda s: pl.BlockSpec((None, None, G, nch, d), lambda b: (s, b, 0, 0, 0))
    return pl.pallas_call(
        kern, grid=(B,),
        in_specs=[qspec, cspec(0), cspec(1), pl.BlockSpec((nch, nsp), lambda b: (0, 0))],
        out_specs=[qspec, pl.BlockSpec((None, SUBLANE, nsp), lambda b: (b, 0, 0))],
        out_shape=[jax.ShapeDtypeStruct((B, G, SUBLANE, d), F32),
                   jax.ShapeDtypeStruct((B, SUBLANE, nsp), F32)],
        compiler_params=_cp(("parallel",)), name="nsa_sample_cmp",
    )(q8, kvcmp, kvcmp, overlap)


def nsa_sample_topk(imp, q_pos):
    nr, nsp = imp.shape
    assert nr == LANE
    n_sel = -(-(q_pos + 1) // SEL_BLOCK)
    n_top = min(SEL_TOP_N, n_sel)
    cur = q_pos // SEL_BLOCK

    def kern(imp_ref, o_ref, sct):
        blk = lax.broadcasted_iota(I32, (nr, nsp), 1)
        valid = blk <= cur
        forced = (blk == 0) | (blk == cur) | (blk == cur - 1)
        score = jnp.where(forced, FORCE_SCORE, jnp.where(valid, imp_ref[...], NEG))
        eye = (lax.broadcasted_iota(I32, (nsp, nsp), 0) == lax.broadcasted_iota(I32, (nsp, nsp), 1)).astype(BF16)
        h, m, l = _split3(score)
        sct[...] = (_mxu(eye, h, True) + _mxu(eye, m, True)) + _mxu(eye, l, True)
        sc = sct[...]
        jidx = lax.broadcasted_iota(I32, (nsp, nr), 0)

        def body(i, cnt):
            row = sct[pl.ds(i, 1), :]
            beats = (row > sc) | ((row == sc) & (jidx > i))
            return cnt + beats.astype(I32)

        cnt = lax.fori_loop(0, n_sel, body, jnp.zeros((nsp, nr), I32))
        for p in range(n_top):
            o_ref[p:p + 1, :] = jnp.sum(jnp.where(cnt == p, jidx, 0), axis=0, keepdims=True)

    return pl.pallas_call(
        kern, grid=(1,),
        in_specs=[pl.BlockSpec((nr, nsp), lambda i: (0, 0))],
        out_specs=pl.BlockSpec((n_top, nr), lambda i: (0, 0)),
        out_shape=jax.ShapeDtypeStruct((n_top, nr), I32),
        scratch_shapes=[pltpu.VMEM((nsp, nr), F32)],
        compiler_params=_cp(("arbitrary",)), name="nsa_sample_topk",
    )(imp)


def nsa_sample_attend(q8, o_cmp, gates, idx, page_ids, pages, new_kv, cwin, new_win, e, q_pos):
    B = q8.shape[0]
    d, G, R = HEAD_DIM, NSA_G, NSA_R
    n_top = idx.shape[0] // (B * G)
    n_pages = page_ids.shape[0] // B
    n_cache_blocks = n_pages * (PAGE_SIZE // SEL_BLOCK)
    bpp = PAGE_SIZE // SEL_BLOCK
    w0 = cwin.shape[2]
    scale = d ** -0.5
    nk = n_top * SEL_BLOCK

    def kern(idx_ref, pid_ref, q_ref, oc_ref, g_ref, pages_ref, nkv_ref, cw_ref, nw_ref, o_ref,
             kvbuf, sem):
        b = pl.program_id(0)

        def slot_copy(slot, j):
            page = pid_ref[b * n_pages + j // bpp]
            r0 = (j % bpp) * SEL_BLOCK
            return pltpu.make_async_copy(
                pages_ref.at[page, pl.ds(r0, SEL_BLOCK), pl.ds(2, 2)], kvbuf.at[slot], sem.at[0])

        for slot in range(G * n_top):
            j = idx_ref[b * G * n_top + slot]

            @pl.when(j < n_cache_blocks)
            def _():
                slot_copy(slot, j).start()

            @pl.when(j >= n_cache_blocks)
            def _():
                kvbuf[slot] = jnp.zeros((SEL_BLOCK, 2, G, d), F32)

            @pl.when(j * SEL_BLOCK == q_pos)
            def _():
                kvbuf[slot, 0] = nkv_ref[2:4]

        for slot in range(G * n_top):
            j = idx_ref[b * G * n_top + slot]

            @pl.when(j < n_cache_blocks)
            def _():
                slot_copy(slot, j).wait()

        lane_k = lax.broadcasted_iota(I32, (1, nk), 1)
        row8 = lax.broadcasted_iota(I32, (SUBLANE, 1), 0)
        for g in range(G):
            qg = q_ref[g]
            kk = jnp.concatenate([kvbuf[g * n_top + p, :, 0, g, :] for p in range(n_top)], axis=0)
            vv = jnp.concatenate([kvbuf[g * n_top + p, :, 1, g, :] for p in range(n_top)], axis=0)
            kpos = lane_k & (SEL_BLOCK - 1)
            for p in range(n_top):
                j = idx_ref[(b * G + g) * n_top + p]
                kpos = kpos + jnp.where(lane_k // SEL_BLOCK == p, j * SEL_BLOCK, 0)
            s = _dot(qg, kk, nt=True) * scale
            p_s = _msoftmax(s, kpos <= q_pos)
            o_slc = _dot(p_s, vv)
            kw = cw_ref[:, 0, g, :]
            vw = cw_ref[:, 1, g, :]
            s_w = _dot(qg, kw, nt=True) * scale
            wpos = (q_pos - w0) + lax.broadcasted_iota(I32, (1, w0), 1)
            wmask = (wpos >= 0) & (wpos <= q_pos) & (wpos > q_pos - WINDOW)
            s_w = jnp.where(wmask, s_w, NEG)
            s_n = jnp.sum(_bf16r(qg) * _bf16r(nw_ref[0, g:g + 1, :]), axis=-1, keepdims=True) * scale
            m = jnp.maximum(jnp.max(s_w, axis=-1, keepdims=True), s_n)
            p_w = jnp.where(wmask, jnp.exp(s_w - m), 0.0)
            p_n = jnp.exp(s_n - m)
            den = jnp.sum(p_w, axis=-1, keepdims=True) + p_n
            o_win = _dot(p_w / den, vw) + _bf16r(p_n / den) * _bf16r(nw_ref[1, g:g + 1, :])
            gt = _sigmoid(g_ref[g])
            gc = [jnp.zeros((SUBLANE, 1), F32)] * 3
            for r in range(R):
                for c in range(3):
                    gc[c] = jnp.where(row8 == r, gt[:, 3 * r + c:3 * r + c + 1], gc[c])
            o_ref[g] = gc[0] * oc_ref[g] + gc[1] * o_slc + gc[2] * o_win

    qspec = pl.BlockSpec((None, G, SUBLANE, d), lambda b, i, p: (b, 0, 0, 0))
    gs = pltpu.PrefetchScalarGridSpec(
        num_scalar_prefetch=2, grid=(B,),
        in_specs=[qspec, qspec,
                  pl.BlockSpec((None, G, 1, LANE), lambda b, i, p: (b, 0, 0, 0)),
                  pl.BlockSpec(memory_space=pl.ANY),
                  pl.BlockSpec((None, N_KV_STREAMS, G, d), lambda b, i, p: (b, 0, 0, 0)),
                  pl.BlockSpec((None, None, w0, 2, G, d), lambda b, i, p: (e, b, 0, 0, 0, 0)),
                  pl.BlockSpec((None, 2, G, d), lambda b, i, p: (b, 0, 0, 0))],
        out_specs=qspec,
        scratch_shapes=[pltpu.VMEM((G * n_top, SEL_BLOCK, 2, G, d), F32),
                        pltpu.SemaphoreType.DMA((1,))],
    )
    return pl.pallas_call(
        kern, grid_spec=gs, out_shape=jax.ShapeDtypeStruct((B, G, SUBLANE, d), F32),
        compiler_params=_cp(("arbitrary",)), name="nsa_sample_attend",
    )(idx, page_ids, q8, o_cmp, gates, pages, new_kv, cwin, new_win)


def rglru(u, gb, conv0, h0, conv_w, conv_b, lru_wa, lru_ba, lru_wx, lru_bx, lru_lambda, e, B, T):
    W = u.shape[1]
    cw = RNN_BLOCK_W
    nblk = W // cw
    tc = min(T, 256)
    nt = T // tc
    assert T % tc == 0 and tc % SUBLANE == 0

    def kern(u_ref, gb_ref, c0_ref, h0_ref, cw_ref, cb_ref, wa_ref, ba_ref, wx_ref, bx_ref, lam_ref,
             y_ref, h_ref, prev, hprev):
        ti = pl.program_id(2)

        @pl.when(ti == 0)
        def _():
            prev[...] = c0_ref[...]
            hprev[...] = h0_ref[...]

        uu = u_ref[...]
        ext = _bf16r(jnp.concatenate([prev[...], uu], axis=0))
        prev[...] = uu[tc - SUBLANE:tc]
        cwr = _bf16r(cw_ref[...])
        xc = cwr[0:1, :] * ext[5:5 + tc]
        for j in range(1, CONV_WIDTH):
            xc = xc + cwr[j:j + 1, :] * ext[5 + j:5 + j + tc]
        xc = xc + cb_ref[...]
        r = _sigmoid(_dot(xc, wa_ref[...]) + ba_ref[...])
        i = _sigmoid(_dot(xc, wx_ref[...]) + bx_ref[...])
        nl = -lam_ref[...]
        softplus = jnp.maximum(nl, 0.0) + jnp.log1p(jnp.exp(-jnp.abs(nl)))
        log_a = -LRU_C * r * softplus
        a = jnp.exp(log_a)
        bt = jnp.sqrt(-jnp.tanh(log_a) * (a * a + 1.0)) * (i * xc)
        rows = lax.broadcasted_iota(I32, (tc, 1), 0)
        step = 1
        while step < tc:
            keep = rows >= step
            bt = jnp.where(keep, a * pltpu.roll(bt, step, 0) + bt, bt)
            a = jnp.where(keep, a * pltpu.roll(a, step, 0), a)
            step *= 2
        h = a * hprev[...] + bt
        hprev[...] = h[tc - 1:tc]
        h_ref[...] = h
        y_ref[...] = h * _gelu(gb_ref[...])

    tile = pl.BlockSpec((tc, cw), lambda b, c, t: (b * nt + t, c))
    vec = lambda: pl.BlockSpec((None, 1, cw), lambda b, c, t: (e, 0, c))
    mat = lambda: pl.BlockSpec((None, None, cw, cw), lambda b, c, t: (e, c, 0, 0))
    bvec = lambda: pl.BlockSpec((None, None, 1, cw), lambda b, c, t: (e, c, 0, 0))
    return pl.pallas_call(
        kern, grid=(B, nblk, nt),
        in_specs=[tile, tile,
                  pl.BlockSpec((None, SUBLANE, cw), lambda b, c, t: (b, 0, c)),
                  pl.BlockSpec((None, 1, cw), lambda b, c, t: (b, 0, c)),
                  pl.BlockSpec((None, CONV_WIDTH, cw), lambda b, c, t: (e, 0, c)),
                  vec(), mat(), bvec(), mat(), bvec(), vec()],
        out_specs=[tile, tile],
        out_shape=[jax.ShapeDtypeStruct((B * T, W), F32)] * 2,
        scratch_shapes=[pltpu.VMEM((SUBLANE, cw), F32), pltpu.VMEM((1, cw), F32)],
        compiler_params=_cp(("parallel", "parallel", "arbitrary")), name="rglru",
    )(u, gb, conv0, h0, conv_w, conv_b[:, None, :], lru_wa, lru_ba[:, :, None, :], lru_wx,
      lru_bx[:, :, None, :], lru_lambda[:, None, :])


def pool_mixer(x, buf16, pool_w, pool_scale, o, B, T, t0):
    D = x.shape[1]
    ng = len(POOL_WINDOWS)
    gw = D // ng
    tt = min(T, 256)
    nt = T // tt
    halo = 2 * SUBLANE
    assert T % tt == 0 and tt >= halo and POOL_WINDOWS == (2, 4, 8, 16)

    def kern(x_ref, buf_ref, w_ref, sc_ref, o_ref, prev):
        g, ti = pl.program_id(0), pl.program_id(2)

        @pl.when(ti == 0)
        def _():
            prev[...] = buf_ref[...]

        xx = x_ref[...]
        ext = jnp.concatenate([prev[...], xx], axis=0)
        prev[...] = xx[tt - halo:tt]
        s2 = ext + pltpu.roll(ext, 1, 0)
        s4 = s2 + pltpu.roll(s2, 2, 0)
        s8 = s4 + pltpu.roll(s4, 4, 0)
        s16 = s8 + pltpu.roll(s8, 8, 0)
        sw = jnp.where(g == 0, s2, jnp.where(g == 1, s4, jnp.where(g == 2, s8, s16)))[halo:]
        wlen = jnp.left_shift(2, g)
        pos = t0 + ti * tt + lax.broadcasted_iota(I32, (tt, 1), 0)
        cnt = jnp.minimum(wlen, pos + 1).astype(F32)
        pooled = sw / cnt - xx
        o_ref[...] = _dot(pooled, w_ref[...]) * sc_ref[...]

    return pl.pallas_call(
        kern, grid=(ng, B, nt),
        in_specs=[pl.BlockSpec((tt, gw), lambda g, b, t: (b * nt + t, g)),
                  pl.BlockSpec((None, halo, gw), lambda g, b, t: (b, 0, g)),
                  pl.BlockSpec((None, None, gw, gw), lambda g, b, t: (o, g, 0, 0)),
                  pl.BlockSpec((None, 1, gw), lambda g, b, t: (o, 0, g))],
        out_specs=pl.BlockSpec((tt, gw), lambda g, b, t: (b * nt + t, g)),
        out_shape=jax.ShapeDtypeStruct((B * T, D), F32),
        scratch_shapes=[pltpu.VMEM((halo, gw), F32)],
        compiler_params=_cp(("parallel", "parallel", "arbitrary")), name="pool_mixer",
    )(x, buf16, pool_w, pool_scale[:, None, :])


def mem_attn(q, mkv, B, T, kv_off=0):
    MW = q.shape[1]
    M = mkv.shape[1]
    dh = MW // MEM_HEADS
    tq = min(T, 256)
    nq = T // tq
    scale = dh ** -0.5

    def kern(q_ref, kv_ref, o_ref):
        for h in range(MEM_HEADS):
            sl = slice(h * dh, (h + 1) * dh)
            s = _dot(q_ref[:, sl], kv_ref[:, 0, sl], nt=True) * scale
            m = jnp.max(s, axis=-1, keepdims=True)
            p = jnp.exp(s - m)
            p = p / jnp.sum(p, axis=-1, keepdims=True)
            o_ref[:, sl] = _dot(p, kv_ref[:, 1, sl])

    return pl.pallas_call(
        kern, grid=(B, nq),
        in_specs=[pl.BlockSpec((tq, MW), lambda b, i: (b * nq + i, 0)),
                  pl.BlockSpec((None, M, 2, MW), lambda b, i: (kv_off + b, 0, 0, 0))],
        out_specs=pl.BlockSpec((tq, MW), lambda b, i: (b * nq + i, 0)),
        out_shape=jax.ShapeDtypeStruct((B * T, MW), F32),
        compiler_params=_cp(("parallel", "parallel")), name="mem_attn",
    )(q, mkv)


ROUTER_ROWS = 40


def moe_router(x, w_t, bias, tm=512):
    N, D = x.shape
    tm = min(tm, N)
    assert N % tm == 0 and tm % LANE == 0

    def kern(x_ref, w_ref, b_ref, e_ref, g_ref):
        lt = _dot(w_ref[...], x_ref[...], nt=True) + b_ref[...]
        c = [lt[i:i + 1] for i in range(MOE_GROUPS)]
        m = functools.reduce(jnp.maximum, c)
        ex = [jnp.exp(ci - m) for ci in c]
        tot = functools.reduce(jnp.add, ex)
        pc = [ei / tot for ei in ex]
        pg = functools.reduce(jnp.maximum, pc)
        grp = jnp.full(pg.shape, MOE_GROUPS - 1, I32)
        for i in range(MOE_GROUPS - 2, -1, -1):
            grp = jnp.where(pc[i] == pg, i, grp)
        lf = []
        for j in range(MOE_EPG):
            v = lt[MOE_GROUPS + j:MOE_GROUPS + j + 1]
            for gi in range(1, MOE_GROUPS):
                r0 = MOE_GROUPS + gi * MOE_EPG + j
                v = jnp.where(grp == gi, lt[r0:r0 + 1], v)
            lf.append(v)
        m = functools.reduce(jnp.maximum, lf)
        ex = [jnp.exp(v - m) for v in lf]
        tot = functools.reduce(jnp.add, ex)
        pf = [ei / tot for ei in ex]

        def first_max(vals):
            best = functools.reduce(jnp.maximum, vals)
            arg = jnp.full(best.shape, len(vals) - 1, I32)
            for j in range(len(vals) - 2, -1, -1):
                arg = jnp.where(vals[j] == best, j, arg)
            return best, arg

        p1, j1 = first_max(pf)
        p2, j2 = first_max([jnp.where(j1 == j, -1.0, pf[j]) for j in range(MOE_EPG)])
        den = p1 + p2
        e_ref[0:1, :] = grp * MOE_EPG + j1
        e_ref[1:2, :] = grp * MOE_EPG + j2
        g_ref[0:1, :] = pg * p1 / den
        g_ref[1:2, :] = pg * p2 / den

    out = pl.BlockSpec((MOE_TOP_K, tm), lambda i: (0, i))
    return pl.pallas_call(
        kern, grid=(N // tm,),
        in_specs=[pl.BlockSpec((tm, D), lambda i: (i, 0)),
                  pl.BlockSpec((ROUTER_ROWS, D), lambda i: (0, 0)),
                  pl.BlockSpec((ROUTER_ROWS, 1), lambda i: (0, 0))],
        out_specs=[out, out],
        out_shape=[jax.ShapeDtypeStruct((MOE_TOP_K, N), I32), jax.ShapeDtypeStruct((MOE_TOP_K, N), F32)],
        compiler_params=_cp(("parallel",)), name="moe_router",
    )(x, w_t, bias)


def moe_experts(x3, tok, aid, nvalid, bexp, nused, w_gate, w_up, w_down, layer, rows, n_assign, tf=256):
    _, nc, _ = x3.shape
    D = nc * LANE
    n_blocks = tok.shape[0] // rows
    FF = w_gate.shape[3]
    nf = FF // tf

    def kern(tok_ref, aid_ref, nv_ref, bexp_ref, nused_ref, x3_ref, wg_ref, wu_ref, wd_ref, y_ref,
             xbuf, xb, acc, ybuf, sem_in, sem_out):
        blk, f = pl.program_id(0), pl.program_id(1)
        nused = nused_ref[0]

        def gather_start(b, slot):
            def body(r, carry):
                pltpu.make_async_copy(x3_ref.at[tok_ref[b * rows + r]], xbuf.at[slot, r], sem_in.at[slot]).start()
                return carry
            lax.fori_loop(0, rows, body, 0)

        def gather_wait(slot):
            pltpu.make_async_copy(x3_ref.at[pl.ds(0, rows)], xbuf.at[slot], sem_in.at[slot]).wait()

        @pl.when((blk < nused) & (f == 0))
        def _():
            slot = blk % 2

            @pl.when(blk == 0)
            def _():
                gather_start(0, 0)

            gather_wait(slot)

            @pl.when(blk + 1 < nused)
            def _():
                gather_start(blk + 1, 1 - slot)

            for c in range(nc):
                xb[:, c * LANE:(c + 1) * LANE] = xbuf[slot, :, c, :].astype(BF16)
            acc[...] = jnp.zeros_like(acc)

        @pl.when(blk < nused)
        def _():
            x = xb[...]
            hg = _mxu(x, wg_ref[...].astype(BF16), False)
            hu = _mxu(x, wu_ref[...].astype(BF16), False)
            hid = hg * _sigmoid(hg) * hu
            acc[...] += _dot(hid, wd_ref[...])

        @pl.when((blk < nused) & (f == nf - 1))
        def _():
            _store_token_major(ybuf, acc[...])
            nv = nv_ref[blk]

            def body(r, carry):
                pltpu.make_async_copy(ybuf.at[r], y_ref.at[aid_ref[blk * rows + r]], sem_out.at[0]).start()
                return carry
            lax.fori_loop(0, nv, body, 0)

            @pl.when(nv > 0)
            def _():
                pltpu.make_async_copy(ybuf.at[pl.ds(0, nv)], y_ref.at[pl.ds(0, nv)], sem_out.at[0]).wait()

    def live(b, nu):
        return jnp.maximum(jnp.minimum(b, nu[0] - 1), 0)

    def feff(b, f, nu):
        return jnp.where(b < nu[0], f, nf - 1)

    gs = pltpu.PrefetchScalarGridSpec(
        num_scalar_prefetch=5, grid=(n_blocks, nf),
        in_specs=[pl.BlockSpec(memory_space=pl.ANY),
                  pl.BlockSpec((None, None, D, tf), lambda b, f, t, a, nv, be, nu: (layer, be[live(b, nu)], 0, feff(b, f, nu))),
                  pl.BlockSpec((None, None, D, tf), lambda b, f, t, a, nv, be, nu: (layer, be[live(b, nu)], 0, feff(b, f, nu))),
                  pl.BlockSpec((None, None, tf, D), lambda b, f, t, a, nv, be, nu: (layer, be[live(b, nu)], feff(b, f, nu), 0))],
        out_specs=pl.BlockSpec(memory_space=pl.ANY),
        scratch_shapes=[pltpu.VMEM((2, rows, nc, LANE), F32), pltpu.VMEM((rows, D), BF16),
                        pltpu.VMEM((rows, D), F32), pltpu.VMEM((rows, nc, LANE), F32),
                        pltpu.SemaphoreType.DMA((2,)), pltpu.SemaphoreType.DMA((1,))],
    )
    return pl.pallas_call(
        kern, grid_spec=gs, out_shape=jax.ShapeDtypeStruct((n_assign, nc, LANE), F32),
        compiler_params=_cp(("arbitrary", "arbitrary")), name="moe_experts",
    )(tok, aid, nvalid, bexp, nused, x3, w_gate, w_up, w_down)


def moe_combine_ln(x, ytok, gate, ln_g, ln_b, idx, alpha, n_tok, tm=128):
    D = x.shape[1]
    nc = D // LANE
    tm = min(tm, n_tok)

    def kern(x_ref, y_ref, gt_ref, g_ref, b_ref, o_ref):
        gt = _bf16r(gt_ref[...])
        mix = jnp.concatenate(
            [sum(gt[:, k:k + 1] * _bf16r(y_ref[:, k, c, :]) for k in range(MOE_TOP_K)) for c in range(nc)], axis=1)
        o_ref[...] = _layer_norm(alpha * x_ref[...] + mix, g_ref[...], b_ref[...])

    row = pl.BlockSpec((tm, D), lambda i: (i, 0))
    par = pl.BlockSpec((None, 1, D), lambda i: (idx, 0, 0))
    return pl.pallas_call(
        kern, grid=(n_tok // tm,),
        in_specs=[row, pl.BlockSpec((tm, MOE_TOP_K, nc, LANE), lambda i: (i, 0, 0, 0)),
                  pl.BlockSpec((tm, MOE_TOP_K), lambda i: (i, 0)), par, par],
        out_specs=row, out_shape=jax.ShapeDtypeStruct((n_tok, D), F32),
        compiler_params=_cp(("parallel",)), name="moe_combine_ln",
    )(x, ytok, gate, ln_g, ln_b)


def hier_moe_ln(x, x3, n_tok, w_rt, b_rt, w_gate, w_up, w_down, layer, ln_g, ln_b, alpha, rows):
    N, D = x.shape
    xr = x if N % LANE == 0 else jnp.pad(x, ((0, LANE - N % LANE), (0, 0)))
    eidx, gate = moe_router(xr, w_rt, b_rt)
    eidx, gate = eidx[:, :n_tok], gate[:, :n_tok]
    A = n_tok * MOE_TOP_K
    e = eidx.T.reshape(A)
    onehot = (e[:, None] == jnp.arange(MOE_EXPERTS, dtype=I32)[None, :]).astype(I32)
    csum = jnp.cumsum(onehot, axis=0)
    counts = csum[-1]
    pos = jnp.sum(csum * onehot, axis=1) - 1
    padded = (counts + rows - 1) // rows * rows
    pad_end = jnp.cumsum(padded)
    pad_start = pad_end - padded
    dest = (jnp.sum(onehot * pad_start[None, :], axis=1) + pos).astype(I32)
    n_blocks = min(A // rows + MOE_EXPERTS, A)
    bexp = jnp.minimum(jnp.searchsorted(pad_end, jnp.arange(n_blocks, dtype=I32) * rows, side="right"),
                       MOE_EXPERTS - 1).astype(I32)
    nused = (pad_end[-1:] // rows).astype(I32)
    nvalid = jnp.clip((pad_start + counts)[bexp] - jnp.arange(n_blocks, dtype=I32) * rows, 0, rows).astype(I32)
    aid = jnp.zeros((n_blocks * rows,), I32).at[dest].set(jnp.arange(A, dtype=I32))
    ytok = moe_experts(x3, aid // MOE_TOP_K, aid, nvalid, bexp, nused, w_gate, w_up, w_down, layer, rows, A)
    ytok = ytok.reshape(n_tok, MOE_TOP_K, D // LANE, LANE)
    return moe_combine_ln(x, ytok, gate.T, ln_g, ln_b, layer * 3 + 2, alpha, n_tok)


def _rope_tables(pos):
    half = HEAD_DIM // 2
    inv_freq = ROPE_THETA ** (-jnp.arange(half, dtype=F32) / half)
    ang = pos.astype(F32)[:, None] * inv_freq[None, :]
    cos, sin = jnp.cos(ang), jnp.sin(ang)
    return jnp.concatenate([cos, cos], axis=1), jnp.concatenate([-sin, sin], axis=1)


def _even_projections(x, pos_rows, w_in, w_rest, w_gates, e, tm):
    G, d = NSA_G, HEAD_DIM
    q_cols = G * NSA_R * d
    kv_cols = N_KV_STREAMS * G * d
    cos, sin = _rope_tables(pos_rows)
    tn = G * d
    mm = functools.partial(matmul, x, tm=tm)
    q = mm(w_in, layer=e, col_off=0, n_out=q_cols, tn=tn, rope=(cos, sin, tuple(range(q_cols // tn))))
    kv = mm(w_in, layer=e, col_off=q_cols, n_out=kv_cols, tn=tn, rope=(cos, sin, (0, 2)))
    win = mm(w_in, layer=e, col_off=q_cols + kv_cols, n_out=2 * G * d, tn=tn, rope=(cos, sin, (0,)))
    gates = mm(w_gates, tn=G * LANE)
    rw = w_rest.shape[1] // 2
    u = mm(w_rest, n_out=rw)
    gb = mm(w_rest, col_off=rw, n_out=rw)
    return q, kv, win, gates, u, gb


def _pad_rows(a, n):
    return jnp.pad(a, ((0, n - a.shape[0]),) + ((0, 0),) * (a.ndim - 1))


def kernel(x_prompt, x_sample, mem_prompt, cache_nsa_kv, cache_nsa_win, state_rglru_h, state_rglru_conv,
           state_pool, cache_mem_kv, page_table, w_in, w_out, cmp_w1, cmp_w2, cmp_pe, conv_w, conv_b,
           lru_wa, lru_ba, lru_wx, lru_bx, lru_lambda, pool_w, pool_scale, mem_wq, mem_wk, mem_wv, mem_wo,
           ln_g, ln_b, moe_w_coarse, moe_b_coarse, moe_w_fine, moe_b_fine, moe_w_gate, moe_w_up, moe_w_down):
    B, T, D = x_prompt.shape
    DB = x_sample.shape[0]
    depth = ln_g.shape[0]
    n_pages = page_table.shape[1]
    past_len = n_pages * PAGE_SIZE
    n_phys = cache_nsa_kv.shape[1]
    G, R, d = NSA_G, NSA_R, HEAD_DIM
    alpha = (2 * depth) ** 0.25
    q_cols, kv_cols, win_cols, gate_cols = G * R * d, N_KV_STREAMS * G * d, 2 * G * d, 3 * G * R
    rest_off = q_cols + kv_cols + win_cols + gate_cols
    RW = (w_in.shape[2] - rest_off) // 2
    MW = mem_wq.shape[2]
    M = mem_prompt.shape[1]
    SR = SAMPLE_ROWS
    TS = SUBLANE

    xp = x_prompt.reshape(B * T, D)
    xs = _pad_rows(x_sample.reshape(DB, D), SR)
    lng = ln_g.reshape(depth * 3, 1, D)
    lnb = ln_b.reshape(depth * 3, 1, D)
    memf = mem_prompt.reshape(B * M, D)

    kv_p, kv_s, win_p, win_s, h_p, h_s, cv_p, cv_s, pl_p, pl_s, mem_p = ([] for _ in range(11))
    for layer in range(depth):
        if layer % 2 == 0:
            e = layer // 2
            wg = w_in[e, :, q_cols + kv_cols + win_cols:rest_off].reshape(D, G, 3 * R)
            wg = jnp.pad(wg, ((0, 0), (0, 0), (0, LANE - 3 * R))).reshape(D, G * LANE)
            w_rest = w_in[e, :, rest_off:]

            q, kv, win, gates, u, gb = _even_projections(
                xp, jnp.arange(T), w_in, w_rest, wg, e, 1024)
            new_kv = kv.reshape(B, T, N_KV_STREAMS, G, d)
            pages_p = new_kv.reshape(B * T // PAGE_SIZE, PAGE_SIZE, N_KV_STREAMS, G, d)
            ab = compress_ab(pages_p, jnp.arange(B * T // PAGE_SIZE, dtype=I32), cmp_w1, cmp_pe, e, B,
                             T // PAGE_SIZE)
            kvcmp = compress_fin(ab, cmp_w2, e)
            o_nsa = nsa_prompt(q, kv, win, kvcmp, gates, B, T)
            y_rnn, h_all = rglru(u, gb, jnp.zeros((B, SUBLANE, RW), F32), jnp.zeros((B, 1, RW), F32),
                                 conv_w, conv_b, lru_wa, lru_ba, lru_wx, lru_bx, lru_lambda, e, B, T)
            mix_p = matmul(o_nsa, w_out, layer=e, x2=y_rnn)
            keep = min(WINDOW, T)
            kv_p.append(new_kv)
            win_p.append(win.reshape(B, T, 2, G, d)[:, T - keep:])
            h_p.append(h_all.reshape(B, T, RW)[:, T - 1])
            cv_p.append(u.reshape(B, T, RW)[:, T - (CONV_WIDTH - 1):])

            q, kv, win, gates, u, gb = _even_projections(
                xs, jnp.full((SR,), past_len), w_in, w_rest, wg, e, SR)
            new_kv_s = kv[:DB].reshape(DB, N_KV_STREAMS, G, d)
            new_win_s = win[:DB].reshape(DB, 2, G, d)
            pages_s = cache_nsa_kv.reshape(cache_nsa_kv.shape[0] * n_phys, PAGE_SIZE, N_KV_STREAMS, G, d)
            pid = (page_table.reshape(DB * n_pages) + e * n_phys).astype(I32)
            ab = compress_ab(pages_s, pid, cmp_w1, cmp_pe, e, DB, n_pages)
            kvcmp = compress_fin(ab, cmp_w2, e)
            q8 = jnp.pad(q[:DB].reshape(DB, G, R, d), ((0, 0), (0, 0), (0, SUBLANE - R), (0, 0)))
            o_cmp, imp = nsa_sample_cmp(q8, kvcmp, past_len)
            idx_t = nsa_sample_topk(_pad_rows(imp.reshape(DB * SUBLANE, -1), LANE), past_len)
            n_top = idx_t.shape[0]
            idx = idx_t[:, :DB * SUBLANE].T.reshape(DB, SUBLANE, n_top)[:, :G].reshape(DB * G * n_top)
            o8 = nsa_sample_attend(q8, o_cmp, gates[:DB].reshape(DB, G, 1, LANE), idx, pid, pages_s,
                                   new_kv_s, cache_nsa_win, new_win_s, e, past_len)
            o_nsa = _pad_rows(o8[:, :, :R].reshape(DB, G * R * d), SR)
            conv0 = state_rglru_conv[e]
            u_t = jnp.pad(u[:DB, None, :], ((0, 0), (0, TS - 1), (0, 0))).reshape(DB * TS, RW)
            gb_t = jnp.pad(gb[:DB, None, :], ((0, 0), (0, TS - 1), (0, 0))).reshape(DB * TS, RW)
            c0 = jnp.pad(conv0, ((0, 0), (SUBLANE - (CONV_WIDTH - 1), 0), (0, 0)))
            y_t, h_t = rglru(u_t, gb_t, c0, state_rglru_h[e][:, None, :], conv_w, conv_b, lru_wa, lru_ba,
                             lru_wx, lru_bx, lru_lambda, e, DB, TS)
            y_rnn = _pad_rows(y_t.reshape(DB, TS, RW)[:, 0], SR)
            mix_s = matmul(o_nsa, w_out, layer=e, x2=y_rnn, tm=SR)
            keep = min(WINDOW, past_len + 1)
            win_all = jnp.concatenate([cache_nsa_win[e], new_win_s[:, None]], axis=1)
            kv_s.append(new_kv_s[:, None])
            win_s.append(win_all[:, win_all.shape[1] - keep:])
            h_s.append(h_t.reshape(DB, TS, RW)[:, 0])
            cv_s.append(jnp.concatenate([conv0, u[:DB, None, :]], axis=1)[:, 1:])
        else:
            o = layer // 2
            mix_p = pool_mixer(xp, jnp.zeros((B, 2 * SUBLANE, D), F32), pool_w, pool_scale, o, B, T, 0)
            pl_p.append(xp.reshape(B, T, D)[:, T - POOL_BUF:])
            tsp = 2 * SUBLANE
            x_t = jnp.pad(xs[:DB, None, :], ((0, 0), (0, tsp - 1), (0, 0))).reshape(DB * tsp, D)
            buf = jnp.pad(state_pool[o], ((0, 0), (tsp - POOL_BUF, 0), (0, 0)))
            mix_t = pool_mixer(x_t, buf, pool_w, pool_scale, o, DB, tsp, past_len)
            mix_s = _pad_rows(mix_t.reshape(DB, tsp, D)[:, 0], SR)
            pl_s.append(jnp.concatenate([state_pool[o], xs[:DB, None, :]], axis=1)[:, 1:])

        xp = add_ln(xp, mix_p, lng, lnb, layer * 3, alpha)
        xs = add_ln(xs, mix_s, lng, lnb, layer * 3, alpha)

        w_kv = jnp.concatenate([mem_wk[layer], mem_wv[layer]], axis=1)
        mkv = matmul(memf, w_kv).reshape(B, M, 2, MW)
        mem_p.append(mkv)
        qm = matmul(xp, mem_wq, layer=layer)
        att = mem_attn(qm, mkv, B, T)
        xp, xp3 = add_ln(xp, matmul(att, mem_wo, layer=layer), lng, lnb, layer * 3 + 1, alpha, token_major=True)
        qm = matmul(xs, mem_wq, layer=layer, tm=SR)
        qm_t = jnp.pad(qm[:DB, None, :], ((0, 0), (0, TS - 1), (0, 0))).reshape(DB * TS, MW)
        att_t = mem_attn(qm_t, cache_mem_kv.reshape((-1,) + cache_mem_kv.shape[2:]), DB, TS,
                         kv_off=layer * DB)
        att = _pad_rows(att_t.reshape(DB, TS, MW)[:, 0], SR)
        xs, xs3 = add_ln(xs, matmul(att, mem_wo, layer=layer, tm=SR), lng, lnb, layer * 3 + 1, alpha,
                         token_major=True)

        w_rt = _pad_rows(jnp.concatenate([moe_w_coarse[layer], moe_w_fine[layer]], axis=1).T, ROUTER_ROWS)
        b_rt = _pad_rows(jnp.concatenate([moe_b_coarse[layer], moe_b_fine[layer]])[:, None], ROUTER_ROWS)
        xp = hier_moe_ln(xp, xp3, B * T, w_rt, b_rt, moe_w_gate, moe_w_up, moe_w_down, layer, lng, lnb, alpha,
                         rows=256)
        xs = _pad_rows(hier_moe_ln(xs, xs3, DB, w_rt, b_rt, moe_w_gate, moe_w_up, moe_w_down, layer, lng, lnb,
                                   alpha, rows=SUBLANE), SR)

    return (xp.reshape(B, T, D), xs[:DB].reshape(DB, 1, D), jnp.stack(kv_p), jnp.stack(kv_s),
            jnp.stack(win_p), jnp.stack(win_s), jnp.stack(h_p), jnp.stack(h_s), jnp.stack(cv_p),
            jnp.stack(cv_s), jnp.stack(pl_p), jnp.stack(pl_s), jnp.stack(mem_p))
```

```python
import functools
import math

import jax
import jax.numpy as jnp
from jax import lax
from jax.experimental import pallas as pl
from jax.experimental.pallas import tpu as pltpu

F32 = jnp.float32
BF16 = jnp.bfloat16
I32 = jnp.int32

HEAD_DIM = 128
NSA_G = 4
NSA_R = 4
N_KV_STREAMS = 4
CMP_BLOCK = 32
CMP_STRIDE = 16
SEL_BLOCK = 64
SEL_TOP_N = 16
WINDOW = 512
FORCE_SCORE = 1e9
NEG = -1e30
ROPE_THETA = 10000.0
CONV_WIDTH = 4
LRU_C = 8.0
RNN_BLOCK_W = 128
POOL_WINDOWS = (2, 4, 8, 16)
POOL_BUF = 15
MEM_HEADS = 4
MOE_GROUPS = 4
MOE_EPG = 8
MOE_EXPERTS = MOE_GROUPS * MOE_EPG
MOE_TOP_K = 2
LN_EPS = 1e-5
PAGE_SIZE = 128

LANE = 128
SUBLANE = 8
VMEM_LIMIT = 56 * 1024 * 1024
SAMPLE_ROWS = 16


def _cp(sem, vmem=VMEM_LIMIT):
    return pltpu.CompilerParams(dimension_semantics=sem, vmem_limit_bytes=vmem)


def _split3(x):
    h = x.astype(BF16)
    r = x - h.astype(F32)
    m = r.astype(BF16)
    l = (r - m.astype(F32)).astype(BF16)
    return h, m, l


def _mxu(a, b, nt):
    if nt:
        return lax.dot_general(a, b, (((1,), (1,)), ((), ())), preferred_element_type=F32)
    return jnp.dot(a, b, preferred_element_type=F32)


def _dot(a, b, nt=False):
    return _mxu(a.astype(BF16), b.astype(BF16), nt)


def _bf16r(x):
    return x.astype(BF16).astype(F32)


def _sigmoid(x):
    return 1.0 / (1.0 + jnp.exp(-x))


def _gelu(x):
    return 0.5 * x * (1.0 + jnp.tanh(0.7978845608028654 * (x + 0.044715 * (x * x * x))))


def _msoftmax(s, mask):
    s = jnp.where(mask, s, NEG)
    m = jnp.max(s, axis=-1, keepdims=True)
    p = jnp.where(mask, jnp.exp(s - m), 0.0)
    return p / jnp.maximum(jnp.sum(p, axis=-1, keepdims=True), 1e-30)


def _msoftmax_nonempty(s, mask):
    s = jnp.where(mask, s, NEG)
    p = jnp.exp(s - jnp.max(s, axis=-1, keepdims=True))
    return p * (1.0 / jnp.sum(p, axis=-1, keepdims=True))


def _layer_norm(v, g, b):
    mu = jnp.mean(v, axis=-1, keepdims=True)
    c = v - mu
    var = jnp.mean(c * c, axis=-1, keepdims=True)
    return c * lax.rsqrt(var + LN_EPS) * g + b


def matmul(x, w, *, layer=0, col_off=0, n_out=None, x2=None, tm=1024, tn=512, rope=None, out_dtype=F32):
    if w.ndim == 2:
        w = w[None]
    M, K1 = x.shape
    K2 = 0 if x2 is None else x2.shape[1]
    assert w.shape[1] == K1 + K2
    n_out = w.shape[2] - col_off if n_out is None else n_out
    xs_in = [x] if x2 is None else [x, x2]
    if any(a.dtype != BF16 for a in xs_in):
        tm = min(tm, 512)
    tm, tn = min(tm, M), min(tn, n_out)
    assert M % tm == 0 and n_out % tn == 0 and col_off % tn == 0
    joff = col_off // tn

    in_specs = [pl.BlockSpec((tm, a.shape[1]), lambda i, j: (i, 0)) for a in xs_in]
    in_specs.append(pl.BlockSpec((None, K1 + K2, tn), lambda i, j: (layer, 0, j + joff)))
    args = xs_in + [w]
    rope_blocks = ()
    if rope is not None:
        cos, sin, rope_blocks = rope
        period = cos.shape[0] // tm
        assert cos.shape[0] % tm == 0
        for t in (cos, sin):
            in_specs.append(pl.BlockSpec((tm, LANE), lambda i, j: (i % period, 0)))
            args.append(t)
    cast = [a.dtype != BF16 for a in xs_in]
    scratch = [pltpu.VMEM((tm, a.shape[1]), BF16) for a, c in zip(xs_in, cast) if c]

    def kern(*refs):
        refs = list(refs)
        x_refs = [refs.pop(0) for _ in xs_in]
        w_ref = refs.pop(0)
        cos_ref, sin_ref = (refs.pop(0), refs.pop(0)) if rope is not None else (None, None)
        o_ref = refs.pop(0)
        j = pl.program_id(1)
        xb_refs = [refs.pop(0) if c else xr for xr, c in zip(x_refs, cast)]

        if any(cast):
            @pl.when(j == 0)
            def _():
                for xr, xb, c in zip(x_refs, xb_refs, cast):
                    if c:
                        xb[...] = xr[...].astype(BF16)

        acc = _mxu(xb_refs[0][...], w_ref[0:K1, :].astype(BF16), False)
        if x2 is not None:
            acc = acc + _mxu(xb_refs[1][...], w_ref[K1:K1 + K2, :].astype(BF16), False)
        if rope is None:
            o_ref[...] = acc.astype(out_dtype)
            return
        rot = functools.reduce(jnp.logical_or, [j == c for c in rope_blocks])

        @pl.when(rot)
        def _():
            c, s = cos_ref[...], sin_ref[...]
            for h in range(tn // LANE):
                seg = acc[:, h * LANE:(h + 1) * LANE]
                o_ref[:, h * LANE:(h + 1) * LANE] = (seg * c + pltpu.roll(seg, LANE // 2, 1) * s).astype(out_dtype)

        @pl.when(jnp.logical_not(rot))
        def _():
            o_ref[...] = acc.astype(out_dtype)

    return pl.pallas_call(
        kern,
        grid=(M // tm, n_out // tn),
        in_specs=in_specs,
        out_specs=pl.BlockSpec((tm, tn), lambda i, j: (i, j)),
        out_shape=jax.ShapeDtypeStruct((M, n_out), out_dtype),
        scratch_shapes=scratch,
        compiler_params=_cp(("parallel", "arbitrary")),
        name="matmul",
    )(*args)


def _token_major(v):
    return v.reshape(v.shape[0], v.shape[1] // LANE, LANE)


def add_ln(x, f, ln_g, ln_b, idx, alpha, tm=256, token_major=False):
    M, D = x.shape
    tm = min(tm, M)

    def kern(x_ref, f_ref, g_ref, b_ref, o_ref, ob_ref, *o3_ref):
        y = _layer_norm(alpha * x_ref[...] + f_ref[...], g_ref[...], b_ref[...])
        o_ref[...] = y
        ob_ref[...] = y.astype(BF16)
        if token_major:
            o3_ref[0][...] = _token_major(y)

    row = pl.BlockSpec((tm, D), lambda i: (i, 0))
    par = pl.BlockSpec((None, 1, D), lambda i: (idx, 0, 0))
    out_specs = [row, row]
    out_shape = [jax.ShapeDtypeStruct((M, D), F32), jax.ShapeDtypeStruct((M, D), BF16)]
    if token_major:
        out_specs.append(pl.BlockSpec((tm, D // LANE, LANE), lambda i: (i, 0, 0)))
        out_shape.append(jax.ShapeDtypeStruct((M, D // LANE, LANE), F32))
    return pl.pallas_call(
        kern, grid=(M // tm,), in_specs=[row, row, par, par], out_specs=out_specs,
        out_shape=out_shape, compiler_params=_cp(("parallel",)), name="add_ln",
    )(x, f, ln_g, ln_b)


CMP_PAGES = 8


def compress_ab(pages, page_ids, cmp_w1, cmp_pe, e, n_batch, pages_per_seq):
    P = CMP_PAGES
    assert pages_per_seq % P == 0
    cpp = PAGE_SIZE // CMP_STRIDE
    steps = pages_per_seq // P
    d = HEAD_DIM
    n_chunks = pages_per_seq * cpp
    rows = NSA_G * P * cpp

    def kern(pid_ref, *refs):
        page_refs, w_ref, pe_ref, o_ref = refs[:P], refs[P], refs[P + 1], refs[P + 2]
        for s in range(2):
            acc_a = jnp.zeros((rows, d), F32)
            acc_b = jnp.zeros((rows, d), F32)
            def chunk_rows(l):
                return jnp.concatenate(
                    [page_refs[i][pl.ds(l, cpp, stride=CMP_STRIDE), s, g, :]
                     for g in range(NSA_G) for i in range(P)], axis=0)

            def pair(l, off):
                lhs = jnp.concatenate([rows2[0] + pe_ref[s, off + l:off + l + 1, :],
                                       rows2[1] + pe_ref[s, off + l + 1:off + l + 2, :]], axis=1)
                return _dot(lhs, w_ref[s, off + l:off + l + 2].reshape(2 * d, d))

            for l in range(0, CMP_STRIDE, 2):
                rows2 = (chunk_rows(l), chunk_rows(l + 1))
                acc_a = acc_a + pair(l, 0)
                acc_b = acc_b + pair(l, CMP_STRIDE)
            for g in range(NSA_G):
                c0 = (s * NSA_G + g) * 2 * d
                o_ref[:, c0:c0 + d] = acc_a[g * P * cpp:(g + 1) * P * cpp]
                o_ref[:, c0 + d:c0 + 2 * d] = acc_b[g * P * cpp:(g + 1) * P * cpp]

    def page_spec(i):
        return pl.BlockSpec((None, PAGE_SIZE, 2, NSA_G, d),
                            lambda b, st, pid: (pid[(b * steps + st) * P + i], 0, 0, 0, 0))

    gs = pltpu.PrefetchScalarGridSpec(
        num_scalar_prefetch=1,
        grid=(n_batch, steps),
        in_specs=[page_spec(i) for i in range(P)]
        + [pl.BlockSpec((None, 2, CMP_BLOCK, d, d), lambda b, st, pid: (e, 0, 0, 0, 0)),
           pl.BlockSpec((None, 2, CMP_BLOCK, d), lambda b, st, pid: (e, 0, 0, 0))],
        out_specs=pl.BlockSpec((None, P * cpp, 2 * NSA_G * 2 * d), lambda b, st, pid: (b, st, 0)),
    )
    return pl.pallas_call(
        kern, grid_spec=gs,
        out_shape=jax.ShapeDtypeStruct((n_batch, n_chunks, 2 * NSA_G * 2 * d), F32),
        compiler_params=_cp(("parallel", "parallel")), name="compress_ab",
    )(page_ids, *([pages] * P), cmp_w1, cmp_pe)


def compress_fin(ab, cmp_w2, e):
    nb, nch, _ = ab.shape
    d = HEAD_DIM

    def kern(ab_ref, w2_ref, o_ref):
        for g in range(NSA_G):
            a = ab_ref[:, g * 2 * d:g * 2 * d + d]
            bn = ab_ref[:, g * 2 * d + d:(g + 1) * 2 * d]
            h = a + pltpu.roll(bn, nch - 1, 0)
            o_ref[g] = _dot(_gelu(h), w2_ref[...])

    return pl.pallas_call(
        kern, grid=(nb, 2),
        in_specs=[pl.BlockSpec((None, nch, NSA_G * 2 * d), lambda b, s: (b, 0, s)),
                  pl.BlockSpec((None, None, d, d), lambda b, s: (e, s, 0, 0))],
        out_specs=pl.BlockSpec((None, None, NSA_G, nch, d), lambda b, s: (s, b, 0, 0, 0)),
        out_shape=jax.ShapeDtypeStruct((2, nb, NSA_G, nch, d), F32),
        compiler_params=_cp(("parallel", "parallel")), name="compress_fin",
    )(ab, cmp_w2)


def _overlap_matrix(n_rows, n_cols):
    ci = jnp.arange(n_rows)[:, None] * CMP_STRIDE
    sj = jnp.arange(n_cols)[None, :] * SEL_BLOCK
    return ((ci < sj + SEL_BLOCK) & (ci + CMP_BLOCK > sj)).astype(BF16)


NSA_TQ = 128


def nsa_prompt(q, kv, win, kvcmp, gates, B, T):
    d, G, R, tq = HEAD_DIM, NSA_G, NSA_R, NSA_TQ
    nq = T // tq
    ncp = kvcmp.shape[3]
    n_cmp = T // CMP_STRIDE - 1
    n_sel = -(-T // SEL_BLOCK)
    n_top = min(SEL_TOP_N, n_sel)
    wk = min(T, WINDOW + tq)
    key_span = min(T, 4 * tq)
    assert n_sel <= LANE and T % tq == 0 and T % key_span == 0
    scale = d ** -0.5
    overlap = _overlap_matrix(ncp, LANE)
    expand = (jnp.arange(LANE)[:, None] == (jnp.arange(T) // SEL_BLOCK)[None, :]).astype(BF16)

    def kern(q_ref, ks_ref, vs_ref, kw_ref, vw_ref, kc_ref, vc_ref, g_ref, ov_ref, ex_ref, o_ref,
             ksb, vsb, kwb, vwb, kcb, vcb):
        qi = pl.program_id(2)

        @pl.when(qi == 0)
        def _():
            ksb[...] = ks_ref[...].astype(BF16)
            vsb[...] = vs_ref[...].astype(BF16)
            kwb[...] = kw_ref[...].astype(BF16)
            vwb[...] = vw_ref[...].astype(BF16)
            kcb[...] = kc_ref[...].astype(BF16)
            vcb[...] = vc_ref[...].astype(BF16)

        t0 = qi * tq
        qb = q_ref[...]
        qs = jnp.concatenate([qb[:, r * d:(r + 1) * d] for r in range(R)], axis=0).astype(BF16)
        qpos_s = t0 + (lax.broadcasted_iota(I32, (R * tq, 1), 0) & (tq - 1))

        s = _mxu(qs, kcb[...], True) * scale
        n_idx = lax.broadcasted_iota(I32, (1, ncp), 1)
        cmask = (n_idx * CMP_STRIDE + (CMP_BLOCK - 1) <= qpos_s) & (n_idx < n_cmp)
        p = _msoftmax(s, cmask)
        pb = p.astype(BF16)
        o_cmp = _mxu(pb, vcb[...], False)
        imp_r = _mxu(pb, ov_ref[...], False)
        imp = imp_r[0:tq]
        for r in range(1, R):
            imp = imp + imp_r[r * tq:(r + 1) * tq]

        blk = lax.broadcasted_iota(I32, (tq, LANE), 1)
        qpos = t0 + lax.broadcasted_iota(I32, (tq, LANE), 0)
        cur = qpos // SEL_BLOCK
        valid = blk <= cur
        forced = (blk == 0) | (blk == cur) | (blk == cur - 1)
        score = jnp.where(forced, FORCE_SCORE, jnp.where(valid, imp, NEG))
        cnt = jnp.zeros((tq, LANE), I32)
        for i in range(n_sel):
            col = score[:, i:i + 1]
            beats = (col > score) | ((col == score) & (blk > i))
            cnt = cnt + beats.astype(I32)
        sel = ((cnt < n_top) & valid).astype(BF16)

        kstart = pl.multiple_of(jnp.clip(t0 - WINDOW, 0, T - wk), tq)
        kwin = kwb[pl.ds(kstart, wk), :]
        vwin = vwb[pl.ds(kstart, wk), :]
        s_w = _mxu(qs, kwin, True) * scale
        kp = kstart + lax.broadcasted_iota(I32, (1, wk), 1)
        wmask = (kp <= qpos_s) & (kp > qpos_s - WINDOW)
        p_w = _msoftmax_nonempty(s_w, wmask)
        o_win = _mxu(p_w.astype(BF16), vwin, False)

        gt = _sigmoid(g_ref[...])

        def selected(nkeys):
            selx = _mxu(sel, ex_ref[:, 0:nkeys], False)
            kpos = lax.broadcasted_iota(I32, (tq, nkeys), 1)
            qpos_t = t0 + lax.broadcasted_iota(I32, (tq, nkeys), 0)
            smask = (selx > 0.5) & (kpos <= qpos_t)
            for r in range(R):
                s_r = _mxu(qs[r * tq:(r + 1) * tq], ksb[0:nkeys, :], True) * scale
                p_r = _msoftmax_nonempty(s_r, smask)
                o_slc = _mxu(p_r.astype(BF16), vsb[0:nkeys, :], False)
                o_r = (gt[:, 3 * r:3 * r + 1] * o_cmp[r * tq:(r + 1) * tq]
                       + gt[:, 3 * r + 1:3 * r + 2] * o_slc
                       + gt[:, 3 * r + 2:3 * r + 3] * o_win[r * tq:(r + 1) * tq])
                o_ref[:, r * d:(r + 1) * d] = o_r.astype(BF16)

        n_span = pl.cdiv(t0 + tq, key_span)
        for n in range(1, T // key_span + 1):
            pl.when(n_span == n)(functools.partial(selected, n * key_span))

    seq = lambda blkcol: pl.BlockSpec((T, d), lambda b, g, qi: (b, blkcol(g)))
    cmp_spec = lambda s: pl.BlockSpec((None, None, None, ncp, d), lambda b, g, qi: (s, b, g, 0, 0))
    return pl.pallas_call(
        kern, grid=(B, G, nq),
        in_specs=[pl.BlockSpec((tq, R * d), lambda b, g, qi: (b * nq + qi, g)),
                  seq(lambda g: 2 * G + g), seq(lambda g: 3 * G + g),
                  seq(lambda g: g), seq(lambda g: G + g),
                  cmp_spec(0), cmp_spec(1),
                  pl.BlockSpec((tq, LANE), lambda b, g, qi: (b * nq + qi, g)),
                  pl.BlockSpec((ncp, LANE), lambda b, g, qi: (0, 0)),
                  pl.BlockSpec((LANE, T), lambda b, g, qi: (0, 0))],
        out_specs=pl.BlockSpec((tq, R * d), lambda b, g, qi: (b * nq + qi, g)),
        out_shape=jax.ShapeDtypeStruct((B * T, G * R * d), BF16),
        scratch_shapes=[pltpu.VMEM((T, d), BF16)] * 4 + [pltpu.VMEM((ncp, d), BF16)] * 2,
        compiler_params=_cp(("parallel", "parallel", "arbitrary")), name="nsa_prompt",
    )(q, kv, kv, win, win, kvcmp, kvcmp, gates, overlap, expand)


def nsa_sample_cmp(q8, kvcmp, q_pos):
    B = q8.shape[0]
    d, G, R = HEAD_DIM, NSA_G, NSA_R
    nch = kvcmp.shape[3]
    n_cmp = nch - 1
    n_sel = -(-(q_pos + 1) // SEL_BLOCK)
    nsp = -(-n_sel // LANE) * LANE
    overlap = _overlap_matrix(nch, nsp)
    scale = d ** -0.5

    def kern(q_ref, k_ref, v_ref, ov_ref, o_ref, imp_ref):
        n_idx = lax.broadcasted_iota(I32, (1, nch), 1)
        cmask = (n_idx * CMP_STRIDE + (CMP_BLOCK - 1) <= q_pos) & (n_idx < n_cmp)
        rows = []
        for g in range(G):
            s = _dot(q_ref[g], k_ref[g], nt=True) * scale
            pb = _msoftmax(s, cmask).astype(BF16)
            o_ref[g] = _dot(pb, v_ref[g])
            rows.append(jnp.sum(_mxu(pb, ov_ref[...], False)[0:R], axis=0, keepdims=True))
        imp_ref[...] = jnp.concatenate(rows + [jnp.zeros((SUBLANE - G, nsp), F32)], axis=0)

    qspec = pl.BlockSpec((None, G, SUBLANE, d), lambda b: (b, 0, 0, 0))
    cspec = lambda s: pl.BlockSpec((None, None, G, nch, d), lambda b: (s, b, 0, 0, 0))
    return pl.pallas_call(
        kern, grid=(B,),
        in_specs=[qspec, cspec(0), cspec(1), pl.BlockSpec((nch, nsp), lambda b: (0, 0))],
        out_specs=[qspec, pl.BlockSpec((None, SUBLANE, nsp), lambda b: (b, 0, 0))],
        out_shape=[jax.ShapeDtypeStruct((B, G, SUBLANE, d), F32),
                   jax.ShapeDtypeStruct((B, SUBLANE, nsp), F32)],
        compiler_params=_cp(("parallel",)), name="nsa_sample_cmp",
    )(q8, kvcmp, kvcmp, overlap)


def nsa_sample_topk(imp, q_pos):
    nr, nsp = imp.shape
    assert nr == LANE
    n_sel = -(-(q_pos + 1) // SEL_BLOCK)
    n_top = min(SEL_TOP_N, n_sel)
    cur = q_pos // SEL_BLOCK

    def kern(imp_ref, o_ref, sct):
        blk = lax.broadcasted_iota(I32, (nr, nsp), 1)
        valid = blk <= cur
        forced = (blk == 0) | (blk == cur) | (blk == cur - 1)
        score = jnp.where(forced, FORCE_SCORE, jnp.where(valid, imp_ref[...], NEG))
        eye = (lax.broadcasted_iota(I32, (nsp, nsp), 0) == lax.broadcasted_iota(I32, (nsp, nsp), 1)).astype(BF16)
        h, m, l = _split3(score)
        sct[...] = (_mxu(eye, h, True) + _mxu(eye, m, True)) + _mxu(eye, l, True)
        sc = sct[...]
        jidx = lax.broadcasted_iota(I32, (nsp, nr), 0)

        def body(i, cnt):
            row = sct[pl.ds(i, 1), :]
            beats = (row > sc) | ((row == sc) & (jidx > i))
            return cnt + beats.astype(I32)

        cnt = lax.fori_loop(0, n_sel, body, jnp.zeros((nsp, nr), I32))
        for p in range(n_top):
            o_ref[p:p + 1, :] = jnp.sum(jnp.where(cnt == p, jidx, 0), axis=0, keepdims=True)

    return pl.pallas_call(
        kern, grid=(1,),
        in_specs=[pl.BlockSpec((nr, nsp), lambda i: (0, 0))],
        out_specs=pl.BlockSpec((n_top, nr), lambda i: (0, 0)),
        out_shape=jax.ShapeDtypeStruct((n_top, nr), I32),
        scratch_shapes=[pltpu.VMEM((nsp, nr), F32)],
        compiler_params=_cp(("arbitrary",)), name="nsa_sample_topk",
    )(imp)


def nsa_sample_attend(q8, o_cmp, gates, idx, page_ids, pages, new_kv, cwin, new_win, e, q_pos):
    B = q8.shape[0]
    d, G, R = HEAD_DIM, NSA_G, NSA_R
    n_top = idx.shape[0] // (B * G)
    n_pages = page_ids.shape[0] // B
    n_cache_blocks = n_pages * (PAGE_SIZE // SEL_BLOCK)
    bpp = PAGE_SIZE // SEL_BLOCK
    w0 = cwin.shape[2]
    scale = d ** -0.5
    nk = n_top * SEL_BLOCK

    def kern(idx_ref, pid_ref, q_ref, oc_ref, g_ref, pages_ref, nkv_ref, cw_ref, nw_ref, o_ref,
             kvbuf, sem):
        b = pl.program_id(0)

        def slot_copy(slot, j):
            page = pid_ref[b * n_pages + j // bpp]
            r0 = (j % bpp) * SEL_BLOCK
            return pltpu.make_async_copy(
                pages_ref.at[page, pl.ds(r0, SEL_BLOCK), pl.ds(2, 2)], kvbuf.at[slot], sem.at[0])

        for slot in range(G * n_top):
            j = idx_ref[b * G * n_top + slot]

            @pl.when(j < n_cache_blocks)
            def _():
                slot_copy(slot, j).start()

            @pl.when(j >= n_cache_blocks)
            def _():
                kvbuf[slot] = jnp.zeros((SEL_BLOCK, 2, G, d), F32)

            @pl.when(j * SEL_BLOCK == q_pos)
            def _():
                kvbuf[slot, 0] = nkv_ref[2:4]

        for slot in range(G * n_top):
            j = idx_ref[b * G * n_top + slot]

            @pl.when(j < n_cache_blocks)
            def _():
                slot_copy(slot, j).wait()

        lane_k = lax.broadcasted_iota(I32, (1, nk), 1)
        row8 = lax.broadcasted_iota(I32, (SUBLANE, 1), 0)
        for g in range(G):
            qg = q_ref[g]
            kk = jnp.concatenate([kvbuf[g * n_top + p, :, 0, g, :] for p in range(n_top)], axis=0)
            vv = jnp.concatenate([kvbuf[g * n_top + p, :, 1, g, :] for p in range(n_top)], axis=0)
            kpos = lane_k & (SEL_BLOCK - 1)
            for p in range(n_top):
                j = idx_ref[(b * G + g) * n_top + p]
                kpos = kpos + jnp.where(lane_k // SEL_BLOCK == p, j * SEL_BLOCK, 0)
            s = _dot(qg, kk, nt=True) * scale
            p_s = _msoftmax(s, kpos <= q_pos)
            o_slc = _dot(p_s, vv)
            kw = cw_ref[:, 0, g, :]
            vw = cw_ref[:, 1, g, :]
            s_w = _dot(qg, kw, nt=True) * scale
            wpos = (q_pos - w0) + lax.broadcasted_iota(I32, (1, w0), 1)
            wmask = (wpos >= 0) & (wpos <= q_pos) & (wpos > q_pos - WINDOW)
            s_w = jnp.where(wmask, s_w, NEG)
            s_n = jnp.sum(_bf16r(qg) * _bf16r(nw_ref[0, g:g + 1, :]), axis=-1, keepdims=True) * scale
            m = jnp.maximum(jnp.max(s_w, axis=-1, keepdims=True), s_n)
            p_w = jnp.where(wmask, jnp.exp(s_w - m), 0.0)
            p_n = jnp.exp(s_n - m)
            den = jnp.sum(p_w, axis=-1, keepdims=True) + p_n
            o_win = _dot(p_w / den, vw) + _bf16r(p_n / den) * _bf16r(nw_ref[1, g:g + 1, :])
            gt = _sigmoid(g_ref[g])
            gc = [jnp.zeros((SUBLANE, 1), F32)] * 3
            for r in range(R):
                for c in range(3):
                    gc[c] = jnp.where(row8 == r, gt[:, 3 * r + c:3 * r + c + 1], gc[c])
            o_ref[g] = gc[0] * oc_ref[g] + gc[1] * o_slc + gc[2] * o_win

    qspec = pl.BlockSpec((None, G, SUBLANE, d), lambda b, i, p: (b, 0, 0, 0))
    gs = pltpu.PrefetchScalarGridSpec(
        num_scalar_prefetch=2, grid=(B,),
        in_specs=[qspec, qspec,
                  pl.BlockSpec((None, G, 1, LANE), lambda b, i, p: (b, 0, 0, 0)),
                  pl.BlockSpec(memory_space=pl.ANY),
                  pl.BlockSpec((None, N_KV_STREAMS, G, d), lambda b, i, p: (b, 0, 0, 0)),
                  pl.BlockSpec((None, None, w0, 2, G, d), lambda b, i, p: (e, b, 0, 0, 0, 0)),
                  pl.BlockSpec((None, 2, G, d), lambda b, i, p: (b, 0, 0, 0))],
        out_specs=qspec,
        scratch_shapes=[pltpu.VMEM((G * n_top, SEL_BLOCK, 2, G, d), F32),
                        pltpu.SemaphoreType.DMA((1,))],
    )
    return pl.pallas_call(
        kern, grid_spec=gs, out_shape=jax.ShapeDtypeStruct((B, G, SUBLANE, d), F32),
        compiler_params=_cp(("arbitrary",)), name="nsa_sample_attend",
    )(idx, page_ids, q8, o_cmp, gates, pages, new_kv, cwin, new_win)


def rglru(u, gb, conv0, h0, conv_w, conv_b, lru_wa, lru_ba, lru_wx, lru_bx, lru_lambda, e, B, T):
    W = u.shape[1]
    cw = RNN_BLOCK_W
    nblk = W // cw
    tc = min(T, 256)
    nt = T // tc
    assert T % tc == 0 and tc % SUBLANE == 0

    def kern(u_ref, gb_ref, c0_ref, h0_ref, cw_ref, cb_ref, wa_ref, ba_ref, wx_ref, bx_ref, lam_ref,
             y_ref, h_ref, prev, hprev):
        ti = pl.program_id(2)

        @pl.when(ti == 0)
        def _():
            prev[...] = c0_ref[...]
            hprev[...] = h0_ref[...]

        uu = u_ref[...]
        ext = _bf16r(jnp.concatenate([prev[...], uu], axis=0))
        prev[...] = uu[tc - SUBLANE:tc]
        cwr = _bf16r(cw_ref[...])
        xc = cwr[0:1, :] * ext[5:5 + tc]
        for j in range(1, CONV_WIDTH):
            xc = xc + cwr[j:j + 1, :] * ext[5 + j:5 + j + tc]
        xc = xc + cb_ref[...]
        r = _sigmoid(_dot(xc, wa_ref[...]) + ba_ref[...])
        i = _sigmoid(_dot(xc, wx_ref[...]) + bx_ref[...])
        nl = -lam_ref[...]
        softplus = jnp.maximum(nl, 0.0) + jnp.log1p(jnp.exp(-jnp.abs(nl)))
        log_a = -LRU_C * r * softplus
        a = jnp.exp(log_a)
        bt = jnp.sqrt(-jnp.tanh(log_a) * (a * a + 1.0)) * (i * xc)
        rows = lax.broadcasted_iota(I32, (tc, 1), 0)
        step = 1
        while step < tc:
            keep = rows >= step
            bt = jnp.where(keep, a * pltpu.roll(bt, step, 0) + bt, bt)
            a = jnp.where(keep, a * pltpu.roll(a, step, 0), a)
            step *= 2
        h = a * hprev[...] + bt
        hprev[...] = h[tc - 1:tc]
        h_ref[...] = h
        y_ref[...] = (h * _gelu(gb_ref[...])).astype(BF16)

    tile = pl.BlockSpec((tc, cw), lambda b, c, t: (b * nt + t, c))
    vec = lambda: pl.BlockSpec((None, 1, cw), lambda b, c, t: (e, 0, c))
    mat = lambda: pl.BlockSpec((None, None, cw, cw), lambda b, c, t: (e, c, 0, 0))
    bvec = lambda: pl.BlockSpec((None, None, 1, cw), lambda b, c, t: (e, c, 0, 0))
    return pl.pallas_call(
        kern, grid=(B, nblk, nt),
        in_specs=[tile, tile,
                  pl.BlockSpec((None, SUBLANE, cw), lambda b, c, t: (b, 0, c)),
                  pl.BlockSpec((None, 1, cw), lambda b, c, t: (b, 0, c)),
                  pl.BlockSpec((None, CONV_WIDTH, cw), lambda b, c, t: (e, 0, c)),
                  vec(), mat(), bvec(), mat(), bvec(), vec()],
        out_specs=[tile, tile],
        out_shape=[jax.ShapeDtypeStruct((B * T, W), BF16), jax.ShapeDtypeStruct((B * T, W), F32)],
        scratch_shapes=[pltpu.VMEM((SUBLANE, cw), F32), pltpu.VMEM((1, cw), F32)],
        compiler_params=_cp(("parallel", "parallel", "arbitrary")), name="rglru",
    )(u, gb, conv0, h0, conv_w, conv_b[:, None, :], lru_wa, lru_ba[:, :, None, :], lru_wx,
      lru_bx[:, :, None, :], lru_lambda[:, None, :])


def pool_mixer(x, buf16, pool_w, pool_scale, o, B, T, t0):
    D = x.shape[1]
    ng = len(POOL_WINDOWS)
    gw = D // ng
    tt = min(T, 256)
    nt = T // tt
    halo = 2 * SUBLANE
    assert T % tt == 0 and tt >= halo and POOL_WINDOWS == (2, 4, 8, 16)

    def kern(x_ref, buf_ref, w_ref, sc_ref, o_ref, prev):
        g, ti = pl.program_id(0), pl.program_id(2)

        @pl.when(ti == 0)
        def _():
            prev[...] = buf_ref[...]

        xx = x_ref[...]
        ext = jnp.concatenate([prev[...], xx], axis=0)
        prev[...] = xx[tt - halo:tt]
        s2 = ext + pltpu.roll(ext, 1, 0)
        s4 = s2 + pltpu.roll(s2, 2, 0)
        s8 = s4 + pltpu.roll(s4, 4, 0)
        s16 = s8 + pltpu.roll(s8, 8, 0)
        sw = jnp.where(g == 0, s2, jnp.where(g == 1, s4, jnp.where(g == 2, s8, s16)))[halo:]
        wlen = jnp.left_shift(2, g)
        pos = t0 + ti * tt + lax.broadcasted_iota(I32, (tt, 1), 0)
        cnt = jnp.minimum(wlen, pos + 1).astype(F32)
        pooled = sw / cnt - xx
        o_ref[...] = _dot(pooled, w_ref[...]) * sc_ref[...]

    return pl.pallas_call(
        kern, grid=(ng, B, nt),
        in_specs=[pl.BlockSpec((tt, gw), lambda g, b, t: (b * nt + t, g)),
                  pl.BlockSpec((None, halo, gw), lambda g, b, t: (b, 0, g)),
                  pl.BlockSpec((None, None, gw, gw), lambda g, b, t: (o, g, 0, 0)),
                  pl.BlockSpec((None, 1, gw), lambda g, b, t: (o, 0, g))],
        out_specs=pl.BlockSpec((tt, gw), lambda g, b, t: (b * nt + t, g)),
        out_shape=jax.ShapeDtypeStruct((B * T, D), F32),
        scratch_shapes=[pltpu.VMEM((halo, gw), F32)],
        compiler_params=_cp(("parallel", "parallel", "arbitrary")), name="pool_mixer",
    )(x, buf16, pool_w, pool_scale[:, None, :])


def mem_attn(q, mkv, B, T, kv_off=0):
    MW = q.shape[1]
    M = mkv.shape[1]
    dh = MW // MEM_HEADS
    tq = min(T, 256)
    nq = T // tq
    scale = dh ** -0.5

    def kern(q_ref, kv_ref, o_ref):
        for h in range(MEM_HEADS):
            sl = slice(h * dh, (h + 1) * dh)
            s = _dot(q_ref[:, sl], kv_ref[:, 0, sl], nt=True) * scale
            m = jnp.max(s, axis=-1, keepdims=True)
            p = jnp.exp(s - m)
            p = p / jnp.sum(p, axis=-1, keepdims=True)
            o_ref[:, sl] = _dot(p, kv_ref[:, 1, sl]).astype(BF16)

    return pl.pallas_call(
        kern, grid=(B, nq),
        in_specs=[pl.BlockSpec((tq, MW), lambda b, i: (b * nq + i, 0)),
                  pl.BlockSpec((None, M, 2, MW), lambda b, i: (kv_off + b, 0, 0, 0))],
        out_specs=pl.BlockSpec((tq, MW), lambda b, i: (b * nq + i, 0)),
        out_shape=jax.ShapeDtypeStruct((B * T, MW), BF16),
        compiler_params=_cp(("parallel", "parallel")), name="mem_attn",
    )(q, mkv)


ROUTER_ROWS = 40


def moe_router(x, w_t, bias, tm=512):
    N, D = x.shape
    tm = min(tm, N)
    assert N % tm == 0 and tm % LANE == 0

    def kern(x_ref, w_ref, b_ref, e_ref, g_ref):
        lt = _dot(w_ref[...], x_ref[...], nt=True) + b_ref[...]
        c = [lt[i:i + 1] for i in range(MOE_GROUPS)]
        m = functools.reduce(jnp.maximum, c)
        ex = [jnp.exp(ci - m) for ci in c]
        tot = functools.reduce(jnp.add, ex)
        pc = [ei / tot for ei in ex]
        pg = functools.reduce(jnp.maximum, pc)
        grp = jnp.full(pg.shape, MOE_GROUPS - 1, I32)
        for i in range(MOE_GROUPS - 2, -1, -1):
            grp = jnp.where(pc[i] == pg, i, grp)
        lf = []
        for j in range(MOE_EPG):
            v = lt[MOE_GROUPS + j:MOE_GROUPS + j + 1]
            for gi in range(1, MOE_GROUPS):
                r0 = MOE_GROUPS + gi * MOE_EPG + j
                v = jnp.where(grp == gi, lt[r0:r0 + 1], v)
            lf.append(v)
        m = functools.reduce(jnp.maximum, lf)
        ex = [jnp.exp(v - m) for v in lf]
        tot = functools.reduce(jnp.add, ex)
        pf = [ei / tot for ei in ex]

        def first_max(vals):
            best = functools.reduce(jnp.maximum, vals)
            arg = jnp.full(best.shape, len(vals) - 1, I32)
            for j in range(len(vals) - 2, -1, -1):
                arg = jnp.where(vals[j] == best, j, arg)
            return best, arg

        p1, j1 = first_max(pf)
        p2, j2 = first_max([jnp.where(j1 == j, -1.0, pf[j]) for j in range(MOE_EPG)])
        den = p1 + p2
        e_ref[0:1, :] = grp * MOE_EPG + j1
        e_ref[1:2, :] = grp * MOE_EPG + j2
        g_ref[0:1, :] = pg * p1 / den
        g_ref[1:2, :] = pg * p2 / den

    out = pl.BlockSpec((MOE_TOP_K, tm), lambda i: (0, i))
    return pl.pallas_call(
        kern, grid=(N // tm,),
        in_specs=[pl.BlockSpec((tm, D), lambda i: (i, 0)),
                  pl.BlockSpec((ROUTER_ROWS, D), lambda i: (0, 0)),
                  pl.BlockSpec((ROUTER_ROWS, 1), lambda i: (0, 0))],
        out_specs=[out, out],
        out_shape=[jax.ShapeDtypeStruct((MOE_TOP_K, N), I32), jax.ShapeDtypeStruct((MOE_TOP_K, N), F32)],
        compiler_params=_cp(("parallel",)), name="moe_router",
    )(x, w_t, bias)


def moe_experts(x3, tok, aid, nvalid, bexp, nused, w_gate, w_up, w_down, layer, rows, n_assign, tf=256):
    _, nc, _ = x3.shape
    D = nc * LANE
    n_blocks = tok.shape[0] // rows
    FF = w_gate.shape[3]
    nf = FF // tf

    def kern(tok_ref, aid_ref, nv_ref, bexp_ref, nused_ref, x3_ref, wg_ref, wu_ref, wd_ref, y_ref,
             xbuf, xb, acc, ybuf, sem_in, sem_out):
        blk, f = pl.program_id(0), pl.program_id(1)
        nused = nused_ref[0]

        def gather_start(b, slot):
            def body(r, carry):
                pltpu.make_async_copy(x3_ref.at[tok_ref[b * rows + r]], xbuf.at[slot, r], sem_in.at[slot]).start()
                return carry
            lax.fori_loop(0, rows, body, 0, unroll=SUBLANE)

        def gather_wait(slot):
            pltpu.make_async_copy(x3_ref.at[pl.ds(0, rows)], xbuf.at[slot], sem_in.at[slot]).wait()

        @pl.when((blk < nused) & (f == 0))
        def _():
            slot = blk % 2

            @pl.when(blk == 0)
            def _():
                gather_start(0, 0)

            gather_wait(slot)

            @pl.when(blk + 1 < nused)
            def _():
                gather_start(blk + 1, 1 - slot)

            xb[...] = xbuf[slot].reshape(rows, D).astype(BF16)
            acc[...] = jnp.zeros_like(acc)

        @pl.when(blk < nused)
        def _():
            x = xb[...]
            hg = _mxu(x, wg_ref[...].astype(BF16), False)
            hu = _mxu(x, wu_ref[...].astype(BF16), False)
            hid = hg * _sigmoid(hg) * hu
            acc[...] += _dot(hid, wd_ref[...])

        @pl.when((blk < nused) & (f == nf - 1))
        def _():
            ybuf[...] = _token_major(acc[...])
            nv = nv_ref[blk]

            def body(r, carry):
                pltpu.make_async_copy(ybuf.at[r], y_ref.at[aid_ref[blk * rows + r]], sem_out.at[0]).start()
                return carry
            lax.fori_loop(0, nv, body, 0)

            @pl.when(nv > 0)
            def _():
                pltpu.make_async_copy(ybuf.at[pl.ds(0, nv)], y_ref.at[pl.ds(0, nv)], sem_out.at[0]).wait()

    def live(b, nu):
        return jnp.maximum(jnp.minimum(b, nu[0] - 1), 0)

    def feff(b, f, nu):
        return jnp.where(b < nu[0], f, nf - 1)

    gs = pltpu.PrefetchScalarGridSpec(
        num_scalar_prefetch=5, grid=(n_blocks, nf),
        in_specs=[pl.BlockSpec(memory_space=pl.ANY),
                  pl.BlockSpec((None, None, D, tf), lambda b, f, t, a, nv, be, nu: (layer, be[live(b, nu)], 0, feff(b, f, nu))),
                  pl.BlockSpec((None, None, D, tf), lambda b, f, t, a, nv, be, nu: (layer, be[live(b, nu)], 0, feff(b, f, nu))),
                  pl.BlockSpec((None, None, tf, D), lambda b, f, t, a, nv, be, nu: (layer, be[live(b, nu)], feff(b, f, nu), 0))],
        out_specs=pl.BlockSpec(memory_space=pl.ANY),
        scratch_shapes=[pltpu.VMEM((2, rows, nc, LANE), F32), pltpu.VMEM((rows, D), BF16),
                        pltpu.VMEM((rows, D), F32), pltpu.VMEM((rows, nc, LANE), F32),
                        pltpu.SemaphoreType.DMA((2,)), pltpu.SemaphoreType.DMA((1,))],
    )
    return pl.pallas_call(
        kern, grid_spec=gs, out_shape=jax.ShapeDtypeStruct((n_assign, nc, LANE), F32),
        compiler_params=_cp(("arbitrary", "arbitrary")), name="moe_experts",
    )(tok, aid, nvalid, bexp, nused, x3, w_gate, w_up, w_down)


def moe_combine_ln(x, ytok, gate, ln_g, ln_b, idx, alpha, n_tok, tm=128):
    D = x.shape[1]
    nc = D // LANE
    tm = min(tm, n_tok)

    def kern(x_ref, y_ref, gt_ref, g_ref, b_ref, o_ref, ob_ref):
        gt = _bf16r(gt_ref[...])
        mix = sum(gt[:, k:k + 1] * _bf16r(y_ref[:, k].reshape(tm, D)) for k in range(MOE_TOP_K))
        y = _layer_norm(alpha * x_ref[...] + mix, g_ref[...], b_ref[...])
        o_ref[...] = y
        ob_ref[...] = y.astype(BF16)

    row = pl.BlockSpec((tm, D), lambda i: (i, 0))
    par = pl.BlockSpec((None, 1, D), lambda i: (idx, 0, 0))
    return pl.pallas_call(
        kern, grid=(n_tok // tm,),
        in_specs=[row, pl.BlockSpec((tm, MOE_TOP_K, nc, LANE), lambda i: (i, 0, 0, 0)),
                  pl.BlockSpec((tm, MOE_TOP_K), lambda i: (i, 0)), par, par],
        out_specs=[row, row],
        out_shape=[jax.ShapeDtypeStruct((n_tok, D), F32), jax.ShapeDtypeStruct((n_tok, D), BF16)],
        compiler_params=_cp(("parallel",)), name="moe_combine_ln",
    )(x, ytok, gate, ln_g, ln_b)


def hier_moe_ln(x, x3, n_tok, w_rt, b_rt, w_gate, w_up, w_down, layer, ln_g, ln_b, alpha, rows):
    N, D = x.shape
    xr = x if N % LANE == 0 else jnp.pad(x, ((0, LANE - N % LANE), (0, 0)))
    eidx, gate = moe_router(xr, w_rt, b_rt)
    eidx, gate = eidx[:, :n_tok], gate[:, :n_tok]
    A = n_tok * MOE_TOP_K
    e = eidx.T.reshape(A)
    onehot = (e[:, None] == jnp.arange(MOE_EXPERTS, dtype=I32)[None, :]).astype(I32)
    csum = jnp.cumsum(onehot, axis=0)
    counts = csum[-1]
    pos = jnp.sum(csum * onehot, axis=1) - 1
    padded = (counts + rows - 1) // rows * rows
    pad_end = jnp.cumsum(padded)
    pad_start = pad_end - padded
    dest = (jnp.sum(onehot * pad_start[None, :], axis=1) + pos).astype(I32)
    n_blocks = min(A // rows + MOE_EXPERTS, A)
    bexp = jnp.minimum(jnp.searchsorted(pad_end, jnp.arange(n_blocks, dtype=I32) * rows, side="right"),
                       MOE_EXPERTS - 1).astype(I32)
    nused = (pad_end[-1:] // rows).astype(I32)
    nvalid = jnp.clip((pad_start + counts)[bexp] - jnp.arange(n_blocks, dtype=I32) * rows, 0, rows).astype(I32)
    aid = jnp.zeros((n_blocks * rows,), I32).at[dest].set(jnp.arange(A, dtype=I32))
    ytok = moe_experts(x3, aid // MOE_TOP_K, aid, nvalid, bexp, nused, w_gate, w_up, w_down, layer, rows, A)
    ytok = ytok.reshape(n_tok, MOE_TOP_K, D // LANE, LANE)
    return moe_combine_ln(x, ytok, gate.T, ln_g, ln_b, layer * 3 + 2, alpha, n_tok)


def _rope_tables(pos):
    half = HEAD_DIM // 2
    inv_freq = ROPE_THETA ** (-jnp.arange(half, dtype=F32) / half)
    ang = pos.astype(F32)[:, None] * inv_freq[None, :]
    cos, sin = jnp.cos(ang), jnp.sin(ang)
    return jnp.concatenate([cos, cos], axis=1), jnp.concatenate([-sin, sin], axis=1)


def _even_projections(x, pos_rows, w_in, w_rest, w_gates, e, tm, q_dtype):
    G, d = NSA_G, HEAD_DIM
    q_cols = G * NSA_R * d
    kv_cols = N_KV_STREAMS * G * d
    cos, sin = _rope_tables(pos_rows)
    tn = G * d
    mm = functools.partial(matmul, x, tm=tm)
    q = mm(w_in, layer=e, col_off=0, n_out=q_cols, tn=tn, rope=(cos, sin, tuple(range(q_cols // tn))),
           out_dtype=q_dtype)
    kv = mm(w_in, layer=e, col_off=q_cols, n_out=kv_cols, tn=tn, rope=(cos, sin, (0, 2)))
    win = mm(w_in, layer=e, col_off=q_cols + kv_cols, n_out=2 * G * d, tn=tn, rope=(cos, sin, (0,)))
    gates = mm(w_gates, tn=G * LANE)
    rw = w_rest.shape[1] // 2
    u = mm(w_rest, n_out=rw)
    gb = mm(w_rest, col_off=rw, n_out=rw)
    return q, kv, win, gates, u, gb


def _pad_rows(a, n):
    return jnp.pad(a, ((0, n - a.shape[0]),) + ((0, 0),) * (a.ndim - 1))


def kernel(x_prompt, x_sample, mem_prompt, cache_nsa_kv, cache_nsa_win, state_rglru_h, state_rglru_conv,
           state_pool, cache_mem_kv, page_table, w_in, w_out, cmp_w1, cmp_w2, cmp_pe, conv_w, conv_b,
           lru_wa, lru_ba, lru_wx, lru_bx, lru_lambda, pool_w, pool_scale, mem_wq, mem_wk, mem_wv, mem_wo,
           ln_g, ln_b, moe_w_coarse, moe_b_coarse, moe_w_fine, moe_b_fine, moe_w_gate, moe_w_up, moe_w_down):
    B, T, D = x_prompt.shape
    DB = x_sample.shape[0]
    depth = ln_g.shape[0]
    n_pages = page_table.shape[1]
    past_len = n_pages * PAGE_SIZE
    n_phys = cache_nsa_kv.shape[1]
    G, R, d = NSA_G, NSA_R, HEAD_DIM
    alpha = (2 * depth) ** 0.25
    q_cols, kv_cols, win_cols, gate_cols = G * R * d, N_KV_STREAMS * G * d, 2 * G * d, 3 * G * R
    rest_off = q_cols + kv_cols + win_cols + gate_cols
    RW = (w_in.shape[2] - rest_off) // 2
    MW = mem_wq.shape[2]
    M = mem_prompt.shape[1]
    SR = SAMPLE_ROWS
    TS = SAMPLE_ROWS

    xp = x_prompt.reshape(B * T, D)
    xpb = xp.astype(BF16)
    xs = _pad_rows(x_sample.reshape(DB, D), SR)
    lng = ln_g.reshape(depth * 3, 1, D)
    lnb = ln_b.reshape(depth * 3, 1, D)
    memf = mem_prompt.reshape(B * M, D)

    kv_p, kv_s, win_p, win_s, h_p, h_s, cv_p, cv_s, pl_p, pl_s, mem_p = ([] for _ in range(11))
    for layer in range(depth):
        if layer % 2 == 0:
            e = layer // 2
            wg = w_in[e, :, q_cols + kv_cols + win_cols:rest_off].reshape(D, G, 3 * R)
            wg = jnp.pad(wg, ((0, 0), (0, 0), (0, LANE - 3 * R))).reshape(D, G * LANE)
            w_rest = w_in[e, :, rest_off:]

            q, kv, win, gates, u, gb = _even_projections(
                xpb, jnp.arange(T), w_in, w_rest, wg, e, 1024, BF16)
            new_kv = kv.reshape(B, T, N_KV_STREAMS, G, d)
            pages_p = new_kv.reshape(B * T // PAGE_SIZE, PAGE_SIZE, N_KV_STREAMS, G, d)
            ab = compress_ab(pages_p, jnp.arange(B * T // PAGE_SIZE, dtype=I32), cmp_w1, cmp_pe, e, B,
                             T // PAGE_SIZE)
            kvcmp = compress_fin(ab, cmp_w2, e)
            o_nsa = nsa_prompt(q, kv, win, kvcmp, gates, B, T)
            y_rnn, h_all = rglru(u, gb, jnp.zeros((B, SUBLANE, RW), F32), jnp.zeros((B, 1, RW), F32),
                                 conv_w, conv_b, lru_wa, lru_ba, lru_wx, lru_bx, lru_lambda, e, B, T)
            mix_p = matmul(o_nsa, w_out, layer=e, x2=y_rnn)
            keep = min(WINDOW, T)
            kv_p.append(new_kv)
            win_p.append(win.reshape(B, T, 2, G, d)[:, T - keep:])
            h_p.append(h_all.reshape(B, T, RW)[:, T - 1])
            cv_p.append(u.reshape(B, T, RW)[:, T - (CONV_WIDTH - 1):])

            q, kv, win, gates, u, gb = _even_projections(
                xs, jnp.full((SR,), past_len), w_in, w_rest, wg, e, SR, F32)
            new_kv_s = kv[:DB].reshape(DB, N_KV_STREAMS, G, d)
            new_win_s = win[:DB].reshape(DB, 2, G, d)
            pages_s = cache_nsa_kv.reshape(cache_nsa_kv.shape[0] * n_phys, PAGE_SIZE, N_KV_STREAMS, G, d)
            pid = (page_table.reshape(DB * n_pages) + e * n_phys).astype(I32)
            ab = compress_ab(pages_s, pid, cmp_w1, cmp_pe, e, DB, n_pages)
            kvcmp = compress_fin(ab, cmp_w2, e)
            q8 = jnp.pad(q[:DB].reshape(DB, G, R, d), ((0, 0), (0, 0), (0, SUBLANE - R), (0, 0)))
            o_cmp, imp = nsa_sample_cmp(q8, kvcmp, past_len)
            idx_t = nsa_sample_topk(_pad_rows(imp.reshape(DB * SUBLANE, -1), LANE), past_len)
            n_top = idx_t.shape[0]
            idx = idx_t[:, :DB * SUBLANE].T.reshape(DB, SUBLANE, n_top)[:, :G].reshape(DB * G * n_top)
            o8 = nsa_sample_attend(q8, o_cmp, gates[:DB].reshape(DB, G, 1, LANE), idx, pid, pages_s,
                                   new_kv_s, cache_nsa_win, new_win_s, e, past_len)
            o_nsa = _pad_rows(o8[:, :, :R].reshape(DB, G * R * d), SR)
            conv0 = state_rglru_conv[e]
            u_t = jnp.pad(u[:DB, None, :], ((0, 0), (0, TS - 1), (0, 0))).reshape(DB * TS, RW)
            gb_t = jnp.pad(gb[:DB, None, :], ((0, 0), (0, TS - 1), (0, 0))).reshape(DB * TS, RW)
            c0 = jnp.pad(conv0, ((0, 0), (SUBLANE - (CONV_WIDTH - 1), 0), (0, 0)))
            y_t, h_t = rglru(u_t, gb_t, c0, state_rglru_h[e][:, None, :], conv_w, conv_b, lru_wa, lru_ba,
                             lru_wx, lru_bx, lru_lambda, e, DB, TS)
            y_rnn = _pad_rows(y_t.reshape(DB, TS, RW)[:, 0], SR)
            mix_s = matmul(o_nsa, w_out, layer=e, x2=y_rnn)
            keep = min(WINDOW, past_len + 1)
            win_all = jnp.concatenate([cache_nsa_win[e], new_win_s[:, None]], axis=1)
            kv_s.append(new_kv_s[:, None])
            win_s.append(win_all[:, win_all.shape[1] - keep:])
            h_s.append(h_t.reshape(DB, TS, RW)[:, 0])
            cv_s.append(jnp.concatenate([conv0, u[:DB, None, :]], axis=1)[:, 1:])
        else:
            o = layer // 2
            mix_p = pool_mixer(xp, jnp.zeros((B, 2 * SUBLANE, D), F32), pool_w, pool_scale, o, B, T, 0)
            pl_p.append(xp.reshape(B, T, D)[:, T - POOL_BUF:])
            tsp = 2 * SUBLANE
            x_t = jnp.pad(xs[:DB, None, :], ((0, 0), (0, tsp - 1), (0, 0))).reshape(DB * tsp, D)
            buf = jnp.pad(state_pool[o], ((0, 0), (tsp - POOL_BUF, 0), (0, 0)))
            mix_t = pool_mixer(x_t, buf, pool_w, pool_scale, o, DB, tsp, past_len)
            mix_s = _pad_rows(mix_t.reshape(DB, tsp, D)[:, 0], SR)
            pl_s.append(jnp.concatenate([state_pool[o], xs[:DB, None, :]], axis=1)[:, 1:])

        xp, xpb = add_ln(xp, mix_p, lng, lnb, layer * 3, alpha)
        xs, _ = add_ln(xs, mix_s, lng, lnb, layer * 3, alpha)

        w_kv = jnp.concatenate([mem_wk[layer], mem_wv[layer]], axis=1)
        mkv = matmul(memf, w_kv).reshape(B, M, 2, MW)
        mem_p.append(mkv)
        qm = matmul(xpb, mem_wq, layer=layer, out_dtype=BF16)
        att = mem_attn(qm, mkv, B, T)
        xp, xpb, xp3 = add_ln(xp, matmul(att, mem_wo, layer=layer), lng, lnb, layer * 3 + 1, alpha,
                              token_major=True)
        qm = matmul(xs, mem_wq, layer=layer)
        qm_t = jnp.pad(qm[:DB, None, :], ((0, 0), (0, TS - 1), (0, 0))).reshape(DB * TS, MW)
        att_t = mem_attn(qm_t, cache_mem_kv.reshape((-1,) + cache_mem_kv.shape[2:]), DB, TS,
                         kv_off=layer * DB)
        att = _pad_rows(att_t.reshape(DB, TS, MW)[:, 0], SR)
        xs, _, xs3 = add_ln(xs, matmul(att, mem_wo, layer=layer), lng, lnb, layer * 3 + 1, alpha,
                            token_major=True)

        w_rt = _pad_rows(jnp.concatenate([moe_w_coarse[layer], moe_w_fine[layer]], axis=1).T, ROUTER_ROWS)
        b_rt = _pad_rows(jnp.concatenate([moe_b_coarse[layer], moe_b_fine[layer]])[:, None], ROUTER_ROWS)
        xp, xpb = hier_moe_ln(xp, xp3, B * T, w_rt, b_rt, moe_w_gate, moe_w_up, moe_w_down, layer, lng, lnb,
                              alpha, rows=256)
        xs = _pad_rows(hier_moe_ln(xs, xs3, DB, w_rt, b_rt, moe_w_gate, moe_w_up, moe_w_down, layer, lng, lnb,
                                   alpha, rows=SUBLANE)[0], SR)

    return (xp.reshape(B, T, D), xs[:DB].reshape(DB, 1, D), jnp.stack(kv_p), jnp.stack(kv_s),
            jnp.stack(win_p), jnp.stack(win_s), jnp.stack(h_p), jnp.stack(h_s), jnp.stack(cv_p),
            jnp.stack(cv_s), jnp.stack(pl_p), jnp.stack(pl_s), jnp.stack(mem_p))
```

```python
import functools
import math

import jax
import jax.numpy as jnp
from jax import lax
from jax.experimental import pallas as pl
from jax.experimental.pallas import tpu as pltpu

F32 = jnp.float32
BF16 = jnp.bfloat16
I32 = jnp.int32

HEAD_DIM = 128
NSA_G = 4
NSA_R = 4
N_KV_STREAMS = 4
CMP_BLOCK = 32
CMP_STRIDE = 16
SEL_BLOCK = 64
SEL_TOP_N = 16
WINDOW = 512
FORCE_SCORE = 1e9
NEG = -1e30
ROPE_THETA = 10000.0
CONV_WIDTH = 4
LRU_C = 8.0
RNN_BLOCK_W = 128
POOL_WINDOWS = (2, 4, 8, 16)
POOL_BUF = 15
MEM_HEADS = 4
MOE_GROUPS = 4
MOE_EPG = 8
MOE_EXPERTS = MOE_GROUPS * MOE_EPG
MOE_TOP_K = 2
LN_EPS = 1e-5
PAGE_SIZE = 128

LANE = 128
SUBLANE = 8
VMEM_LIMIT = 56 * 1024 * 1024
SAMPLE_ROWS = 16


def _cp(sem, vmem=VMEM_LIMIT):
    return pltpu.CompilerParams(dimension_semantics=sem, vmem_limit_bytes=vmem)


def _split3(x):
    h = x.astype(BF16)
    r = x - h.astype(F32)
    m = r.astype(BF16)
    l = (r - m.astype(F32)).astype(BF16)
    return h, m, l


def _mxu(a, b, nt):
    if nt:
        return lax.dot_general(a, b, (((1,), (1,)), ((), ())), preferred_element_type=F32)
    return jnp.dot(a, b, preferred_element_type=F32)


def _dot(a, b, nt=False):
    return _mxu(a.astype(BF16), b.astype(BF16), nt)


def _bf16r(x):
    return x.astype(BF16).astype(F32)


def _sigmoid(x):
    return 1.0 / (1.0 + jnp.exp(-x))


def _gelu(x):
    return 0.5 * x * (1.0 + jnp.tanh(0.7978845608028654 * (x + 0.044715 * (x * x * x))))


def _msoftmax(s, mask):
    s = jnp.where(mask, s, NEG)
    m = jnp.max(s, axis=-1, keepdims=True)
    p = jnp.where(mask, jnp.exp(s - m), 0.0)
    return p / jnp.maximum(jnp.sum(p, axis=-1, keepdims=True), 1e-30)


def _msoftmax_nonempty(s, mask):
    s = jnp.where(mask, s, NEG)
    p = jnp.exp(s - jnp.max(s, axis=-1, keepdims=True))
    return p * (1.0 / jnp.sum(p, axis=-1, keepdims=True))


def _layer_norm(v, g, b):
    mu = jnp.mean(v, axis=-1, keepdims=True)
    c = v - mu
    var = jnp.mean(c * c, axis=-1, keepdims=True)
    return c * lax.rsqrt(var + LN_EPS) * g + b


def matmul(x, w, *, layer=0, col_off=0, n_out=None, x2=None, tm=1024, tn=512, rope=None, out_dtype=F32):
    if w.ndim == 2:
        w = w[None]
    M, K1 = x.shape
    K2 = 0 if x2 is None else x2.shape[1]
    assert w.shape[1] == K1 + K2
    n_out = w.shape[2] - col_off if n_out is None else n_out
    xs_in = [x] if x2 is None else [x, x2]
    if any(a.dtype != BF16 for a in xs_in):
        tm = min(tm, 512)
    tm, tn = min(tm, M), min(tn, n_out)
    assert M % tm == 0 and n_out % tn == 0 and col_off % tn == 0
    joff = col_off // tn

    in_specs = [pl.BlockSpec((tm, a.shape[1]), lambda i, j: (i, 0)) for a in xs_in]
    in_specs.append(pl.BlockSpec((None, K1 + K2, tn), lambda i, j: (layer, 0, j + joff)))
    args = xs_in + [w]
    rope_blocks = ()
    if rope is not None:
        cos, sin, rope_blocks = rope
        period = cos.shape[0] // tm
        assert cos.shape[0] % tm == 0
        for t in (cos, sin):
            in_specs.append(pl.BlockSpec((tm, LANE), lambda i, j: (i % period, 0)))
            args.append(t)
    cast = [a.dtype != BF16 for a in xs_in]
    scratch = [pltpu.VMEM((tm, a.shape[1]), BF16) for a, c in zip(xs_in, cast) if c]

    def kern(*refs):
        refs = list(refs)
        x_refs = [refs.pop(0) for _ in xs_in]
        w_ref = refs.pop(0)
        cos_ref, sin_ref = (refs.pop(0), refs.pop(0)) if rope is not None else (None, None)
        o_ref = refs.pop(0)
        j = pl.program_id(1)
        xb_refs = [refs.pop(0) if c else xr for xr, c in zip(x_refs, cast)]

        if any(cast):
            @pl.when(j == 0)
            def _():
                for xr, xb, c in zip(x_refs, xb_refs, cast):
                    if c:
                        xb[...] = xr[...].astype(BF16)

        acc = _mxu(xb_refs[0][...], w_ref[0:K1, :].astype(BF16), False)
        if x2 is not None:
            acc = acc + _mxu(xb_refs[1][...], w_ref[K1:K1 + K2, :].astype(BF16), False)
        if rope is None:
            o_ref[...] = acc.astype(out_dtype)
            return
        rot = functools.reduce(jnp.logical_or, [j == c for c in rope_blocks])

        @pl.when(rot)
        def _():
            c, s = cos_ref[...], sin_ref[...]
            for h in range(tn // LANE):
                seg = acc[:, h * LANE:(h + 1) * LANE]
                o_ref[:, h * LANE:(h + 1) * LANE] = (seg * c + pltpu.roll(seg, LANE // 2, 1) * s).astype(out_dtype)

        @pl.when(jnp.logical_not(rot))
        def _():
            o_ref[...] = acc.astype(out_dtype)

    return pl.pallas_call(
        kern,
        grid=(M // tm, n_out // tn),
        in_specs=in_specs,
        out_specs=pl.BlockSpec((tm, tn), lambda i, j: (i, j)),
        out_shape=jax.ShapeDtypeStruct((M, n_out), out_dtype),
        scratch_shapes=scratch,
        compiler_params=_cp(("parallel", "arbitrary")),
        name="matmul",
    )(*args)


def _token_major(v):
    return v.reshape(v.shape[0], v.shape[1] // LANE, LANE)


def add_ln(x, f, ln_g, ln_b, idx, alpha, tm=256, token_major=False):
    M, D = x.shape
    tm = min(tm, M)

    def kern(x_ref, f_ref, g_ref, b_ref, o_ref, ob_ref, *o3_ref):
        y = _layer_norm(alpha * x_ref[...] + f_ref[...], g_ref[...], b_ref[...])
        o_ref[...] = y
        ob_ref[...] = y.astype(BF16)
        if token_major:
            o3_ref[0][...] = _token_major(y)

    row = pl.BlockSpec((tm, D), lambda i: (i, 0))
    par = pl.BlockSpec((None, 1, D), lambda i: (idx, 0, 0))
    out_specs = [row, row]
    out_shape = [jax.ShapeDtypeStruct((M, D), F32), jax.ShapeDtypeStruct((M, D), BF16)]
    if token_major:
        out_specs.append(pl.BlockSpec((tm, D // LANE, LANE), lambda i: (i, 0, 0)))
        out_shape.append(jax.ShapeDtypeStruct((M, D // LANE, LANE), F32))
    return pl.pallas_call(
        kern, grid=(M // tm,), in_specs=[row, row, par, par], out_specs=out_specs,
        out_shape=out_shape, compiler_params=_cp(("parallel",)), name="add_ln",
    )(x, f, ln_g, ln_b)


CMP_PAGES = 8


def compress_ab(pages, page_ids, cmp_w1, cmp_pe, e, n_batch, pages_per_seq):
    P = CMP_PAGES
    assert pages_per_seq % P == 0
    cpp = PAGE_SIZE // CMP_STRIDE
    steps = pages_per_seq // P
    d = HEAD_DIM
    n_chunks = pages_per_seq * cpp
    rows = NSA_G * P * cpp

    def kern(pid_ref, *refs):
        page_refs, w_ref, pe_ref, o_ref = refs[:P], refs[P], refs[P + 1], refs[P + 2]
        for s in range(2):
            acc_a = jnp.zeros((rows, d), F32)
            acc_b = jnp.zeros((rows, d), F32)
            def chunk_rows(l):
                return jnp.concatenate(
                    [page_refs[i][pl.ds(l, cpp, stride=CMP_STRIDE), s, g, :]
                     for g in range(NSA_G) for i in range(P)], axis=0)

            def pair(l, off):
                lhs = jnp.concatenate([rows2[0] + pe_ref[s, off + l:off + l + 1, :],
                                       rows2[1] + pe_ref[s, off + l + 1:off + l + 2, :]], axis=1)
                return _dot(lhs, w_ref[s, off + l:off + l + 2].reshape(2 * d, d))

            for l in range(0, CMP_STRIDE, 2):
                rows2 = (chunk_rows(l), chunk_rows(l + 1))
                acc_a = acc_a + pair(l, 0)
                acc_b = acc_b + pair(l, CMP_STRIDE)
            for g in range(NSA_G):
                c0 = (s * NSA_G + g) * 2 * d
                o_ref[:, c0:c0 + d] = acc_a[g * P * cpp:(g + 1) * P * cpp]
                o_ref[:, c0 + d:c0 + 2 * d] = acc_b[g * P * cpp:(g + 1) * P * cpp]

    def page_spec(i):
        return pl.BlockSpec((None, PAGE_SIZE, 2, NSA_G, d),
                            lambda b, st, pid: (pid[(b * steps + st) * P + i], 0, 0, 0, 0))

    gs = pltpu.PrefetchScalarGridSpec(
        num_scalar_prefetch=1,
        grid=(n_batch, steps),
        in_specs=[page_spec(i) for i in range(P)]
        + [pl.BlockSpec((None, 2, CMP_BLOCK, d, d), lambda b, st, pid: (e, 0, 0, 0, 0)),
           pl.BlockSpec((None, 2, CMP_BLOCK, d), lambda b, st, pid: (e, 0, 0, 0))],
        out_specs=pl.BlockSpec((None, P * cpp, 2 * NSA_G * 2 * d), lambda b, st, pid: (b, st, 0)),
    )
    return pl.pallas_call(
        kern, grid_spec=gs,
        out_shape=jax.ShapeDtypeStruct((n_batch, n_chunks, 2 * NSA_G * 2 * d), F32),
        compiler_params=_cp(("parallel", "parallel")), name="compress_ab",
    )(page_ids, *([pages] * P), cmp_w1, cmp_pe)


def compress_fin(ab, cmp_w2, e):
    nb, nch, _ = ab.shape
    d = HEAD_DIM

    def kern(ab_ref, w2_ref, o_ref):
        for g in range(NSA_G):
            a = ab_ref[:, g * 2 * d:g * 2 * d + d]
            bn = ab_ref[:, g * 2 * d + d:(g + 1) * 2 * d]
            h = a + pltpu.roll(bn, nch - 1, 0)
            o_ref[g] = _dot(_gelu(h), w2_ref[...])

    return pl.pallas_call(
        kern, grid=(nb, 2),
        in_specs=[pl.BlockSpec((None, nch, NSA_G * 2 * d), lambda b, s: (b, 0, s)),
                  pl.BlockSpec((None, None, d, d), lambda b, s: (e, s, 0, 0))],
        out_specs=pl.BlockSpec((None, None, NSA_G, nch, d), lambda b, s: (s, b, 0, 0, 0)),
        out_shape=jax.ShapeDtypeStruct((2, nb, NSA_G, nch, d), F32),
        compiler_params=_cp(("parallel", "parallel")), name="compress_fin",
    )(ab, cmp_w2)


def _overlap_matrix(n_rows, n_cols):
    ci = jnp.arange(n_rows)[:, None] * CMP_STRIDE
    sj = jnp.arange(n_cols)[None, :] * SEL_BLOCK
    return ((ci < sj + SEL_BLOCK) & (ci + CMP_BLOCK > sj)).astype(BF16)


NSA_TQ = 128


def nsa_prompt(q, kv, win, kvcmp, gates, B, T):
    d, G, R, tq = HEAD_DIM, NSA_G, NSA_R, NSA_TQ
    nq = T // tq
    ncp = kvcmp.shape[3]
    n_cmp = T // CMP_STRIDE - 1
    n_sel = -(-T // SEL_BLOCK)
    n_top = min(SEL_TOP_N, n_sel)
    wk = min(T, WINDOW + tq)
    key_span = min(T, 4 * tq)
    assert n_sel <= LANE and T % tq == 0 and T % key_span == 0
    scale = d ** -0.5
    overlap = _overlap_matrix(ncp, LANE)
    expand = (jnp.arange(LANE)[:, None] == (jnp.arange(T) // SEL_BLOCK)[None, :]).astype(BF16)

    def kern(q_ref, ks_ref, vs_ref, kw_ref, vw_ref, kc_ref, vc_ref, g_ref, ov_ref, ex_ref, o_ref,
             ksb, vsb, kwb, vwb, kcb, vcb):
        qi = pl.program_id(2)

        @pl.when(qi == 0)
        def _():
            ksb[...] = ks_ref[...].astype(BF16)
            vsb[...] = vs_ref[...].astype(BF16)
            kwb[...] = kw_ref[...].astype(BF16)
            vwb[...] = vw_ref[...].astype(BF16)
            kcb[...] = kc_ref[...].astype(BF16)
            vcb[...] = vc_ref[...].astype(BF16)

        t0 = qi * tq
        qb = q_ref[...]
        qs = jnp.concatenate([qb[:, r * d:(r + 1) * d] for r in range(R)], axis=0).astype(BF16)
        qpos_s = t0 + (lax.broadcasted_iota(I32, (R * tq, 1), 0) & (tq - 1))

        s = _mxu(qs, kcb[...], True) * scale
        n_idx = lax.broadcasted_iota(I32, (1, ncp), 1)
        cmask = (n_idx * CMP_STRIDE + (CMP_BLOCK - 1) <= qpos_s) & (n_idx < n_cmp)
        p = _msoftmax(s, cmask)
        pb = p.astype(BF16)
        o_cmp = _mxu(pb, vcb[...], False)
        imp_r = _mxu(pb, ov_ref[...], False)
        imp = imp_r[0:tq]
        for r in range(1, R):
            imp = imp + imp_r[r * tq:(r + 1) * tq]

        blk = lax.broadcasted_iota(I32, (tq, LANE), 1)
        qpos = t0 + lax.broadcasted_iota(I32, (tq, LANE), 0)
        cur = qpos // SEL_BLOCK
        valid = blk <= cur
        forced = (blk == 0) | (blk == cur) | (blk == cur - 1)
        score = jnp.where(forced, FORCE_SCORE, jnp.where(valid, imp, NEG))
        cnt = jnp.zeros((tq, LANE), I32)
        for i in range(n_sel):
            col = score[:, i:i + 1]
            beats = (col > score) | ((col == score) & (blk > i))
            cnt = cnt + beats.astype(I32)
        sel = ((cnt < n_top) & valid).astype(BF16)

        kstart = pl.multiple_of(jnp.clip(t0 - WINDOW, 0, T - wk), tq)
        kwin = kwb[pl.ds(kstart, wk), :]
        vwin = vwb[pl.ds(kstart, wk), :]
        s_w = _mxu(qs, kwin, True) * scale
        kp = kstart + lax.broadcasted_iota(I32, (1, wk), 1)
        wmask = (kp <= qpos_s) & (kp > qpos_s - WINDOW)
        p_w = _msoftmax_nonempty(s_w, wmask)
        o_win = _mxu(p_w.astype(BF16), vwin, False)

        gt = _sigmoid(g_ref[...])

        def selected(nkeys):
            selx = _mxu(sel, ex_ref[:, 0:nkeys], False)
            kpos = lax.broadcasted_iota(I32, (tq, nkeys), 1)
            qpos_t = t0 + lax.broadcasted_iota(I32, (tq, nkeys), 0)
            smask = (selx > 0.5) & (kpos <= qpos_t)
            for r in range(R):
                s_r = _mxu(qs[r * tq:(r + 1) * tq], ksb[0:nkeys, :], True) * scale
                p_r = _msoftmax_nonempty(s_r, smask)
                o_slc = _mxu(p_r.astype(BF16), vsb[0:nkeys, :], False)
                o_r = (gt[:, 3 * r:3 * r + 1] * o_cmp[r * tq:(r + 1) * tq]
                       + gt[:, 3 * r + 1:3 * r + 2] * o_slc
                       + gt[:, 3 * r + 2:3 * r + 3] * o_win[r * tq:(r + 1) * tq])
                o_ref[:, r * d:(r + 1) * d] = o_r.astype(BF16)

        n_span = pl.cdiv(t0 + tq, key_span)
        for n in range(1, T // key_span + 1):
            pl.when(n_span == n)(functools.partial(selected, n * key_span))

    seq = lambda blkcol: pl.BlockSpec((T, d), lambda b, g, qi: (b, blkcol(g)))
    cmp_spec = lambda s: pl.BlockSpec((None, None, None, ncp, d), lambda b, g, qi: (s, b, g, 0, 0))
    return pl.pallas_call(
        kern, grid=(B, G, nq),
        in_specs=[pl.BlockSpec((tq, R * d), lambda b, g, qi: (b * nq + qi, g)),
                  seq(lambda g: 2 * G + g), seq(lambda g: 3 * G + g),
                  seq(lambda g: g), seq(lambda g: G + g),
                  cmp_spec(0), cmp_spec(1),
                  pl.BlockSpec((tq, LANE), lambda b, g, qi: (b * nq + qi, g)),
                  pl.BlockSpec((ncp, LANE), lambda b, g, qi: (0, 0)),
                  pl.BlockSpec((LANE, T), lambda b, g, qi: (0, 0))],
        out_specs=pl.BlockSpec((tq, R * d), lambda b, g, qi: (b * nq + qi, g)),
        out_shape=jax.ShapeDtypeStruct((B * T, G * R * d), BF16),
        scratch_shapes=[pltpu.VMEM((T, d), BF16)] * 4 + [pltpu.VMEM((ncp, d), BF16)] * 2,
        compiler_params=_cp(("parallel", "parallel", "arbitrary")), name="nsa_prompt",
    )(q, kv, kv, win, win, kvcmp, kvcmp, gates, overlap, expand)


def nsa_sample_cmp(q8, kvcmp, q_pos):
    B = q8.shape[0]
    d, G, R = HEAD_DIM, NSA_G, NSA_R
    nch = kvcmp.shape[3]
    n_cmp = nch - 1
    n_sel = -(-(q_pos + 1) // SEL_BLOCK)
    nsp = -(-n_sel // LANE) * LANE
    overlap = _overlap_matrix(nch, nsp)
    scale = d ** -0.5

    def kern(q_ref, k_ref, v_ref, ov_ref, o_ref, imp_ref):
        n_idx = lax.broadcasted_iota(I32, (1, nch), 1)
        cmask = (n_idx * CMP_STRIDE + (CMP_BLOCK - 1) <= q_pos) & (n_idx < n_cmp)
        rows = []
        for g in range(G):
            s = _dot(q_ref[g], k_ref[g], nt=True) * scale
            pb = _msoftmax(s, cmask).astype(BF16)
            o_ref[g] = _dot(pb, v_ref[g])
            rows.append(jnp.sum(_mxu(pb, ov_ref[...], False)[0:R], axis=0, keepdims=True))
        imp_ref[...] = jnp.concatenate(rows + [jnp.zeros((SUBLANE - G, nsp), F32)], axis=0)

    qspec = pl.BlockSpec((None, G, SUBLANE, d), lambda b: (b, 0, 0, 0))
    cspec = lambda s: pl.BlockSpec((None, None, G, nch, d), lambda b: (s, b, 0, 0, 0))
    return pl.pallas_call(
        kern, grid=(B,),
        in_specs=[qspec, cspec(0), cspec(1), pl.BlockSpec((nch, nsp), lambda b: (0, 0))],
        out_specs=[qspec, pl.BlockSpec((None, SUBLANE, nsp), lambda b: (b, 0, 0))],
        out_shape=[jax.ShapeDtypeStruct((B, G, SUBLANE, d), F32),
                   jax.ShapeDtypeStruct((B, SUBLANE, nsp), F32)],
        compiler_params=_cp(("parallel",)), name="nsa_sample_cmp",
    )(q8, kvcmp, kvcmp, overlap)


def nsa_sample_topk(imp, q_pos):
    nr, nsp = imp.shape
    assert nr == LANE
    n_sel = -(-(q_pos + 1) // SEL_BLOCK)
    n_top = min(SEL_TOP_N, n_sel)
    cur = q_pos // SEL_BLOCK

    def kern(imp_ref, o_ref, sct):
        blk = lax.broadcasted_iota(I32, (nr, nsp), 1)
        valid = blk <= cur
        forced = (blk == 0) | (blk == cur) | (blk == cur - 1)
        score = jnp.where(forced, FORCE_SCORE, jnp.where(valid, imp_ref[...], NEG))
        eye = (lax.broadcasted_iota(I32, (nsp, nsp), 0) == lax.broadcasted_iota(I32, (nsp, nsp), 1)).astype(BF16)
        h, m, l = _split3(score)
        sct[...] = (_mxu(eye, h, True) + _mxu(eye, m, True)) + _mxu(eye, l, True)
        sc = sct[...]
        jidx = lax.broadcasted_iota(I32, (nsp, nr), 0)

        def body(i, cnt):
            row = sct[pl.ds(i, 1), :]
            beats = (row > sc) | ((row == sc) & (jidx > i))
            return cnt + beats.astype(I32)

        cnt = lax.fori_loop(0, n_sel, body, jnp.zeros((nsp, nr), I32))
        for p in range(n_top):
            o_ref[p:p + 1, :] = jnp.sum(jnp.where(cnt == p, jidx, 0), axis=0, keepdims=True)

    return pl.pallas_call(
        kern, grid=(1,),
        in_specs=[pl.BlockSpec((nr, nsp), lambda i: (0, 0))],
        out_specs=pl.BlockSpec((n_top, nr), lambda i: (0, 0)),
        out_shape=jax.ShapeDtypeStruct((n_top, nr), I32),
        scratch_shapes=[pltpu.VMEM((nsp, nr), F32)],
        compiler_params=_cp(("arbitrary",)), name="nsa_sample_topk",
    )(imp)


def nsa_sample_attend(q8, o_cmp, gates, idx, page_ids, pages, new_kv, cwin, new_win, e, q_pos):
    B = q8.shape[0]
    d, G, R = HEAD_DIM, NSA_G, NSA_R
    n_top = idx.shape[0] // (B * G)
    n_pages = page_ids.shape[0] // B
    n_cache_blocks = n_pages * (PAGE_SIZE // SEL_BLOCK)
    bpp = PAGE_SIZE // SEL_BLOCK
    w0 = cwin.shape[2]
    scale = d ** -0.5
    nk = n_top * SEL_BLOCK

    def kern(idx_ref, pid_ref, q_ref, oc_ref, g_ref, pages_ref, nkv_ref, cw_ref, nw_ref, o_ref,
             kvbuf, sem):
        b = pl.program_id(0)

        def slot_copy(slot, j):
            page = pid_ref[b * n_pages + j // bpp]
            r0 = (j % bpp) * SEL_BLOCK
            return pltpu.make_async_copy(
                pages_ref.at[page, pl.ds(r0, SEL_BLOCK), pl.ds(2, 2)], kvbuf.at[slot], sem.at[0])

        for slot in range(G * n_top):
            j = idx_ref[b * G * n_top + slot]

            @pl.when(j < n_cache_blocks)
            def _():
                slot_copy(slot, j).start()

            @pl.when(j >= n_cache_blocks)
            def _():
                kvbuf[slot] = jnp.zeros((SEL_BLOCK, 2, G, d), F32)

            @pl.when(j * SEL_BLOCK == q_pos)
            def _():
                kvbuf[slot, 0] = nkv_ref[2:4]

        for slot in range(G * n_top):
            j = idx_ref[b * G * n_top + slot]

            @pl.when(j < n_cache_blocks)
            def _():
                slot_copy(slot, j).wait()

        lane_k = lax.broadcasted_iota(I32, (1, nk), 1)
        row8 = lax.broadcasted_iota(I32, (SUBLANE, 1), 0)
        for g in range(G):
            qg = q_ref[g]
            kk = jnp.concatenate([kvbuf[g * n_top + p, :, 0, g, :] for p in range(n_top)], axis=0)
            vv = jnp.concatenate([kvbuf[g * n_top + p, :, 1, g, :] for p in range(n_top)], axis=0)
            kpos = lane_k & (SEL_BLOCK - 1)
            for p in range(n_top):
                j = idx_ref[(b * G + g) * n_top + p]
                kpos = kpos + jnp.where(lane_k // SEL_BLOCK == p, j * SEL_BLOCK, 0)
            s = _dot(qg, kk, nt=True) * scale
            p_s = _msoftmax(s, kpos <= q_pos)
            o_slc = _dot(p_s, vv)
            kw = cw_ref[:, 0, g, :]
            vw = cw_ref[:, 1, g, :]
            s_w = _dot(qg, kw, nt=True) * scale
            wpos = (q_pos - w0) + lax.broadcasted_iota(I32, (1, w0), 1)
            wmask = (wpos >= 0) & (wpos <= q_pos) & (wpos > q_pos - WINDOW)
            s_w = jnp.where(wmask, s_w, NEG)
            s_n = jnp.sum(_bf16r(qg) * _bf16r(nw_ref[0, g:g + 1, :]), axis=-1, keepdims=True) * scale
            m = jnp.maximum(jnp.max(s_w, axis=-1, keepdims=True), s_n)
            p_w = jnp.where(wmask, jnp.exp(s_w - m), 0.0)
            p_n = jnp.exp(s_n - m)
            den = jnp.sum(p_w, axis=-1, keepdims=True) + p_n
            o_win = _dot(p_w / den, vw) + _bf16r(p_n / den) * _bf16r(nw_ref[1, g:g + 1, :])
            gt = _sigmoid(g_ref[g])
            gc = [jnp.zeros((SUBLANE, 1), F32)] * 3
            for r in range(R):
                for c in range(3):
                    gc[c] = jnp.where(row8 == r, gt[:, 3 * r + c:3 * r + c + 1], gc[c])
            o_ref[g] = gc[0] * oc_ref[g] + gc[1] * o_slc + gc[2] * o_win

    qspec = pl.BlockSpec((None, G, SUBLANE, d), lambda b, i, p: (b, 0, 0, 0))
    gs = pltpu.PrefetchScalarGridSpec(
        num_scalar_prefetch=2, grid=(B,),
        in_specs=[qspec, qspec,
                  pl.BlockSpec((None, G, 1, LANE), lambda b, i, p: (b, 0, 0, 0)),
                  pl.BlockSpec(memory_space=pl.ANY),
                  pl.BlockSpec((None, N_KV_STREAMS, G, d), lambda b, i, p: (b, 0, 0, 0)),
                  pl.BlockSpec((None, None, w0, 2, G, d), lambda b, i, p: (e, b, 0, 0, 0, 0)),
                  pl.BlockSpec((None, 2, G, d), lambda b, i, p: (b, 0, 0, 0))],
        out_specs=qspec,
        scratch_shapes=[pltpu.VMEM((G * n_top, SEL_BLOCK, 2, G, d), F32),
                        pltpu.SemaphoreType.DMA((1,))],
    )
    return pl.pallas_call(
        kern, grid_spec=gs, out_shape=jax.ShapeDtypeStruct((B, G, SUBLANE, d), F32),
        compiler_params=_cp(("arbitrary",)), name="nsa_sample_attend",
    )(idx, page_ids, q8, o_cmp, gates, pages, new_kv, cwin, new_win)


def rglru(u, gb, conv0, h0, conv_w, conv_b, lru_wa, lru_ba, lru_wx, lru_bx, lru_lambda, e, B, T):
    W = u.shape[1]
    cw = RNN_BLOCK_W
    nblk = W // cw
    tc = min(T, 256)
    nt = T // tc
    assert T % tc == 0 and tc % SUBLANE == 0

    def kern(u_ref, gb_ref, c0_ref, h0_ref, cw_ref, cb_ref, wa_ref, ba_ref, wx_ref, bx_ref, lam_ref,
             y_ref, h_ref, prev, hprev):
        ti = pl.program_id(2)

        @pl.when(ti == 0)
        def _():
            prev[...] = c0_ref[...]
            hprev[...] = h0_ref[...]

        uu = u_ref[...]
        ext = _bf16r(jnp.concatenate([prev[...], uu], axis=0))
        prev[...] = uu[tc - SUBLANE:tc]
        cwr = _bf16r(cw_ref[...])
        xc = cwr[0:1, :] * ext[5:5 + tc]
        for j in range(1, CONV_WIDTH):
            xc = xc + cwr[j:j + 1, :] * ext[5 + j:5 + j + tc]
        xc = xc + cb_ref[...]
        r = _sigmoid(_dot(xc, wa_ref[...]) + ba_ref[...])
        i = _sigmoid(_dot(xc, wx_ref[...]) + bx_ref[...])
        nl = -lam_ref[...]
        softplus = jnp.maximum(nl, 0.0) + jnp.log1p(jnp.exp(-jnp.abs(nl)))
        log_a = -LRU_C * r * softplus
        a = jnp.exp(log_a)
        bt = jnp.sqrt(-jnp.tanh(log_a) * (a * a + 1.0)) * (i * xc)
        rows = lax.broadcasted_iota(I32, (tc, 1), 0)
        step = 1
        while step < tc:
            keep = rows >= step
            bt = jnp.where(keep, a * pltpu.roll(bt, step, 0) + bt, bt)
            a = jnp.where(keep, a * pltpu.roll(a, step, 0), a)
            step *= 2
        h = a * hprev[...] + bt
        hprev[...] = h[tc - 1:tc]
        h_ref[...] = h
        y_ref[...] = (h * _gelu(gb_ref[...])).astype(BF16)

    tile = pl.BlockSpec((tc, cw), lambda b, c, t: (b * nt + t, c))
    vec = lambda: pl.BlockSpec((None, 1, cw), lambda b, c, t: (e, 0, c))
    mat = lambda: pl.BlockSpec((None, None, cw, cw), lambda b, c, t: (e, c, 0, 0))
    bvec = lambda: pl.BlockSpec((None, None, 1, cw), lambda b, c, t: (e, c, 0, 0))
    return pl.pallas_call(
        kern, grid=(B, nblk, nt),
        in_specs=[tile, tile,
                  pl.BlockSpec((None, SUBLANE, cw), lambda b, c, t: (b, 0, c)),
                  pl.BlockSpec((None, 1, cw), lambda b, c, t: (b, 0, c)),
                  pl.BlockSpec((None, CONV_WIDTH, cw), lambda b, c, t: (e, 0, c)),
                  vec(), mat(), bvec(), mat(), bvec(), vec()],
        out_specs=[tile, tile],
        out_shape=[jax.ShapeDtypeStruct((B * T, W), BF16), jax.ShapeDtypeStruct((B * T, W), F32)],
        scratch_shapes=[pltpu.VMEM((SUBLANE, cw), F32), pltpu.VMEM((1, cw), F32)],
        compiler_params=_cp(("parallel", "parallel", "arbitrary")), name="rglru",
    )(u, gb, conv0, h0, conv_w, conv_b[:, None, :], lru_wa, lru_ba[:, :, None, :], lru_wx,
      lru_bx[:, :, None, :], lru_lambda[:, None, :])


def pool_mixer(x, buf16, pool_w, pool_scale, o, B, T, t0):
    D = x.shape[1]
    ng = len(POOL_WINDOWS)
    gw = D // ng
    tt = min(T, 256)
    nt = T // tt
    halo = 2 * SUBLANE
    assert T % tt == 0 and tt >= halo and POOL_WINDOWS == (2, 4, 8, 16)

    def kern(x_ref, buf_ref, w_ref, sc_ref, o_ref, prev):
        g, ti = pl.program_id(0), pl.program_id(2)

        @pl.when(ti == 0)
        def _():
            prev[...] = buf_ref[...]

        xx = x_ref[...]
        ext = jnp.concatenate([prev[...], xx], axis=0)
        prev[...] = xx[tt - halo:tt]
        s2 = ext + pltpu.roll(ext, 1, 0)
        s4 = s2 + pltpu.roll(s2, 2, 0)
        s8 = s4 + pltpu.roll(s4, 4, 0)
        s16 = s8 + pltpu.roll(s8, 8, 0)
        sw = jnp.where(g == 0, s2, jnp.where(g == 1, s4, jnp.where(g == 2, s8, s16)))[halo:]
        wlen = jnp.left_shift(2, g)
        pos = t0 + ti * tt + lax.broadcasted_iota(I32, (tt, 1), 0)
        cnt = jnp.minimum(wlen, pos + 1).astype(F32)
        pooled = sw / cnt - xx
        o_ref[...] = _dot(pooled, w_ref[...]) * sc_ref[...]

    return pl.pallas_call(
        kern, grid=(ng, B, nt),
        in_specs=[pl.BlockSpec((tt, gw), lambda g, b, t: (b * nt + t, g)),
                  pl.BlockSpec((None, halo, gw), lambda g, b, t: (b, 0, g)),
                  pl.BlockSpec((None, None, gw, gw), lambda g, b, t: (o, g, 0, 0)),
                  pl.BlockSpec((None, 1, gw), lambda g, b, t: (o, 0, g))],
        out_specs=pl.BlockSpec((tt, gw), lambda g, b, t: (b * nt + t, g)),
        out_shape=jax.ShapeDtypeStruct((B * T, D), F32),
        scratch_shapes=[pltpu.VMEM((halo, gw), F32)],
        compiler_params=_cp(("parallel", "parallel", "arbitrary")), name="pool_mixer",
    )(x, buf16, pool_w, pool_scale[:, None, :])


def mem_attn(q, mkv, B, T, kv_off=0):
    MW = q.shape[1]
    M = mkv.shape[1]
    dh = MW // MEM_HEADS
    tq = min(T, 256)
    nq = T // tq
    scale = dh ** -0.5

    def kern(q_ref, kv_ref, o_ref):
        for h in range(MEM_HEADS):
            sl = slice(h * dh, (h + 1) * dh)
            s = _dot(q_ref[:, sl], kv_ref[:, 0, sl], nt=True) * scale
            m = jnp.max(s, axis=-1, keepdims=True)
            p = jnp.exp(s - m)
            p = p / jnp.sum(p, axis=-1, keepdims=True)
            o_ref[:, sl] = _dot(p, kv_ref[:, 1, sl]).astype(BF16)

    return pl.pallas_call(
        kern, grid=(B, nq),
        in_specs=[pl.BlockSpec((tq, MW), lambda b, i: (b * nq + i, 0)),
                  pl.BlockSpec((None, M, 2, MW), lambda b, i: (kv_off + b, 0, 0, 0))],
        out_specs=pl.BlockSpec((tq, MW), lambda b, i: (b * nq + i, 0)),
        out_shape=jax.ShapeDtypeStruct((B * T, MW), BF16),
        compiler_params=_cp(("parallel", "parallel")), name="mem_attn",
    )(q, mkv)


ROUTER_ROWS = 40


def moe_router(x, w_t, bias, tm=512):
    N, D = x.shape
    tm = min(tm, N)
    assert N % tm == 0 and tm % LANE == 0

    def kern(x_ref, w_ref, b_ref, e_ref, g_ref):
        lt = _dot(w_ref[...], x_ref[...], nt=True) + b_ref[...]
        c = [lt[i:i + 1] for i in range(MOE_GROUPS)]
        m = functools.reduce(jnp.maximum, c)
        ex = [jnp.exp(ci - m) for ci in c]
        tot = functools.reduce(jnp.add, ex)
        pc = [ei / tot for ei in ex]
        pg = functools.reduce(jnp.maximum, pc)
        grp = jnp.full(pg.shape, MOE_GROUPS - 1, I32)
        for i in range(MOE_GROUPS - 2, -1, -1):
            grp = jnp.where(pc[i] == pg, i, grp)
        lf = []
        for j in range(MOE_EPG):
            v = lt[MOE_GROUPS + j:MOE_GROUPS + j + 1]
            for gi in range(1, MOE_GROUPS):
                r0 = MOE_GROUPS + gi * MOE_EPG + j
                v = jnp.where(grp == gi, lt[r0:r0 + 1], v)
            lf.append(v)
        m = functools.reduce(jnp.maximum, lf)
        ex = [jnp.exp(v - m) for v in lf]
        tot = functools.reduce(jnp.add, ex)
        pf = [ei / tot for ei in ex]

        def first_max(vals):
            best = functools.reduce(jnp.maximum, vals)
            arg = jnp.full(best.shape, len(vals) - 1, I32)
            for j in range(len(vals) - 2, -1, -1):
                arg = jnp.where(vals[j] == best, j, arg)
            return best, arg

        p1, j1 = first_max(pf)
        p2, j2 = first_max([jnp.where(j1 == j, -1.0, pf[j]) for j in range(MOE_EPG)])
        den = p1 + p2
        e_ref[0:1, :] = grp * MOE_EPG + j1
        e_ref[1:2, :] = grp * MOE_EPG + j2
        g_ref[0:1, :] = pg * p1 / den
        g_ref[1:2, :] = pg * p2 / den

    out = pl.BlockSpec((MOE_TOP_K, tm), lambda i: (0, i))
    return pl.pallas_call(
        kern, grid=(N // tm,),
        in_specs=[pl.BlockSpec((tm, D), lambda i: (i, 0)),
                  pl.BlockSpec((ROUTER_ROWS, D), lambda i: (0, 0)),
                  pl.BlockSpec((ROUTER_ROWS, 1), lambda i: (0, 0))],
        out_specs=[out, out],
        out_shape=[jax.ShapeDtypeStruct((MOE_TOP_K, N), I32), jax.ShapeDtypeStruct((MOE_TOP_K, N), F32)],
        compiler_params=_cp(("parallel",)), name="moe_router",
    )(x, w_t, bias)


def moe_experts(x3, tok, aid, nvalid, bexp, nused, w_gate, w_up, w_down, layer, rows, n_assign, tf=256):
    _, nc, _ = x3.shape
    D = nc * LANE
    n_blocks = tok.shape[0] // rows
    FF = w_gate.shape[3]
    nf = FF // tf

    assert nf >= 4

    def kern(tok_ref, aid_ref, nv_ref, bexp_ref, nused_ref, x3_ref, wg_ref, wu_ref, wd_ref, y_ref,
             xbuf, xb, acc, ybuf, sem_in, sem_out):
        blk, f = pl.program_id(0), pl.program_id(1)
        nused = nused_ref[0]
        active = blk < nused

        def gather_start(b):
            def body(r, carry):
                pltpu.make_async_copy(x3_ref.at[tok_ref[b * rows + r]], xbuf.at[r], sem_in.at[0]).start()
                return carry
            lax.fori_loop(0, rows, body, 0, unroll=SUBLANE)

        def gather_wait():
            pltpu.make_async_copy(x3_ref.at[pl.ds(0, rows)], xbuf, sem_in.at[0]).wait()

        def scatter_start(b):
            slot = b % 2

            def body(r, carry):
                pltpu.make_async_copy(ybuf.at[slot, r], y_ref.at[aid_ref[b * rows + r]], sem_out.at[slot]).start()
                return carry
            lax.fori_loop(0, nv_ref[b], body, 0)

        def scatter_wait(b):
            nv = nv_ref[b]

            @pl.when(nv > 0)
            def _():
                pltpu.make_async_copy(ybuf.at[b % 2, pl.ds(0, nv)], y_ref.at[pl.ds(0, nv)], sem_out.at[b % 2]).wait()

        @pl.when(active & (f == 0))
        def _():
            pl.when(blk == 0)(lambda: gather_start(0))
            gather_wait()
            xb[...] = xbuf[...].reshape(rows, D).astype(BF16)
            acc[...] = jnp.zeros_like(acc)

        pl.when(active & (f == 1) & (blk >= 1))(lambda: scatter_start(blk - 1))
        pl.when((f == 2) & (blk + 1 < nused))(lambda: gather_start(blk + 1))

        @pl.when(active)
        def _():
            x = xb[...]
            hg = _mxu(x, wg_ref[...].astype(BF16), False)
            hu = _mxu(x, wu_ref[...].astype(BF16), False)
            hid = hg * _sigmoid(hg) * hu
            acc[...] += _dot(hid, wd_ref[...])

        @pl.when(active & (f == nf - 1))
        def _():
            pl.when(blk >= 2)(lambda: scatter_wait(blk - 2))
            ybuf[blk % 2] = _token_major(acc[...])

            @pl.when(blk == nused - 1)
            def _():
                pl.when(blk >= 1)(lambda: scatter_wait(blk - 1))
                scatter_start(blk)
                scatter_wait(blk)

    def live(b, nu):
        return jnp.maximum(jnp.minimum(b, nu[0] - 1), 0)

    def feff(b, f, nu):
        return jnp.where(b < nu[0], f, nf - 1)

    gs = pltpu.PrefetchScalarGridSpec(
        num_scalar_prefetch=5, grid=(n_blocks, nf),
        in_specs=[pl.BlockSpec(memory_space=pl.ANY),
                  pl.BlockSpec((None, None, D, tf), lambda b, f, t, a, nv, be, nu: (layer, be[live(b, nu)], 0, feff(b, f, nu))),
                  pl.BlockSpec((None, None, D, tf), lambda b, f, t, a, nv, be, nu: (layer, be[live(b, nu)], 0, feff(b, f, nu))),
                  pl.BlockSpec((None, None, tf, D), lambda b, f, t, a, nv, be, nu: (layer, be[live(b, nu)], feff(b, f, nu), 0))],
        out_specs=pl.BlockSpec(memory_space=pl.ANY),
        scratch_shapes=[pltpu.VMEM((rows, nc, LANE), F32), pltpu.VMEM((rows, D), BF16),
                        pltpu.VMEM((rows, D), F32), pltpu.VMEM((2, rows, nc, LANE), F32),
                        pltpu.SemaphoreType.DMA((1,)), pltpu.SemaphoreType.DMA((2,))],
    )
    return pl.pallas_call(
        kern, grid_spec=gs, out_shape=jax.ShapeDtypeStruct((n_assign, nc, LANE), F32),
        compiler_params=_cp(("arbitrary", "arbitrary")), name="moe_experts",
    )(tok, aid, nvalid, bexp, nused, x3, w_gate, w_up, w_down)


def moe_combine_ln(x, ytok, gate, ln_g, ln_b, idx, alpha, n_tok, tm=128):
    D = x.shape[1]
    nc = D // LANE
    tm = min(tm, n_tok)

    def kern(x_ref, y_ref, gt_ref, g_ref, b_ref, o_ref, ob_ref):
        gt = _bf16r(gt_ref[...])
        mix = sum(gt[:, k:k + 1] * _bf16r(y_ref[:, k].reshape(tm, D)) for k in range(MOE_TOP_K))
        y = _layer_norm(alpha * x_ref[...] + mix, g_ref[...], b_ref[...])
        o_ref[...] = y
        ob_ref[...] = y.astype(BF16)

    row = pl.BlockSpec((tm, D), lambda i: (i, 0))
    par = pl.BlockSpec((None, 1, D), lambda i: (idx, 0, 0))
    return pl.pallas_call(
        kern, grid=(n_tok // tm,),
        in_specs=[row, pl.BlockSpec((tm, MOE_TOP_K, nc, LANE), lambda i: (i, 0, 0, 0)),
                  pl.BlockSpec((tm, MOE_TOP_K), lambda i: (i, 0)), par, par],
        out_specs=[row, row],
        out_shape=[jax.ShapeDtypeStruct((n_tok, D), F32), jax.ShapeDtypeStruct((n_tok, D), BF16)],
        compiler_params=_cp(("parallel",)), name="moe_combine_ln",
    )(x, ytok, gate, ln_g, ln_b)


def hier_moe_ln(x, x3, n_tok, w_rt, b_rt, w_gate, w_up, w_down, layer, ln_g, ln_b, alpha, rows):
    N, D = x.shape
    xr = x if N % LANE == 0 else jnp.pad(x, ((0, LANE - N % LANE), (0, 0)))
    eidx, gate = moe_router(xr, w_rt, b_rt)
    eidx, gate = eidx[:, :n_tok], gate[:, :n_tok]
    A = n_tok * MOE_TOP_K
    e = eidx.T.reshape(A)
    onehot = (e[:, None] == jnp.arange(MOE_EXPERTS, dtype=I32)[None, :]).astype(I32)
    csum = jnp.cumsum(onehot, axis=0)
    counts = csum[-1]
    pos = jnp.sum(csum * onehot, axis=1) - 1
    padded = (counts + rows - 1) // rows * rows
    pad_end = jnp.cumsum(padded)
    pad_start = pad_end - padded
    dest = (jnp.sum(onehot * pad_start[None, :], axis=1) + pos).astype(I32)
    n_blocks = min(A // rows + MOE_EXPERTS, A)
    bexp = jnp.minimum(jnp.searchsorted(pad_end, jnp.arange(n_blocks, dtype=I32) * rows, side="right"),
                       MOE_EXPERTS - 1).astype(I32)
    nused = (pad_end[-1:] // rows).astype(I32)
    nvalid = jnp.clip((pad_start + counts)[bexp] - jnp.arange(n_blocks, dtype=I32) * rows, 0, rows).astype(I32)
    aid = jnp.zeros((n_blocks * rows,), I32).at[dest].set(jnp.arange(A, dtype=I32))
    ytok = moe_experts(x3, aid // MOE_TOP_K, aid, nvalid, bexp, nused, w_gate, w_up, w_down, layer, rows, A)
    ytok = ytok.reshape(n_tok, MOE_TOP_K, D // LANE, LANE)
    return moe_combine_ln(x, ytok, gate.T, ln_g, ln_b, layer * 3 + 2, alpha, n_tok)


def _rope_tables(pos):
    half = HEAD_DIM // 2
    inv_freq = ROPE_THETA ** (-jnp.arange(half, dtype=F32) / half)
    ang = pos.astype(F32)[:, None] * inv_freq[None, :]
    cos, sin = jnp.cos(ang), jnp.sin(ang)
    return jnp.concatenate([cos, cos], axis=1), jnp.concatenate([-sin, sin], axis=1)


def _even_projections(x, pos_rows, w_in, w_rest, w_gates, e, tm, q_dtype):
    G, d = NSA_G, HEAD_DIM
    q_cols = G * NSA_R * d
    kv_cols = N_KV_STREAMS * G * d
    cos, sin = _rope_tables(pos_rows)
    tn = G * d
    mm = functools.partial(matmul, x, tm=tm)
    q = mm(w_in, layer=e, col_off=0, n_out=q_cols, tn=tn, rope=(cos, sin, tuple(range(q_cols // tn))),
           out_dtype=q_dtype)
    kv = mm(w_in, layer=e, col_off=q_cols, n_out=kv_cols, tn=tn, rope=(cos, sin, (0, 2)))
    win = mm(w_in, layer=e, col_off=q_cols + kv_cols, n_out=2 * G * d, tn=tn, rope=(cos, sin, (0,)))
    gates = mm(w_gates, tn=G * LANE)
    rw = w_rest.shape[1] // 2
    u = mm(w_rest, n_out=rw)
    gb = mm(w_rest, col_off=rw, n_out=rw)
    return q, kv, win, gates, u, gb


def _pad_rows(a, n):
    return jnp.pad(a, ((0, n - a.shape[0]),) + ((0, 0),) * (a.ndim - 1))


def kernel(x_prompt, x_sample, mem_prompt, cache_nsa_kv, cache_nsa_win, state_rglru_h, state_rglru_conv,
           state_pool, cache_mem_kv, page_table, w_in, w_out, cmp_w1, cmp_w2, cmp_pe, conv_w, conv_b,
           lru_wa, lru_ba, lru_wx, lru_bx, lru_lambda, pool_w, pool_scale, mem_wq, mem_wk, mem_wv, mem_wo,
           ln_g, ln_b, moe_w_coarse, moe_b_coarse, moe_w_fine, moe_b_fine, moe_w_gate, moe_w_up, moe_w_down):
    B, T, D = x_prompt.shape
    DB = x_sample.shape[0]
    depth = ln_g.shape[0]
    n_pages = page_table.shape[1]
    past_len = n_pages * PAGE_SIZE
    n_phys = cache_nsa_kv.shape[1]
    G, R, d = NSA_G, NSA_R, HEAD_DIM
    alpha = (2 * depth) ** 0.25
    q_cols, kv_cols, win_cols, gate_cols = G * R * d, N_KV_STREAMS * G * d, 2 * G * d, 3 * G * R
    rest_off = q_cols + kv_cols + win_cols + gate_cols
    RW = (w_in.shape[2] - rest_off) // 2
    MW = mem_wq.shape[2]
    M = mem_prompt.shape[1]
    SR = SAMPLE_ROWS
    TS = SAMPLE_ROWS

    xp = x_prompt.reshape(B * T, D)
    xpb = xp.astype(BF16)
    xs = _pad_rows(x_sample.reshape(DB, D), SR)
    lng = ln_g.reshape(depth * 3, 1, D)
    lnb = ln_b.reshape(depth * 3, 1, D)
    memf = mem_prompt.reshape(B * M, D)

    kv_p, kv_s, win_p, win_s, h_p, h_s, cv_p, cv_s, pl_p, pl_s, mem_p = ([] for _ in range(11))
    for layer in range(depth):
        if layer % 2 == 0:
            e = layer // 2
            wg = w_in[e, :, q_cols + kv_cols + win_cols:rest_off].reshape(D, G, 3 * R)
            wg = jnp.pad(wg, ((0, 0), (0, 0), (0, LANE - 3 * R))).reshape(D, G * LANE)
            w_rest = w_in[e, :, rest_off:]

            q, kv, win, gates, u, gb = _even_projections(
                xpb, jnp.arange(T), w_in, w_rest, wg, e, 1024, BF16)
            pages_p = kv.reshape(B * T // PAGE_SIZE, PAGE_SIZE, N_KV_STREAMS, G, d)
            new_kv = pages_p.reshape(B, T, N_KV_STREAMS, G, d)
            ab = compress_ab(pages_p, jnp.arange(B * T // PAGE_SIZE, dtype=I32), cmp_w1, cmp_pe, e, B,
                             T // PAGE_SIZE)
            kvcmp = compress_fin(ab, cmp_w2, e)
            o_nsa = nsa_prompt(q, kv, win, kvcmp, gates, B, T)
            y_rnn, h_all = rglru(u, gb, jnp.zeros((B, SUBLANE, RW), F32), jnp.zeros((B, 1, RW), F32),
                                 conv_w, conv_b, lru_wa, lru_ba, lru_wx, lru_bx, lru_lambda, e, B, T)
            mix_p = matmul(o_nsa, w_out, layer=e, x2=y_rnn)
            keep = min(WINDOW, T)
            kv_p.append(new_kv)
            win_p.append(win.reshape(B, T, 2, G, d)[:, T - keep:])
            h_p.append(h_all.reshape(B, T, RW)[:, T - 1])
            cv_p.append(u.reshape(B, T, RW)[:, T - (CONV_WIDTH - 1):])

            q, kv, win, gates, u, gb = _even_projections(
                xs, jnp.full((SR,), past_len), w_in, w_rest, wg, e, SR, F32)
            new_kv_s = kv[:DB].reshape(DB, N_KV_STREAMS, G, d)
            new_win_s = win[:DB].reshape(DB, 2, G, d)
            pages_s = cache_nsa_kv.reshape(cache_nsa_kv.shape[0] * n_phys, PAGE_SIZE, N_KV_STREAMS, G, d)
            pid = (page_table.reshape(DB * n_pages) + e * n_phys).astype(I32)
            ab = compress_ab(pages_s, pid, cmp_w1, cmp_pe, e, DB, n_pages)
            kvcmp = compress_fin(ab, cmp_w2, e)
            q8 = jnp.pad(q[:DB].reshape(DB, G, R, d), ((0, 0), (0, 0), (0, SUBLANE - R), (0, 0)))
            o_cmp, imp = nsa_sample_cmp(q8, kvcmp, past_len)
            idx_t = nsa_sample_topk(_pad_rows(imp.reshape(DB * SUBLANE, -1), LANE), past_len)
            n_top = idx_t.shape[0]
            idx = idx_t[:, :DB * SUBLANE].T.reshape(DB, SUBLANE, n_top)[:, :G].reshape(DB * G * n_top)
            o8 = nsa_sample_attend(q8, o_cmp, gates[:DB].reshape(DB, G, 1, LANE), idx, pid, pages_s,
                                   new_kv_s, cache_nsa_win, new_win_s, e, past_len)
            o_nsa = _pad_rows(o8[:, :, :R].reshape(DB, G * R * d), SR)
            conv0 = state_rglru_conv[e]
            u_t = jnp.pad(u[:DB, None, :], ((0, 0), (0, TS - 1), (0, 0))).reshape(DB * TS, RW)
            gb_t = jnp.pad(gb[:DB, None, :], ((0, 0), (0, TS - 1), (0, 0))).reshape(DB * TS, RW)
            c0 = jnp.pad(conv0, ((0, 0), (SUBLANE - (CONV_WIDTH - 1), 0), (0, 0)))
            y_t, h_t = rglru(u_t, gb_t, c0, state_rglru_h[e][:, None, :], conv_w, conv_b, lru_wa, lru_ba,
                             lru_wx, lru_bx, lru_lambda, e, DB, TS)
            y_rnn = _pad_rows(y_t.reshape(DB, TS, RW)[:, 0], SR)
            mix_s = matmul(o_nsa, w_out, layer=e, x2=y_rnn)
            keep = min(WINDOW, past_len + 1)
            win_all = jnp.concatenate([cache_nsa_win[e], new_win_s[:, None]], axis=1)
            kv_s.append(new_kv_s[:, None])
            win_s.append(win_all[:, win_all.shape[1] - keep:])
            h_s.append(h_t.reshape(DB, TS, RW)[:, 0])
            cv_s.append(jnp.concatenate([conv0, u[:DB, None, :]], axis=1)[:, 1:])
        else:
            o = layer // 2
            mix_p = pool_mixer(xp, jnp.zeros((B, 2 * SUBLANE, D), F32), pool_w, pool_scale, o, B, T, 0)
            pl_p.append(xp.reshape(B, T, D)[:, T - POOL_BUF:])
            tsp = 2 * SUBLANE
            x_t = jnp.pad(xs[:DB, None, :], ((0, 0), (0, tsp - 1), (0, 0))).reshape(DB * tsp, D)
            buf = jnp.pad(state_pool[o], ((0, 0), (tsp - POOL_BUF, 0), (0, 0)))
            mix_t = pool_mixer(x_t, buf, pool_w, pool_scale, o, DB, tsp, past_len)
            mix_s = _pad_rows(mix_t.reshape(DB, tsp, D)[:, 0], SR)
            pl_s.append(jnp.concatenate([state_pool[o], xs[:DB, None, :]], axis=1)[:, 1:])

        xp, xpb = add_ln(xp, mix_p, lng, lnb, layer * 3, alpha)
        xs, _ = add_ln(xs, mix_s, lng, lnb, layer * 3, alpha)

        w_kv = jnp.concatenate([mem_wk[layer], mem_wv[layer]], axis=1)
        mkv = matmul(memf, w_kv).reshape(B, M, 2, MW)
        mem_p.append(mkv)
        qm = matmul(xpb, mem_wq, layer=layer, out_dtype=BF16)
        att = mem_attn(qm, mkv, B, T)
        xp, xpb, xp3 = add_ln(xp, matmul(att, mem_wo, layer=layer), lng, lnb, layer * 3 + 1, alpha,
                              token_major=True)
        qm = matmul(xs, mem_wq, layer=layer)
        qm_t = jnp.pad(qm[:DB, None, :], ((0, 0), (0, TS - 1), (0, 0))).reshape(DB * TS, MW)
        att_t = mem_attn(qm_t, cache_mem_kv.reshape((-1,) + cache_mem_kv.shape[2:]), DB, TS,
                         kv_off=layer * DB)
        att = _pad_rows(att_t.reshape(DB, TS, MW)[:, 0], SR)
        xs, _, xs3 = add_ln(xs, matmul(att, mem_wo, layer=layer), lng, lnb, layer * 3 + 1, alpha,
                            token_major=True)

        w_rt = _pad_rows(jnp.concatenate([moe_w_coarse[layer], moe_w_fine[layer]], axis=1).T, ROUTER_ROWS)
        b_rt = _pad_rows(jnp.concatenate([moe_b_coarse[layer], moe_b_fine[layer]])[:, None], ROUTER_ROWS)
        xp, xpb = hier_moe_ln(xp, xp3, B * T, w_rt, b_rt, moe_w_gate, moe_w_up, moe_w_down, layer, lng, lnb,
                              alpha, rows=256)
        xs = _pad_rows(hier_moe_ln(xs, xs3, DB, w_rt, b_rt, moe_w_gate, moe_w_up, moe_w_down, layer, lng, lnb,
                                   alpha, rows=SUBLANE)[0], SR)

    return (xp.reshape(B, T, D), xs[:DB].reshape(DB, 1, D), jnp.stack(kv_p), jnp.stack(kv_s),
            jnp.stack(win_p), jnp.stack(win_s), jnp.stack(h_p), jnp.stack(h_s), jnp.stack(cv_p),
            jnp.stack(cv_s), jnp.stack(pl_p), jnp.stack(pl_s), jnp.stack(mem_p))
```

```python
import functools
import math

import jax
import jax.numpy as jnp
from jax import lax
from jax.experimental import pallas as pl
from jax.experimental.pallas import tpu as pltpu

F32 = jnp.float32
BF16 = jnp.bfloat16
I32 = jnp.int32

HEAD_DIM = 128
NSA_G = 4
NSA_R = 4
N_KV_STREAMS = 4
CMP_BLOCK = 32
CMP_STRIDE = 16
SEL_BLOCK = 64
SEL_TOP_N = 16
WINDOW = 512
FORCE_SCORE = 1e9
NEG = -1e30
ROPE_THETA = 10000.0
CONV_WIDTH = 4
LRU_C = 8.0
RNN_BLOCK_W = 128
POOL_WINDOWS = (2, 4, 8, 16)
POOL_BUF = 15
MEM_HEADS = 4
MOE_GROUPS = 4
MOE_EPG = 8
MOE_EXPERTS = MOE_GROUPS * MOE_EPG
MOE_TOP_K = 2
LN_EPS = 1e-5
PAGE_SIZE = 128

LANE = 128
SUBLANE = 8
VMEM_LIMIT = 56 * 1024 * 1024
SAMPLE_ROWS = 16


def _cp(sem, vmem=VMEM_LIMIT):
    return pltpu.CompilerParams(dimension_semantics=sem, vmem_limit_bytes=vmem)


def _split3(x):
    h = x.astype(BF16)
    r = x - h.astype(F32)
    m = r.astype(BF16)
    l = (r - m.astype(F32)).astype(BF16)
    return h, m, l


def _mxu(a, b, nt):
    if nt:
        return lax.dot_general(a, b, (((1,), (1,)), ((), ())), preferred_element_type=F32)
    return jnp.dot(a, b, preferred_element_type=F32)


def _dot(a, b, nt=False):
    return _mxu(a.astype(BF16), b.astype(BF16), nt)


def _bf16r(x):
    return x.astype(BF16).astype(F32)


def _sigmoid(x):
    return 1.0 / (1.0 + jnp.exp(-x))


def _gelu(x):
    return 0.5 * x * (1.0 + jnp.tanh(0.7978845608028654 * (x + 0.044715 * (x * x * x))))


def _msoftmax(s, mask):
    s = jnp.where(mask, s, NEG)
    m = jnp.max(s, axis=-1, keepdims=True)
    p = jnp.where(mask, jnp.exp(s - m), 0.0)
    return p / jnp.maximum(jnp.sum(p, axis=-1, keepdims=True), 1e-30)


def _msoftmax_nonempty(s, mask):
    s = jnp.where(mask, s, NEG)
    p = jnp.exp(s - jnp.max(s, axis=-1, keepdims=True))
    return p * (1.0 / jnp.sum(p, axis=-1, keepdims=True))


def _layer_norm(v, g, b):
    mu = jnp.mean(v, axis=-1, keepdims=True)
    c = v - mu
    var = jnp.mean(c * c, axis=-1, keepdims=True)
    return c * lax.rsqrt(var + LN_EPS) * g + b


def matmul(x, w, *, layer=0, col_off=0, n_out=None, x2=None, tm=1024, tn=512, rope=None, out_dtype=F32):
    if w.ndim == 2:
        w = w[None]
    M, K1 = x.shape
    K2 = 0 if x2 is None else x2.shape[1]
    assert w.shape[1] == K1 + K2
    n_out = w.shape[2] - col_off if n_out is None else n_out
    xs_in = [x] if x2 is None else [x, x2]
    if any(a.dtype != BF16 for a in xs_in):
        tm = min(tm, 512)
    tm, tn = min(tm, M), min(tn, n_out)
    assert M % tm == 0 and n_out % tn == 0 and col_off % tn == 0
    joff = col_off // tn

    in_specs = [pl.BlockSpec((tm, a.shape[1]), lambda i, j: (i, 0)) for a in xs_in]
    in_specs.append(pl.BlockSpec((None, K1 + K2, tn), lambda i, j: (layer, 0, j + joff)))
    args = xs_in + [w]
    rope_blocks = ()
    if rope is not None:
        cos, sin, rope_blocks = rope
        period = cos.shape[0] // tm
        assert cos.shape[0] % tm == 0
        for t in (cos, sin):
            in_specs.append(pl.BlockSpec((tm, LANE), lambda i, j: (i % period, 0)))
            args.append(t)
    cast = [a.dtype != BF16 for a in xs_in]
    scratch = [pltpu.VMEM((tm, a.shape[1]), BF16) for a, c in zip(xs_in, cast) if c]

    def kern(*refs):
        refs = list(refs)
        x_refs = [refs.pop(0) for _ in xs_in]
        w_ref = refs.pop(0)
        cos_ref, sin_ref = (refs.pop(0), refs.pop(0)) if rope is not None else (None, None)
        o_ref = refs.pop(0)
        j = pl.program_id(1)
        xb_refs = [refs.pop(0) if c else xr for xr, c in zip(x_refs, cast)]

        if any(cast):
            @pl.when(j == 0)
            def _():
                for xr, xb, c in zip(x_refs, xb_refs, cast):
                    if c:
                        xb[...] = xr[...].astype(BF16)

        acc = _mxu(xb_refs[0][...], w_ref[0:K1, :].astype(BF16), False)
        if x2 is not None:
            acc = acc + _mxu(xb_refs[1][...], w_ref[K1:K1 + K2, :].astype(BF16), False)
        if rope is None:
            o_ref[...] = acc.astype(out_dtype)
            return
        rot = functools.reduce(jnp.logical_or, [j == c for c in rope_blocks])

        @pl.when(rot)
        def _():
            c, s = cos_ref[...], sin_ref[...]
            for h in range(tn // LANE):
                seg = acc[:, h * LANE:(h + 1) * LANE]
                o_ref[:, h * LANE:(h + 1) * LANE] = (seg * c + pltpu.roll(seg, LANE // 2, 1) * s).astype(out_dtype)

        @pl.when(jnp.logical_not(rot))
        def _():
            o_ref[...] = acc.astype(out_dtype)

    return pl.pallas_call(
        kern,
        grid=(M // tm, n_out // tn),
        in_specs=in_specs,
        out_specs=pl.BlockSpec((tm, tn), lambda i, j: (i, j)),
        out_shape=jax.ShapeDtypeStruct((M, n_out), out_dtype),
        scratch_shapes=scratch,
        compiler_params=_cp(("parallel", "arbitrary")),
        name="matmul",
    )(*args)


def _token_major(v):
    return v.reshape(v.shape[0], v.shape[1] // LANE, LANE)


def add_ln(x, f, ln_g, ln_b, idx, alpha, tm=256, token_major=False):
    M, D = x.shape
    tm = min(tm, M)

    def kern(x_ref, f_ref, g_ref, b_ref, o_ref, ob_ref, *o3_ref):
        y = _layer_norm(alpha * x_ref[...] + f_ref[...], g_ref[...], b_ref[...])
        o_ref[...] = y
        ob_ref[...] = y.astype(BF16)
        if token_major:
            o3_ref[0][...] = _token_major(y)

    row = pl.BlockSpec((tm, D), lambda i: (i, 0))
    par = pl.BlockSpec((None, 1, D), lambda i: (idx, 0, 0))
    out_specs = [row, row]
    out_shape = [jax.ShapeDtypeStruct((M, D), F32), jax.ShapeDtypeStruct((M, D), BF16)]
    if token_major:
        out_specs.append(pl.BlockSpec((tm, D // LANE, LANE), lambda i: (i, 0, 0)))
        out_shape.append(jax.ShapeDtypeStruct((M, D // LANE, LANE), F32))
    return pl.pallas_call(
        kern, grid=(M // tm,), in_specs=[row, row, par, par], out_specs=out_specs,
        out_shape=out_shape, compiler_params=_cp(("parallel",)), name="add_ln",
    )(x, f, ln_g, ln_b)


CMP_PAGES = 8


def compress_ab(pages, page_ids, cmp_w1, cmp_pe, e, n_batch, pages_per_seq):
    P = CMP_PAGES
    assert pages_per_seq % P == 0
    cpp = PAGE_SIZE // CMP_STRIDE
    steps = pages_per_seq // P
    d, G = HEAD_DIM, NSA_G
    n_chunks = pages_per_seq * cpp
    rows = P * cpp * G

    def kern(pid_ref, *refs):
        page_refs, w_ref, pe_ref, o_ref = refs[:P], refs[P], refs[P + 1], refs[P + 2]
        for s in range(2):
            acc_a = jnp.zeros((rows, d), F32)
            acc_b = jnp.zeros((rows, d), F32)

            def chunk_rows(l):
                return jnp.concatenate(
                    [page_refs[i][pl.ds(l, cpp, stride=CMP_STRIDE), s].reshape(cpp * G, d) for i in range(P)],
                    axis=0)

            def pair(l, off):
                lhs = jnp.concatenate([rows2[0] + pe_ref[s, off + l:off + l + 1, :],
                                       rows2[1] + pe_ref[s, off + l + 1:off + l + 2, :]], axis=1)
                return _dot(lhs, w_ref[s, off + l:off + l + 2].reshape(2 * d, d))

            for l in range(0, CMP_STRIDE, 2):
                rows2 = (chunk_rows(l), chunk_rows(l + 1))
                acc_a = acc_a + pair(l, 0)
                acc_b = acc_b + pair(l, CMP_STRIDE)
            o_ref[:, s * 2 * d:s * 2 * d + d] = acc_a
            o_ref[:, s * 2 * d + d:(s + 1) * 2 * d] = acc_b

    def page_spec(i):
        return pl.BlockSpec((None, PAGE_SIZE, 2, G, d),
                            lambda b, st, pid: (pid[(b * steps + st) * P + i], 0, 0, 0, 0))

    gs = pltpu.PrefetchScalarGridSpec(
        num_scalar_prefetch=1,
        grid=(n_batch, steps),
        in_specs=[page_spec(i) for i in range(P)]
        + [pl.BlockSpec((None, 2, CMP_BLOCK, d, d), lambda b, st, pid: (e, 0, 0, 0, 0)),
           pl.BlockSpec((None, 2, CMP_BLOCK, d), lambda b, st, pid: (e, 0, 0, 0))],
        out_specs=pl.BlockSpec((None, rows, 2 * 2 * d), lambda b, st, pid: (b, st, 0)),
    )
    return pl.pallas_call(
        kern, grid_spec=gs,
        out_shape=jax.ShapeDtypeStruct((n_batch, n_chunks * G, 2 * 2 * d), F32),
        compiler_params=_cp(("parallel", "parallel")), name="compress_ab",
    )(page_ids, *([pages] * P), cmp_w1, cmp_pe)


def compress_fin(ab, cmp_w2, e):
    nb, nrow, _ = ab.shape
    d, G = HEAD_DIM, NSA_G
    nch = nrow // G

    def kern(ab_ref, w2_ref, o_ref):
        h = ab_ref[:, 0:d] + pltpu.roll(ab_ref[:, d:2 * d], nrow - G, 0)
        res = _dot(_gelu(h), w2_ref[...]).reshape(nch, G, d)
        for g in range(G):
            o_ref[g] = res[:, g, :]

    return pl.pallas_call(
        kern, grid=(nb, 2),
        in_specs=[pl.BlockSpec((None, nrow, 2 * d), lambda b, s: (b, 0, s)),
                  pl.BlockSpec((None, None, d, d), lambda b, s: (e, s, 0, 0))],
        out_specs=pl.BlockSpec((None, None, NSA_G, nch, d), lambda b, s: (s, b, 0, 0, 0)),
        out_shape=jax.ShapeDtypeStruct((2, nb, NSA_G, nch, d), F32),
        compiler_params=_cp(("parallel", "parallel")), name="compress_fin",
    )(ab, cmp_w2)


def _overlap_matrix(n_rows, n_cols):
    ci = jnp.arange(n_rows)[:, None] * CMP_STRIDE
    sj = jnp.arange(n_cols)[None, :] * SEL_BLOCK
    return ((ci < sj + SEL_BLOCK) & (ci + CMP_BLOCK > sj)).astype(BF16)


NSA_TQ = 128


def nsa_prompt(q, kv, win, kvcmp, gates, B, T):
    d, G, R, tq = HEAD_DIM, NSA_G, NSA_R, NSA_TQ
    nq = T // tq
    ncp = kvcmp.shape[3]
    n_cmp = T // CMP_STRIDE - 1
    n_sel = -(-T // SEL_BLOCK)
    n_top = min(SEL_TOP_N, n_sel)
    wk = min(T, WINDOW + tq)
    key_span = min(T, 4 * tq)
    assert n_sel <= LANE and T % tq == 0 and T % key_span == 0
    scale = d ** -0.5
    overlap = _overlap_matrix(ncp, LANE)
    expand = (jnp.arange(LANE)[:, None] == (jnp.arange(T) // SEL_BLOCK)[None, :]).astype(BF16)

    def kern(q_ref, ks_ref, vs_ref, kw_ref, vw_ref, kc_ref, vc_ref, g_ref, ov_ref, ex_ref, o_ref,
             ksb, vsb, kwb, vwb, kcb, vcb):
        qi = pl.program_id(2)

        @pl.when(qi == 0)
        def _():
            ksb[...] = ks_ref[...].astype(BF16)
            vsb[...] = vs_ref[...].astype(BF16)
            kwb[...] = kw_ref[...].astype(BF16)
            vwb[...] = vw_ref[...].astype(BF16)
            kcb[...] = kc_ref[...].astype(BF16)
            vcb[...] = vc_ref[...].astype(BF16)

        t0 = qi * tq
        qb = q_ref[...]
        qs = jnp.concatenate([qb[:, r * d:(r + 1) * d] for r in range(R)], axis=0).astype(BF16)
        qpos_s = t0 + (lax.broadcasted_iota(I32, (R * tq, 1), 0) & (tq - 1))

        s = _mxu(qs, kcb[...], True) * scale
        n_idx = lax.broadcasted_iota(I32, (1, ncp), 1)
        cmask = (n_idx * CMP_STRIDE + (CMP_BLOCK - 1) <= qpos_s) & (n_idx < n_cmp)
        p = _msoftmax(s, cmask)
        pb = p.astype(BF16)
        o_cmp = _mxu(pb, vcb[...], False)
        imp_r = _mxu(pb, ov_ref[...], False)
        imp = imp_r[0:tq]
        for r in range(1, R):
            imp = imp + imp_r[r * tq:(r + 1) * tq]

        blk = lax.broadcasted_iota(I32, (tq, LANE), 1)
        qpos = t0 + lax.broadcasted_iota(I32, (tq, LANE), 0)
        cur = qpos // SEL_BLOCK
        valid = blk <= cur
        forced = (blk == 0) | (blk == cur) | (blk == cur - 1)
        score = jnp.where(forced, FORCE_SCORE, jnp.where(valid, imp, NEG))
        cnt = jnp.zeros((tq, LANE), I32)
        for i in range(n_sel):
            col = score[:, i:i + 1]
            beats = (col > score) | ((col == score) & (blk > i))
            cnt = cnt + beats.astype(I32)
        sel = ((cnt < n_top) & valid).astype(BF16)

        kstart = pl.multiple_of(jnp.clip(t0 - WINDOW, 0, T - wk), tq)
        kwin = kwb[pl.ds(kstart, wk), :]
        vwin = vwb[pl.ds(kstart, wk), :]
        s_w = _mxu(qs, kwin, True) * scale
        kp = kstart + lax.broadcasted_iota(I32, (1, wk), 1)
        wmask = (kp <= qpos_s) & (kp > qpos_s - WINDOW)
        p_w = _msoftmax_nonempty(s_w, wmask)
        o_win = _mxu(p_w.astype(BF16), vwin, False)

        gt = _sigmoid(g_ref[...])

        def selected(nkeys):
            selx = _mxu(sel, ex_ref[:, 0:nkeys], False)
            kpos = lax.broadcasted_iota(I32, (tq, nkeys), 1)
            qpos_t = t0 + lax.broadcasted_iota(I32, (tq, nkeys), 0)
            smask = (selx > 0.5) & (kpos <= qpos_t)
            for r in range(R):
                s_r = _mxu(qs[r * tq:(r + 1) * tq], ksb[0:nkeys, :], True) * scale
                p_r = _msoftmax_nonempty(s_r, smask)
                o_slc = _mxu(p_r.astype(BF16), vsb[0:nkeys, :], False)
                o_r = (gt[:, 3 * r:3 * r + 1] * o_cmp[r * tq:(r + 1) * tq]
                       + gt[:, 3 * r + 1:3 * r + 2] * o_slc
                       + gt[:, 3 * r + 2:3 * r + 3] * o_win[r * tq:(r + 1) * tq])
                o_ref[:, r * d:(r + 1) * d] = o_r.astype(BF16)

        n_span = pl.cdiv(t0 + tq, key_span)
        for n in range(1, T // key_span + 1):
            pl.when(n_span == n)(functools.partial(selected, n * key_span))

    seq = lambda blkcol: pl.BlockSpec((T, d), lambda b, g, qi: (b, blkcol(g)))
    cmp_spec = lambda s: pl.BlockSpec((None, None, None, ncp, d), lambda b, g, qi: (s, b, g, 0, 0))
    return pl.pallas_call(
        kern, grid=(B, G, nq),
        in_specs=[pl.BlockSpec((tq, R * d), lambda b, g, qi: (b * nq + qi, g)),
                  seq(lambda g: 2 * G + g), seq(lambda g: 3 * G + g),
                  seq(lambda g: g), seq(lambda g: G + g),
                  cmp_spec(0), cmp_spec(1),
                  pl.BlockSpec((tq, LANE), lambda b, g, qi: (b * nq + qi, g)),
                  pl.BlockSpec((ncp, LANE), lambda b, g, qi: (0, 0)),
                  pl.BlockSpec((LANE, T), lambda b, g, qi: (0, 0))],
        out_specs=pl.BlockSpec((tq, R * d), lambda b, g, qi: (b * nq + qi, g)),
        out_shape=jax.ShapeDtypeStruct((B * T, G * R * d), BF16),
        scratch_shapes=[pltpu.VMEM((T, d), BF16)] * 4 + [pltpu.VMEM((ncp, d), BF16)] * 2,
        compiler_params=_cp(("parallel", "parallel", "arbitrary")), name="nsa_prompt",
    )(q, kv, kv, win, win, kvcmp, kvcmp, gates, overlap, expand)


def nsa_sample_cmp(q8, kvcmp, q_pos):
    B = q8.shape[0]
    d, G, R = HEAD_DIM, NSA_G, NSA_R
    nch = kvcmp.shape[3]
    n_cmp = nch - 1
    n_sel = -(-(q_pos + 1) // SEL_BLOCK)
    nsp = -(-n_sel // LANE) * LANE
    overlap = _overlap_matrix(nch, nsp)
    scale = d ** -0.5

    def kern(q_ref, k_ref, v_ref, ov_ref, o_ref, imp_ref):
        n_idx = lax.broadcasted_iota(I32, (1, nch), 1)
        cmask = (n_idx * CMP_STRIDE + (CMP_BLOCK - 1) <= q_pos) & (n_idx < n_cmp)
        rows = []
        for g in range(G):
            s = _dot(q_ref[g], k_ref[g], nt=True) * scale
            pb = _msoftmax(s, cmask).astype(BF16)
            o_ref[g] = _dot(pb, v_ref[g])
            rows.append(jnp.sum(_mxu(pb, ov_ref[...], False)[0:R], axis=0, keepdims=True))
        imp_ref[...] = jnp.concatenate(rows + [jnp.zeros((SUBLANE - G, nsp), F32)], axis=0)

    qspec = pl.BlockSpec((None, G, SUBLANE, d), lambda b: (b, 0, 0, 0))
    cspec = lambda s: pl.BlockSpec((None, None, G, nch, d), lambda b: (s, b, 0, 0, 0))
    return pl.pallas_call(
        kern, grid=(B,),
        in_specs=[qspec, cspec(0), cspec(1), pl.BlockSpec((nch, nsp), lambda b: (0, 0))],
        out_specs=[qspec, pl.BlockSpec((None, SUBLANE, nsp), lambda b: (b, 0, 0))],
        out_shape=[jax.ShapeDtypeStruct((B, G, SUBLANE, d), F32),
                   jax.ShapeDtypeStruct((B, SUBLANE, nsp), F32)],
        compiler_params=_cp(("parallel",)), name="nsa_sample_cmp",
    )(q8, kvcmp, kvcmp, overlap)


def nsa_sample_topk(imp, q_pos):
    nr, nsp = imp.shape
    assert nr == LANE
    n_sel = -(-(q_pos + 1) // SEL_BLOCK)
    n_top = min(SEL_TOP_N, n_sel)
    cur = q_pos // SEL_BLOCK

    def kern(imp_ref, o_ref, sct):
        blk = lax.broadcasted_iota(I32, (nr, nsp), 1)
        valid = blk <= cur
        forced = (blk == 0) | (blk == cur) | (blk == cur - 1)
        score = jnp.where(forced, FORCE_SCORE, jnp.where(valid, imp_ref[...], NEG))
        eye = (lax.broadcasted_iota(I32, (nsp, nsp), 0) == lax.broadcasted_iota(I32, (nsp, nsp), 1)).astype(BF16)
        h, m, l = _split3(score)
        sct[...] = (_mxu(eye, h, True) + _mxu(eye, m, True)) + _mxu(eye, l, True)
        sc = sct[...]
        jidx = lax.broadcasted_iota(I32, (nsp, nr), 0)

        def body(i, cnt):
            row = sct[pl.ds(i, 1), :]
            beats = (row > sc) | ((row == sc) & (jidx > i))
            return cnt + beats.astype(I32)

        cnt = lax.fori_loop(0, n_sel, body, jnp.zeros((nsp, nr), I32))
        for p in range(n_top):
            o_ref[p:p + 1, :] = jnp.sum(jnp.where(cnt == p, jidx, 0), axis=0, keepdims=True)

    return pl.pallas_call(
        kern, grid=(1,),
        in_specs=[pl.BlockSpec((nr, nsp), lambda i: (0, 0))],
        out_specs=pl.BlockSpec((n_top, nr), lambda i: (0, 0)),
        out_shape=jax.ShapeDtypeStruct((n_top, nr), I32),
        scratch_shapes=[pltpu.VMEM((nsp, nr), F32)],
        compiler_params=_cp(("arbitrary",)), name="nsa_sample_topk",
    )(imp)


def nsa_sample_attend(q8, o_cmp, gates, idx, page_ids, pages, new_kv, cwin, new_win, e, q_pos):
    B = q8.shape[0]
    d, G, R = HEAD_DIM, NSA_G, NSA_R
    n_top = idx.shape[0] // (B * G)
    n_pages = page_ids.shape[0] // B
    n_cache_blocks = n_pages * (PAGE_SIZE // SEL_BLOCK)
    bpp = PAGE_SIZE // SEL_BLOCK
    w0 = cwin.shape[2]
    scale = d ** -0.5
    nk = n_top * SEL_BLOCK

    def kern(idx_ref, pid_ref, q_ref, oc_ref, g_ref, pages_ref, nkv_ref, cw_ref, nw_ref, o_ref,
             kvbuf, sem):
        b = pl.program_id(0)

        def slot_copy(slot, j):
            page = pid_ref[b * n_pages + j // bpp]
            r0 = (j % bpp) * SEL_BLOCK
            return pltpu.make_async_copy(
                pages_ref.at[page, pl.ds(r0, SEL_BLOCK), pl.ds(2, 2)], kvbuf.at[slot], sem.at[0])

        for slot in range(G * n_top):
            j = idx_ref[b * G * n_top + slot]

            @pl.when(j < n_cache_blocks)
            def _():
                slot_copy(slot, j).start()

            @pl.when(j >= n_cache_blocks)
            def _():
                kvbuf[slot] = jnp.zeros((SEL_BLOCK, 2, G, d), F32)

            @pl.when(j * SEL_BLOCK == q_pos)
            def _():
                kvbuf[slot, 0] = nkv_ref[2:4]

        for slot in range(G * n_top):
            j = idx_ref[b * G * n_top + slot]

            @pl.when(j < n_cache_blocks)
            def _():
                slot_copy(slot, j).wait()

        lane_k = lax.broadcasted_iota(I32, (1, nk), 1)
        row8 = lax.broadcasted_iota(I32, (SUBLANE, 1), 0)
        for g in range(G):
            qg = q_ref[g]
            kk = jnp.concatenate([kvbuf[g * n_top + p, :, 0, g, :] for p in range(n_top)], axis=0)
            vv = jnp.concatenate([kvbuf[g * n_top + p, :, 1, g, :] for p in range(n_top)], axis=0)
            kpos = lane_k & (SEL_BLOCK - 1)
            for p in range(n_top):
                j = idx_ref[(b * G + g) * n_top + p]
                kpos = kpos + jnp.where(lane_k // SEL_BLOCK == p, j * SEL_BLOCK, 0)
            s = _dot(qg, kk, nt=True) * scale
            p_s = _msoftmax(s, kpos <= q_pos)
            o_slc = _dot(p_s, vv)
            kw = cw_ref[:, 0, g, :]
            vw = cw_ref[:, 1, g, :]
            s_w = _dot(qg, kw, nt=True) * scale
            wpos = (q_pos - w0) + lax.broadcasted_iota(I32, (1, w0), 1)
            wmask = (wpos >= 0) & (wpos <= q_pos) & (wpos > q_pos - WINDOW)
            s_w = jnp.where(wmask, s_w, NEG)
            s_n = jnp.sum(_bf16r(qg) * _bf16r(nw_ref[0, g:g + 1, :]), axis=-1, keepdims=True) * scale
            m = jnp.maximum(jnp.max(s_w, axis=-1, keepdims=True), s_n)
            p_w = jnp.where(wmask, jnp.exp(s_w - m), 0.0)
            p_n = jnp.exp(s_n - m)
            den = jnp.sum(p_w, axis=-1, keepdims=True) + p_n
            o_win = _dot(p_w / den, vw) + _bf16r(p_n / den) * _bf16r(nw_ref[1, g:g + 1, :])
            gt = _sigmoid(g_ref[g])
            gc = [jnp.zeros((SUBLANE, 1), F32)] * 3
            for r in range(R):
                for c in range(3):
                    gc[c] = jnp.where(row8 == r, gt[:, 3 * r + c:3 * r + c + 1], gc[c])
            o_ref[g] = gc[0] * oc_ref[g] + gc[1] * o_slc + gc[2] * o_win

    qspec = pl.BlockSpec((None, G, SUBLANE, d), lambda b, i, p: (b, 0, 0, 0))
    gs = pltpu.PrefetchScalarGridSpec(
        num_scalar_prefetch=2, grid=(B,),
        in_specs=[qspec, qspec,
                  pl.BlockSpec((None, G, 1, LANE), lambda b, i, p: (b, 0, 0, 0)),
                  pl.BlockSpec(memory_space=pl.ANY),
                  pl.BlockSpec((None, N_KV_STREAMS, G, d), lambda b, i, p: (b, 0, 0, 0)),
                  pl.BlockSpec((None, None, w0, 2, G, d), lambda b, i, p: (e, b, 0, 0, 0, 0)),
                  pl.BlockSpec((None, 2, G, d), lambda b, i, p: (b, 0, 0, 0))],
        out_specs=qspec,
        scratch_shapes=[pltpu.VMEM((G * n_top, SEL_BLOCK, 2, G, d), F32),
                        pltpu.SemaphoreType.DMA((1,))],
    )
    return pl.pallas_call(
        kern, grid_spec=gs, out_shape=jax.ShapeDtypeStruct((B, G, SUBLANE, d), F32),
        compiler_params=_cp(("arbitrary",)), name="nsa_sample_attend",
    )(idx, page_ids, q8, o_cmp, gates, pages, new_kv, cwin, new_win)


def rglru(u, gb, conv0, h0, conv_w, conv_b, lru_wa, lru_ba, lru_wx, lru_bx, lru_lambda, e, B, T):
    W = u.shape[1]
    cw = RNN_BLOCK_W
    nblk = W // cw
    tc = min(T, 256)
    nt = T // tc
    assert T % tc == 0 and tc % SUBLANE == 0

    def kern(u_ref, gb_ref, c0_ref, h0_ref, cw_ref, cb_ref, wa_ref, ba_ref, wx_ref, bx_ref, lam_ref,
             y_ref, h_ref, prev, hprev):
        ti = pl.program_id(2)

        @pl.when(ti == 0)
        def _():
            prev[...] = c0_ref[...]
            hprev[...] = h0_ref[...]

        uu = u_ref[...]
        ext = _bf16r(jnp.concatenate([prev[...], uu], axis=0))
        prev[...] = uu[tc - SUBLANE:tc]
        cwr = _bf16r(cw_ref[...])
        xc = cwr[0:1, :] * ext[5:5 + tc]
        for j in range(1, CONV_WIDTH):
            xc = xc + cwr[j:j + 1, :] * ext[5 + j:5 + j + tc]
        xc = xc + cb_ref[...]
        r = _sigmoid(_dot(xc, wa_ref[...]) + ba_ref[...])
        i = _sigmoid(_dot(xc, wx_ref[...]) + bx_ref[...])
        nl = -lam_ref[...]
        softplus = jnp.maximum(nl, 0.0) + jnp.log1p(jnp.exp(-jnp.abs(nl)))
        log_a = -LRU_C * r * softplus
        a = jnp.exp(log_a)
        bt = jnp.sqrt(-jnp.tanh(log_a) * (a * a + 1.0)) * (i * xc)
        rows = lax.broadcasted_iota(I32, (tc, 1), 0)
        step = 1
        while step < tc:
            keep = rows >= step
            bt = jnp.where(keep, a * pltpu.roll(bt, step, 0) + bt, bt)
            a = jnp.where(keep, a * pltpu.roll(a, step, 0), a)
            step *= 2
        h = a * hprev[...] + bt
        hprev[...] = h[tc - 1:tc]
        h_ref[...] = h
        y_ref[...] = (h * _gelu(gb_ref[...])).astype(BF16)

    tile = pl.BlockSpec((tc, cw), lambda b, c, t: (b * nt + t, c))
    vec = lambda: pl.BlockSpec((None, 1, cw), lambda b, c, t: (e, 0, c))
    mat = lambda: pl.BlockSpec((None, None, cw, cw), lambda b, c, t: (e, c, 0, 0))
    bvec = lambda: pl.BlockSpec((None, None, 1, cw), lambda b, c, t: (e, c, 0, 0))
    return pl.pallas_call(
        kern, grid=(B, nblk, nt),
        in_specs=[tile, tile,
                  pl.BlockSpec((None, SUBLANE, cw), lambda b, c, t: (b, 0, c)),
                  pl.BlockSpec((None, 1, cw), lambda b, c, t: (b, 0, c)),
                  pl.BlockSpec((None, CONV_WIDTH, cw), lambda b, c, t: (e, 0, c)),
                  vec(), mat(), bvec(), mat(), bvec(), vec()],
        out_specs=[tile, tile],
        out_shape=[jax.ShapeDtypeStruct((B * T, W), BF16), jax.ShapeDtypeStruct((B * T, W), F32)],
        scratch_shapes=[pltpu.VMEM((SUBLANE, cw), F32), pltpu.VMEM((1, cw), F32)],
        compiler_params=_cp(("parallel", "parallel", "arbitrary")), name="rglru",
    )(u, gb, conv0, h0, conv_w, conv_b[:, None, :], lru_wa, lru_ba[:, :, None, :], lru_wx,
      lru_bx[:, :, None, :], lru_lambda[:, None, :])


def pool_mixer(x, buf16, pool_w, pool_scale, o, B, T, t0):
    D = x.shape[1]
    ng = len(POOL_WINDOWS)
    gw = D // ng
    tt = min(T, 256)
    nt = T // tt
    halo = 2 * SUBLANE
    assert T % tt == 0 and tt >= halo and POOL_WINDOWS == (2, 4, 8, 16)

    def kern(x_ref, buf_ref, w_ref, sc_ref, o_ref, prev):
        g, ti = pl.program_id(0), pl.program_id(2)

        @pl.when(ti == 0)
        def _():
            prev[...] = buf_ref[...]

        xx = x_ref[...]
        ext = jnp.concatenate([prev[...], xx], axis=0)
        prev[...] = xx[tt - halo:tt]
        s2 = ext + pltpu.roll(ext, 1, 0)
        s4 = s2 + pltpu.roll(s2, 2, 0)
        s8 = s4 + pltpu.roll(s4, 4, 0)
        s16 = s8 + pltpu.roll(s8, 8, 0)
        sw = jnp.where(g == 0, s2, jnp.where(g == 1, s4, jnp.where(g == 2, s8, s16)))[halo:]
        wlen = jnp.left_shift(2, g)
        pos = t0 + ti * tt + lax.broadcasted_iota(I32, (tt, 1), 0)
        cnt = jnp.minimum(wlen, pos + 1).astype(F32)
        pooled = sw / cnt - xx
        o_ref[...] = _dot(pooled, w_ref[...]) * sc_ref[...]

    return pl.pallas_call(
        kern, grid=(ng, B, nt),
        in_specs=[pl.BlockSpec((tt, gw), lambda g, b, t: (b * nt + t, g)),
                  pl.BlockSpec((None, halo, gw), lambda g, b, t: (b, 0, g)),
                  pl.BlockSpec((None, None, gw, gw), lambda g, b, t: (o, g, 0, 0)),
                  pl.BlockSpec((None, 1, gw), lambda g, b, t: (o, 0, g))],
        out_specs=pl.BlockSpec((tt, gw), lambda g, b, t: (b * nt + t, g)),
        out_shape=jax.ShapeDtypeStruct((B * T, D), F32),
        scratch_shapes=[pltpu.VMEM((halo, gw), F32)],
        compiler_params=_cp(("parallel", "parallel", "arbitrary")), name="pool_mixer",
    )(x, buf16, pool_w, pool_scale[:, None, :])


def mem_attn(q, mkv, B, T, kv_off=0):
    MW = q.shape[1]
    M = mkv.shape[1]
    dh = MW // MEM_HEADS
    tq = min(T, 256)
    nq = T // tq
    scale = dh ** -0.5

    def kern(q_ref, kv_ref, o_ref):
        for h in range(MEM_HEADS):
            sl = slice(h * dh, (h + 1) * dh)
            s = _dot(q_ref[:, sl], kv_ref[:, 0, sl], nt=True) * scale
            m = jnp.max(s, axis=-1, keepdims=True)
            p = jnp.exp(s - m)
            p = p / jnp.sum(p, axis=-1, keepdims=True)
            o_ref[:, sl] = _dot(p, kv_ref[:, 1, sl]).astype(BF16)

    return pl.pallas_call(
        kern, grid=(B, nq),
        in_specs=[pl.BlockSpec((tq, MW), lambda b, i: (b * nq + i, 0)),
                  pl.BlockSpec((None, M, 2, MW), lambda b, i: (kv_off + b, 0, 0, 0))],
        out_specs=pl.BlockSpec((tq, MW), lambda b, i: (b * nq + i, 0)),
        out_shape=jax.ShapeDtypeStruct((B * T, MW), BF16),
        compiler_params=_cp(("parallel", "parallel")), name="mem_attn",
    )(q, mkv)


ROUTER_ROWS = 40


def moe_router(x, w_t, bias, tm=512):
    N, D = x.shape
    tm = min(tm, N)
    assert N % tm == 0 and tm % LANE == 0

    def kern(x_ref, w_ref, b_ref, e_ref, g_ref):
        lt = _dot(w_ref[...], x_ref[...], nt=True) + b_ref[...]
        c = [lt[i:i + 1] for i in range(MOE_GROUPS)]
        m = functools.reduce(jnp.maximum, c)
        ex = [jnp.exp(ci - m) for ci in c]
        tot = functools.reduce(jnp.add, ex)
        pc = [ei / tot for ei in ex]
        pg = functools.reduce(jnp.maximum, pc)
        grp = jnp.full(pg.shape, MOE_GROUPS - 1, I32)
        for i in range(MOE_GROUPS - 2, -1, -1):
            grp = jnp.where(pc[i] == pg, i, grp)
        lf = []
        for j in range(MOE_EPG):
            v = lt[MOE_GROUPS + j:MOE_GROUPS + j + 1]
            for gi in range(1, MOE_GROUPS):
                r0 = MOE_GROUPS + gi * MOE_EPG + j
                v = jnp.where(grp == gi, lt[r0:r0 + 1], v)
            lf.append(v)
        m = functools.reduce(jnp.maximum, lf)
        ex = [jnp.exp(v - m) for v in lf]
        tot = functools.reduce(jnp.add, ex)
        pf = [ei / tot for ei in ex]

        def first_max(vals):
            best = functools.reduce(jnp.maximum, vals)
            arg = jnp.full(best.shape, len(vals) - 1, I32)
            for j in range(len(vals) - 2, -1, -1):
                arg = jnp.where(vals[j] == best, j, arg)
            return best, arg

        p1, j1 = first_max(pf)
        p2, j2 = first_max([jnp.where(j1 == j, -1.0, pf[j]) for j in range(MOE_EPG)])
        den = p1 + p2
        e_ref[0:1, :] = grp * MOE_EPG + j1
        e_ref[1:2, :] = grp * MOE_EPG + j2
        g_ref[0:1, :] = pg * p1 / den
        g_ref[1:2, :] = pg * p2 / den

    out = pl.BlockSpec((MOE_TOP_K, tm), lambda i: (0, i))
    return pl.pallas_call(
        kern, grid=(N // tm,),
        in_specs=[pl.BlockSpec((tm, D), lambda i: (i, 0)),
                  pl.BlockSpec((ROUTER_ROWS, D), lambda i: (0, 0)),
                  pl.BlockSpec((ROUTER_ROWS, 1), lambda i: (0, 0))],
        out_specs=[out, out],
        out_shape=[jax.ShapeDtypeStruct((MOE_TOP_K, N), I32), jax.ShapeDtypeStruct((MOE_TOP_K, N), F32)],
        compiler_params=_cp(("parallel",)), name="moe_router",
    )(x, w_t, bias)


MOE_SUB = 2
MOE_TF = 128


def moe_experts_grouped(x3, tok, aid, nvalid, gexp, nsub, nused, w_gate, w_up, w_down, layer, rows, n_assign):
    _, nc, _ = x3.shape
    D = nc * LANE
    SB, tf = MOE_SUB, MOE_TF
    n_groups = tok.shape[0] // (SB * rows)
    nf = w_gate.shape[3] // tf
    assert nf >= 2 and SB == 2

    def kern(tok_ref, aid_ref, nv_ref, gexp_ref, nsub_ref, nused_ref, x3_ref, wg_ref, wu_ref, wd_ref, y_ref,
             xbuf, xb, acc, ybuf, sem_in, sem_out):
        g, f = pl.program_id(0), pl.program_id(1)
        nused = nused_ref[0]
        active = g < nused
        ns = nsub_ref[g]

        def gather_start(grp, j):
            base = (grp * SB + j) * rows

            def body(r, carry):
                pltpu.make_async_copy(x3_ref.at[tok_ref[base + r]], xbuf.at[j, r], sem_in.at[j]).start()
                return carry
            lax.fori_loop(0, rows, body, 0, unroll=SUBLANE)

        def gather_wait(j):
            pltpu.make_async_copy(x3_ref.at[pl.ds(0, rows)], xbuf.at[j], sem_in.at[j]).wait()

        def scatter_start(grp, j):
            base = (grp * SB + j) * rows

            def body(r, carry):
                pltpu.make_async_copy(ybuf.at[j, r], y_ref.at[aid_ref[base + r]], sem_out.at[j]).start()
                return carry
            lax.fori_loop(0, nv_ref[grp * SB + j], body, 0)

        def scatter_wait(grp, j):
            nv = nv_ref[grp * SB + j]

            @pl.when(nv > 0)
            def _():
                pltpu.make_async_copy(ybuf.at[j, pl.ds(0, nv)], y_ref.at[pl.ds(0, nv)], sem_out.at[j]).wait()

        @pl.when(active & (f == 0))
        def _():
            pl.when(g == 0)(lambda: gather_start(0, 0))
            pl.when(ns > 1)(lambda: gather_start(g, 1))
            for j in range(SB):
                @pl.when(j < ns)
                def _():
                    gather_wait(j)
                    xb[j * rows:(j + 1) * rows, :] = xbuf[j].reshape(rows, D).astype(BF16)
            acc[...] = jnp.zeros_like(acc)

        pl.when((f == nf - 2) & (g + 1 < nused))(lambda: gather_start(g + 1, 0))

        @pl.when(active)
        def _():
            wg, wu, wd = wg_ref[...].astype(BF16), wu_ref[...].astype(BF16), wd_ref[...].astype(BF16)
            for j in range(SB):
                @pl.when(j < ns)
                def _():
                    x = xb[j * rows:(j + 1) * rows, :]
                    hg = _mxu(x, wg, False)
                    hu = _mxu(x, wu, False)
                    hid = (hg * _sigmoid(hg) * hu).astype(BF16)
                    acc[j * rows:(j + 1) * rows, :] += _mxu(hid, wd, False)

        @pl.when(active & (f == nf - 1))
        def _():
            for j in range(SB):
                pl.when((g >= 1) & (j < nsub_ref[jnp.maximum(g - 1, 0)]))(
                    functools.partial(scatter_wait, jnp.maximum(g - 1, 0), j))

                @pl.when(j < ns)
                def _():
                    ybuf[j] = _token_major(acc[j * rows:(j + 1) * rows, :])
                    scatter_start(g, j)

            @pl.when(g == nused - 1)
            def _():
                for j in range(SB):
                    pl.when(j < ns)(functools.partial(scatter_wait, g, j))

    def live(g, nu):
        return jnp.maximum(jnp.minimum(g, nu[0] - 1), 0)

    def feff(g, f, nu):
        return jnp.where(g < nu[0], f, nf - 1)

    def wspec(shape, col):
        def imap(g, f, t, a, nv, ge, nsb, nu):
            e, ff = ge[live(g, nu)], feff(g, f, nu)
            return (layer, e, 0, ff) if col else (layer, e, ff, 0)
        return pl.BlockSpec(shape, imap)

    gs = pltpu.PrefetchScalarGridSpec(
        num_scalar_prefetch=6, grid=(n_groups, nf),
        in_specs=[pl.BlockSpec(memory_space=pl.ANY),
                  wspec((None, None, D, tf), True), wspec((None, None, D, tf), True),
                  wspec((None, None, tf, D), False)],
        out_specs=pl.BlockSpec(memory_space=pl.ANY),
        scratch_shapes=[pltpu.VMEM((SB, rows, nc, LANE), F32), pltpu.VMEM((SB * rows, D), BF16),
                        pltpu.VMEM((SB * rows, D), F32), pltpu.VMEM((SB, rows, nc, LANE), F32),
                        pltpu.SemaphoreType.DMA((SB,)), pltpu.SemaphoreType.DMA((SB,))],
    )
    return pl.pallas_call(
        kern, grid_spec=gs, out_shape=jax.ShapeDtypeStruct((n_assign, nc, LANE), F32),
        compiler_params=_cp(("arbitrary", "arbitrary")), name="moe_experts",
    )(tok, aid, nvalid, gexp, nsub, nused, x3, w_gate, w_up, w_down)


def moe_combine_ln(x, ytok, gate, ln_g, ln_b, idx, alpha, n_tok, tm=128):
    D = x.shape[1]
    nc = D // LANE
    tm = min(tm, n_tok)

    def kern(x_ref, y_ref, gt_ref, g_ref, b_ref, o_ref, ob_ref):
        gt = _bf16r(gt_ref[...])
        mix = sum(gt[:, k:k + 1] * _bf16r(y_ref[:, k].reshape(tm, D)) for k in range(MOE_TOP_K))
        y = _layer_norm(alpha * x_ref[...] + mix, g_ref[...], b_ref[...])
        o_ref[...] = y
        ob_ref[...] = y.astype(BF16)

    row = pl.BlockSpec((tm, D), lambda i: (i, 0))
    par = pl.BlockSpec((None, 1, D), lambda i: (idx, 0, 0))
    return pl.pallas_call(
        kern, grid=(n_tok // tm,),
        in_specs=[row, pl.BlockSpec((tm, MOE_TOP_K, nc, LANE), lambda i: (i, 0, 0, 0)),
                  pl.BlockSpec((tm, MOE_TOP_K), lambda i: (i, 0)), par, par],
        out_specs=[row, row],
        out_shape=[jax.ShapeDtypeStruct((n_tok, D), F32), jax.ShapeDtypeStruct((n_tok, D), BF16)],
        compiler_params=_cp(("parallel",)), name="moe_combine_ln",
    )(x, ytok, gate, ln_g, ln_b)


def hier_moe_ln(x, x3, n_tok, w_rt, b_rt, w_gate, w_up, w_down, layer, ln_g, ln_b, alpha, rows):
    N, D = x.shape
    xr = x if N % LANE == 0 else jnp.pad(x, ((0, LANE - N % LANE), (0, 0)))
    eidx, gate = moe_router(xr, w_rt, b_rt)
    eidx, gate = eidx[:, :n_tok], gate[:, :n_tok]
    A = n_tok * MOE_TOP_K
    e = eidx.T.reshape(A)
    onehot = (e[:, None] == jnp.arange(MOE_EXPERTS, dtype=I32)[None, :]).astype(I32)
    csum = jnp.cumsum(onehot, axis=0)
    counts = csum[-1]
    pos = jnp.sum(csum * onehot, axis=1) - 1
    grows = MOE_SUB * rows
    padded = (counts + grows - 1) // grows * grows
    pad_end = jnp.cumsum(padded)
    pad_start = pad_end - padded
    dest = (jnp.sum(onehot * pad_start[None, :], axis=1) + pos).astype(I32)
    n_groups = min(A // grows + MOE_EXPERTS, A)
    gexp = jnp.minimum(jnp.searchsorted(pad_end, jnp.arange(n_groups, dtype=I32) * grows, side="right"),
                       MOE_EXPERTS - 1).astype(I32)
    nused = (pad_end[-1:] // grows).astype(I32)
    valid_end = (pad_start + counts)[gexp]
    nvalid = jnp.clip(jnp.repeat(valid_end, MOE_SUB) - jnp.arange(n_groups * MOE_SUB, dtype=I32) * rows,
                      0, rows).astype(I32)
    nsub = jnp.sum((nvalid > 0).reshape(n_groups, MOE_SUB), axis=1).astype(I32)
    aid = jnp.zeros((n_groups * grows,), I32).at[dest].set(jnp.arange(A, dtype=I32))
    ytok = moe_experts_grouped(x3, aid // MOE_TOP_K, aid, nvalid, gexp, nsub, nused, w_gate, w_up, w_down,
                               layer, rows, A)
    ytok = ytok.reshape(n_tok, MOE_TOP_K, D // LANE, LANE)
    return moe_combine_ln(x, ytok, gate.T, ln_g, ln_b, layer * 3 + 2, alpha, n_tok)


def _rope_tables(pos):
    half = HEAD_DIM // 2
    inv_freq = ROPE_THETA ** (-jnp.arange(half, dtype=F32) / half)
    ang = pos.astype(F32)[:, None] * inv_freq[None, :]
    cos, sin = jnp.cos(ang), jnp.sin(ang)
    return jnp.concatenate([cos, cos], axis=1), jnp.concatenate([-sin, sin], axis=1)


def _even_projections(x, pos_rows, w_in, w_rest, w_gates, e, tm, q_dtype):
    G, d = NSA_G, HEAD_DIM
    q_cols = G * NSA_R * d
    kv_cols = N_KV_STREAMS * G * d
    cos, sin = _rope_tables(pos_rows)
    tn = G * d
    mm = functools.partial(matmul, x, tm=tm)
    q = mm(w_in, layer=e, col_off=0, n_out=q_cols, tn=tn, rope=(cos, sin, tuple(range(q_cols // tn))),
           out_dtype=q_dtype)
    kv = mm(w_in, layer=e, col_off=q_cols, n_out=kv_cols, tn=tn, rope=(cos, sin, (0, 2)))
    win = mm(w_in, layer=e, col_off=q_cols + kv_cols, n_out=2 * G * d, tn=tn, rope=(cos, sin, (0,)))
    gates = mm(w_gates, tn=G * LANE)
    rw = w_rest.shape[1] // 2
    u = mm(w_rest, n_out=rw)
    gb = mm(w_rest, col_off=rw, n_out=rw)
    return q, kv, win, gates, u, gb


def _pad_rows(a, n):
    return jnp.pad(a, ((0, n - a.shape[0]),) + ((0, 0),) * (a.ndim - 1))


def kernel(x_prompt, x_sample, mem_prompt, cache_nsa_kv, cache_nsa_win, state_rglru_h, state_rglru_conv,
           state_pool, cache_mem_kv, page_table, w_in, w_out, cmp_w1, cmp_w2, cmp_pe, conv_w, conv_b,
           lru_wa, lru_ba, lru_wx, lru_bx, lru_lambda, pool_w, pool_scale, mem_wq, mem_wk, mem_wv, mem_wo,
           ln_g, ln_b, moe_w_coarse, moe_b_coarse, moe_w_fine, moe_b_fine, moe_w_gate, moe_w_up, moe_w_down):
    B, T, D = x_prompt.shape
    DB = x_sample.shape[0]
    depth = ln_g.shape[0]
    n_pages = page_table.shape[1]
    past_len = n_pages * PAGE_SIZE
    n_phys = cache_nsa_kv.shape[1]
    G, R, d = NSA_G, NSA_R, HEAD_DIM
    alpha = (2 * depth) ** 0.25
    q_cols, kv_cols, win_cols, gate_cols = G * R * d, N_KV_STREAMS * G * d, 2 * G * d, 3 * G * R
    rest_off = q_cols + kv_cols + win_cols + gate_cols
    RW = (w_in.shape[2] - rest_off) // 2
    MW = mem_wq.shape[2]
    M = mem_prompt.shape[1]
    SR = SAMPLE_ROWS
    TS = SAMPLE_ROWS

    xp = x_prompt.reshape(B * T, D)
    xpb = xp.astype(BF16)
    xs = _pad_rows(x_sample.reshape(DB, D), SR)
    lng = ln_g.reshape(depth * 3, 1, D)
    lnb = ln_b.reshape(depth * 3, 1, D)
    memf = mem_prompt.reshape(B * M, D)

    kv_p, kv_s, win_p, win_s, h_p, h_s, cv_p, cv_s, pl_p, pl_s, mem_p = ([] for _ in range(11))
    for layer in range(depth):
        if layer % 2 == 0:
            e = layer // 2
            wg = w_in[e, :, q_cols + kv_cols + win_cols:rest_off].reshape(D, G, 3 * R)
            wg = jnp.pad(wg, ((0, 0), (0, 0), (0, LANE - 3 * R))).reshape(D, G * LANE)
            w_rest = w_in[e, :, rest_off:]

            q, kv, win, gates, u, gb = _even_projections(
                xpb, jnp.arange(T), w_in, w_rest, wg, e, 1024, BF16)
            pages_p = kv.reshape(B * T // PAGE_SIZE, PAGE_SIZE, N_KV_STREAMS, G, d)
            new_kv = pages_p.reshape(B, T, N_KV_STREAMS, G, d)
            ab = compress_ab(pages_p, jnp.arange(B * T // PAGE_SIZE, dtype=I32), cmp_w1, cmp_pe, e, B,
                             T // PAGE_SIZE)
            kvcmp = compress_fin(ab, cmp_w2, e)
            o_nsa = nsa_prompt(q, kv, win, kvcmp, gates, B, T)
            y_rnn, h_all = rglru(u, gb, jnp.zeros((B, SUBLANE, RW), F32), jnp.zeros((B, 1, RW), F32),
                                 conv_w, conv_b, lru_wa, lru_ba, lru_wx, lru_bx, lru_lambda, e, B, T)
            mix_p = matmul(o_nsa, w_out, layer=e, x2=y_rnn)
            keep = min(WINDOW, T)
            kv_p.append(new_kv)
            win_p.append(win.reshape(B, T, 2, G, d)[:, T - keep:])
            h_p.append(h_all.reshape(B, T, RW)[:, T - 1])
            cv_p.append(u.reshape(B, T, RW)[:, T - (CONV_WIDTH - 1):])

            q, kv, win, gates, u, gb = _even_projections(
                xs, jnp.full((SR,), past_len), w_in, w_rest, wg, e, SR, F32)
            new_kv_s = kv[:DB].reshape(DB, N_KV_STREAMS, G, d)
            new_win_s = win[:DB].reshape(DB, 2, G, d)
            pages_s = cache_nsa_kv.reshape(cache_nsa_kv.shape[0] * n_phys, PAGE_SIZE, N_KV_STREAMS, G, d)
            pid = (page_table.reshape(DB * n_pages) + e * n_phys).astype(I32)
            ab = compress_ab(pages_s, pid, cmp_w1, cmp_pe, e, DB, n_pages)
            kvcmp = compress_fin(ab, cmp_w2, e)
            q8 = jnp.pad(q[:DB].reshape(DB, G, R, d), ((0, 0), (0, 0), (0, SUBLANE - R), (0, 0)))
            o_cmp, imp = nsa_sample_cmp(q8, kvcmp, past_len)
            idx_t = nsa_sample_topk(_pad_rows(imp.reshape(DB * SUBLANE, -1), LANE), past_len)
            n_top = idx_t.shape[0]
            idx = idx_t[:, :DB * SUBLANE].T.reshape(DB, SUBLANE, n_top)[:, :G].reshape(DB * G * n_top)
            o8 = nsa_sample_attend(q8, o_cmp, gates[:DB].reshape(DB, G, 1, LANE), idx, pid, pages_s,
                                   new_kv_s, cache_nsa_win, new_win_s, e, past_len)
            o_nsa = _pad_rows(o8[:, :, :R].reshape(DB, G * R * d), SR)
            conv0 = state_rglru_conv[e]
            u_t = jnp.pad(u[:DB, None, :], ((0, 0), (0, TS - 1), (0, 0))).reshape(DB * TS, RW)
            gb_t = jnp.pad(gb[:DB, None, :], ((0, 0), (0, TS - 1), (0, 0))).reshape(DB * TS, RW)
            c0 = jnp.pad(conv0, ((0, 0), (SUBLANE - (CONV_WIDTH - 1), 0), (0, 0)))
            y_t, h_t = rglru(u_t, gb_t, c0, state_rglru_h[e][:, None, :], conv_w, conv_b, lru_wa, lru_ba,
                             lru_wx, lru_bx, lru_lambda, e, DB, TS)
            y_rnn = _pad_rows(y_t.reshape(DB, TS, RW)[:, 0], SR)
            mix_s = matmul(o_nsa, w_out, layer=e, x2=y_rnn)
            keep = min(WINDOW, past_len + 1)
            win_all = jnp.concatenate([cache_nsa_win[e], new_win_s[:, None]], axis=1)
            kv_s.append(new_kv_s[:, None])
            win_s.append(win_all[:, win_all.shape[1] - keep:])
            h_s.append(h_t.reshape(DB, TS, RW)[:, 0])
            cv_s.append(jnp.concatenate([conv0, u[:DB, None, :]], axis=1)[:, 1:])
        else:
            o = layer // 2
            mix_p = pool_mixer(xp, jnp.zeros((B, 2 * SUBLANE, D), F32), pool_w, pool_scale, o, B, T, 0)
            pl_p.append(xp.reshape(B, T, D)[:, T - POOL_BUF:])
            tsp = 2 * SUBLANE
            x_t = jnp.pad(xs[:DB, None, :], ((0, 0), (0, tsp - 1), (0, 0))).reshape(DB * tsp, D)
            buf = jnp.pad(state_pool[o], ((0, 0), (tsp - POOL_BUF, 0), (0, 0)))
            mix_t = pool_mixer(x_t, buf, pool_w, pool_scale, o, DB, tsp, past_len)
            mix_s = _pad_rows(mix_t.reshape(DB, tsp, D)[:, 0], SR)
            pl_s.append(jnp.concatenate([state_pool[o], xs[:DB, None, :]], axis=1)[:, 1:])

        xp, xpb = add_ln(xp, mix_p, lng, lnb, layer * 3, alpha)
        xs, _ = add_ln(xs, mix_s, lng, lnb, layer * 3, alpha)

        w_kv = jnp.concatenate([mem_wk[layer], mem_wv[layer]], axis=1)
        mkv = matmul(memf, w_kv).reshape(B, M, 2, MW)
        mem_p.append(mkv)
        qm = matmul(xpb, mem_wq, layer=layer, out_dtype=BF16)
        att = mem_attn(qm, mkv, B, T)
        xp, xpb, xp3 = add_ln(xp, matmul(att, mem_wo, layer=layer), lng, lnb, layer * 3 + 1, alpha,
                              token_major=True)
        qm = matmul(xs, mem_wq, layer=layer)
        qm_t = jnp.pad(qm[:DB, None, :], ((0, 0), (0, TS - 1), (0, 0))).reshape(DB * TS, MW)
        att_t = mem_attn(qm_t, cache_mem_kv.reshape((-1,) + cache_mem_kv.shape[2:]), DB, TS,
                         kv_off=layer * DB)
        att = _pad_rows(att_t.reshape(DB, TS, MW)[:, 0], SR)
        xs, _, xs3 = add_ln(xs, matmul(att, mem_wo, layer=layer), lng, lnb, layer * 3 + 1, alpha,
                            token_major=True)

        w_rt = _pad_rows(jnp.concatenate([moe_w_coarse[layer], moe_w_fine[layer]], axis=1).T, ROUTER_ROWS)
        b_rt = _pad_rows(jnp.concatenate([moe_b_coarse[layer], moe_b_fine[layer]])[:, None], ROUTER_ROWS)
        xp, xpb = hier_moe_ln(xp, xp3, B * T, w_rt, b_rt, moe_w_gate, moe_w_up, moe_w_down, layer, lng, lnb,
                              alpha, rows=256)
        xs = _pad_rows(hier_moe_ln(xs, xs3, DB, w_rt, b_rt, moe_w_gate, moe_w_up, moe_w_down, layer, lng, lnb,
                                   alpha, rows=SUBLANE)[0], SR)

    return (xp.reshape(B, T, D), xs[:DB].reshape(DB, 1, D), jnp.stack(kv_p), jnp.stack(kv_s),
            jnp.stack(win_p), jnp.stack(win_s), jnp.stack(h_p), jnp.stack(h_s), jnp.stack(cv_p),
            jnp.stack(cv_s), jnp.stack(pl_p), jnp.stack(pl_s), jnp.stack(mem_p))
```

```python
import functools
import math

import jax
import jax.numpy as jnp
from jax import lax
from jax.experimental import pallas as pl
from jax.experimental.pallas import tpu as pltpu

F32 = jnp.float32
BF16 = jnp.bfloat16
I32 = jnp.int32

HEAD_DIM = 128
NSA_G = 4
NSA_R = 4
N_KV_STREAMS = 4
CMP_BLOCK = 32
CMP_STRIDE = 16
SEL_BLOCK = 64
SEL_TOP_N = 16
WINDOW = 512
FORCE_SCORE = 1e9
NEG = -1e30
ROPE_THETA = 10000.0
CONV_WIDTH = 4
LRU_C = 8.0
RNN_BLOCK_W = 128
POOL_WINDOWS = (2, 4, 8, 16)
POOL_BUF = 15
MEM_HEADS = 4
MOE_GROUPS = 4
MOE_EPG = 8
MOE_EXPERTS = MOE_GROUPS * MOE_EPG
MOE_TOP_K = 2
LN_EPS = 1e-5
PAGE_SIZE = 128

LANE = 128
SUBLANE = 8
VMEM_LIMIT = 56 * 1024 * 1024
SAMPLE_ROWS = 16


def _cp(sem, vmem=VMEM_LIMIT):
    return pltpu.CompilerParams(dimension_semantics=sem, vmem_limit_bytes=vmem)


def _split3(x):
    h = x.astype(BF16)
    r = x - h.astype(F32)
    m = r.astype(BF16)
    l = (r - m.astype(F32)).astype(BF16)
    return h, m, l


def _mxu(a, b, nt):
    if nt:
        return lax.dot_general(a, b, (((1,), (1,)), ((), ())), preferred_element_type=F32)
    return jnp.dot(a, b, preferred_element_type=F32)


def _dot(a, b, nt=False):
    return _mxu(a.astype(BF16), b.astype(BF16), nt)


def _bf16r(x):
    return x.astype(BF16).astype(F32)


def _sigmoid(x):
    return 1.0 / (1.0 + jnp.exp(-x))


def _gelu(x):
    return 0.5 * x * (1.0 + jnp.tanh(0.7978845608028654 * (x + 0.044715 * (x * x * x))))


def _msoftmax(s, mask):
    s = jnp.where(mask, s, NEG)
    m = jnp.max(s, axis=-1, keepdims=True)
    p = jnp.where(mask, jnp.exp(s - m), 0.0)
    return p / jnp.maximum(jnp.sum(p, axis=-1, keepdims=True), 1e-30)


def _msoftmax_nonempty(s, mask):
    s = jnp.where(mask, s, NEG)
    p = jnp.exp(s - jnp.max(s, axis=-1, keepdims=True))
    return p * (1.0 / jnp.sum(p, axis=-1, keepdims=True))


def _layer_norm(v, g, b):
    mu = jnp.mean(v, axis=-1, keepdims=True)
    c = v - mu
    var = jnp.mean(c * c, axis=-1, keepdims=True)
    return c * lax.rsqrt(var + LN_EPS) * g + b


def matmul(x, w, *, layer=0, col_off=0, n_out=None, x2=None, tm=1024, tn=512, rope=None, out_dtype=F32):
    if w.ndim == 2:
        w = w[None]
    M, K1 = x.shape
    K2 = 0 if x2 is None else x2.shape[1]
    assert w.shape[1] == K1 + K2
    n_out = w.shape[2] - col_off if n_out is None else n_out
    xs_in = [x] if x2 is None else [x, x2]
    if any(a.dtype != BF16 for a in xs_in):
        tm = min(tm, 512)
    tm, tn = min(tm, M), min(tn, n_out)
    assert M % tm == 0 and n_out % tn == 0 and col_off % tn == 0
    joff = col_off // tn

    in_specs = [pl.BlockSpec((tm, a.shape[1]), lambda i, j: (i, 0)) for a in xs_in]
    in_specs.append(pl.BlockSpec((None, K1 + K2, tn), lambda i, j: (layer, 0, j + joff)))
    args = xs_in + [w]
    rope_blocks = ()
    if rope is not None:
        cos, sin, rope_blocks = rope
        period = cos.shape[0] // tm
        assert cos.shape[0] % tm == 0
        for t in (cos, sin):
            in_specs.append(pl.BlockSpec((tm, LANE), lambda i, j: (i % period, 0)))
            args.append(t)
    cast = [a.dtype != BF16 for a in xs_in]
    scratch = [pltpu.VMEM((tm, a.shape[1]), BF16) for a, c in zip(xs_in, cast) if c]

    def kern(*refs):
        refs = list(refs)
        x_refs = [refs.pop(0) for _ in xs_in]
        w_ref = refs.pop(0)
        cos_ref, sin_ref = (refs.pop(0), refs.pop(0)) if rope is not None else (None, None)
        o_ref = refs.pop(0)
        j = pl.program_id(1)
        xb_refs = [refs.pop(0) if c else xr for xr, c in zip(x_refs, cast)]

        if any(cast):
            @pl.when(j == 0)
            def _():
                for xr, xb, c in zip(x_refs, xb_refs, cast):
                    if c:
                        xb[...] = xr[...].astype(BF16)

        acc = _mxu(xb_refs[0][...], w_ref[0:K1, :].astype(BF16), False)
        if x2 is not None:
            acc = acc + _mxu(xb_refs[1][...], w_ref[K1:K1 + K2, :].astype(BF16), False)
        if rope is None:
            o_ref[...] = acc.astype(out_dtype)
            return
        rot = functools.reduce(jnp.logical_or, [j == c for c in rope_blocks])

        @pl.when(rot)
        def _():
            c, s = cos_ref[...], sin_ref[...]
            for h in range(tn // LANE):
                seg = acc[:, h * LANE:(h + 1) * LANE]
                o_ref[:, h * LANE:(h + 1) * LANE] = (seg * c + pltpu.roll(seg, LANE // 2, 1) * s).astype(out_dtype)

        @pl.when(jnp.logical_not(rot))
        def _():
            o_ref[...] = acc.astype(out_dtype)

    return pl.pallas_call(
        kern,
        grid=(M // tm, n_out // tn),
        in_specs=in_specs,
        out_specs=pl.BlockSpec((tm, tn), lambda i, j: (i, j)),
        out_shape=jax.ShapeDtypeStruct((M, n_out), out_dtype),
        scratch_shapes=scratch,
        compiler_params=_cp(("parallel", "arbitrary")),
        name="matmul",
    )(*args)


def _token_major(v):
    return v.reshape(v.shape[0], v.shape[1] // LANE, LANE)


def add_ln(x, f, ln_g, ln_b, idx, alpha, tm=256, token_major=False):
    M, D = x.shape
    tm = min(tm, M)

    def kern(x_ref, f_ref, g_ref, b_ref, o_ref, ob_ref, *o3_ref):
        y = _layer_norm(alpha * x_ref[...] + f_ref[...], g_ref[...], b_ref[...])
        o_ref[...] = y
        ob_ref[...] = y.astype(BF16)
        if token_major:
            o3_ref[0][...] = _token_major(y)

    row = pl.BlockSpec((tm, D), lambda i: (i, 0))
    par = pl.BlockSpec((None, 1, D), lambda i: (idx, 0, 0))
    out_specs = [row, row]
    out_shape = [jax.ShapeDtypeStruct((M, D), F32), jax.ShapeDtypeStruct((M, D), BF16)]
    if token_major:
        out_specs.append(pl.BlockSpec((tm, D // LANE, LANE), lambda i: (i, 0, 0)))
        out_shape.append(jax.ShapeDtypeStruct((M, D // LANE, LANE), F32))
    return pl.pallas_call(
        kern, grid=(M // tm,), in_specs=[row, row, par, par], out_specs=out_specs,
        out_shape=out_shape, compiler_params=_cp(("parallel",)), name="add_ln",
    )(x, f, ln_g, ln_b)


CMP_PAGES = 8


def compress_ab(pages, page_ids, cmp_w1, cmp_pe, e, n_batch, pages_per_seq):
    P = CMP_PAGES
    assert pages_per_seq % P == 0
    cpp = PAGE_SIZE // CMP_STRIDE
    steps = pages_per_seq // P
    d, G = HEAD_DIM, NSA_G
    n_chunks = pages_per_seq * cpp
    rows = P * cpp * G

    def kern(pid_ref, *refs):
        page_refs, w_ref, pe_ref, o_ref = refs[:P], refs[P], refs[P + 1], refs[P + 2]
        for s in range(2):
            acc_a = jnp.zeros((rows, d), F32)
            acc_b = jnp.zeros((rows, d), F32)

            def chunk_rows(l):
                return jnp.concatenate(
                    [page_refs[i][pl.ds(l, cpp, stride=CMP_STRIDE), s].reshape(cpp * G, d) for i in range(P)],
                    axis=0)

            def pair(l, off):
                lhs = jnp.concatenate([rows2[0] + pe_ref[s, off + l:off + l + 1, :],
                                       rows2[1] + pe_ref[s, off + l + 1:off + l + 2, :]], axis=1)
                return _dot(lhs, w_ref[s, off + l:off + l + 2].reshape(2 * d, d))

            for l in range(0, CMP_STRIDE, 2):
                rows2 = (chunk_rows(l), chunk_rows(l + 1))
                acc_a = acc_a + pair(l, 0)
                acc_b = acc_b + pair(l, CMP_STRIDE)
            o_ref[:, s * 2 * d:s * 2 * d + d] = acc_a
            o_ref[:, s * 2 * d + d:(s + 1) * 2 * d] = acc_b

    def page_spec(i):
        return pl.BlockSpec((None, PAGE_SIZE, 2, G, d),
                            lambda b, st, pid: (pid[(b * steps + st) * P + i], 0, 0, 0, 0))

    gs = pltpu.PrefetchScalarGridSpec(
        num_scalar_prefetch=1,
        grid=(n_batch, steps),
        in_specs=[page_spec(i) for i in range(P)]
        + [pl.BlockSpec((None, 2, CMP_BLOCK, d, d), lambda b, st, pid: (e, 0, 0, 0, 0)),
           pl.BlockSpec((None, 2, CMP_BLOCK, d), lambda b, st, pid: (e, 0, 0, 0))],
        out_specs=pl.BlockSpec((None, rows, 2 * 2 * d), lambda b, st, pid: (b, st, 0)),
    )
    return pl.pallas_call(
        kern, grid_spec=gs,
        out_shape=jax.ShapeDtypeStruct((n_batch, n_chunks * G, 2 * 2 * d), F32),
        compiler_params=_cp(("parallel", "parallel")), name="compress_ab",
    )(page_ids, *([pages] * P), cmp_w1, cmp_pe)


def compress_fin(ab, cmp_w2, e):
    nb, nrow, _ = ab.shape
    d, G = HEAD_DIM, NSA_G
    nch = nrow // G

    def kern(ab_ref, w2_ref, o_ref):
        h = ab_ref[:, 0:d] + pltpu.roll(ab_ref[:, d:2 * d], nrow - G, 0)
        res = _dot(_gelu(h), w2_ref[...]).reshape(nch, G, d)
        for g in range(G):
            o_ref[g] = res[:, g, :]

    return pl.pallas_call(
        kern, grid=(nb, 2),
        in_specs=[pl.BlockSpec((None, nrow, 2 * d), lambda b, s: (b, 0, s)),
                  pl.BlockSpec((None, None, d, d), lambda b, s: (e, s, 0, 0))],
        out_specs=pl.BlockSpec((None, None, NSA_G, nch, d), lambda b, s: (s, b, 0, 0, 0)),
        out_shape=jax.ShapeDtypeStruct((2, nb, NSA_G, nch, d), F32),
        compiler_params=_cp(("parallel", "parallel")), name="compress_fin",
    )(ab, cmp_w2)


def _overlap_matrix(n_rows, n_cols):
    ci = jnp.arange(n_rows)[:, None] * CMP_STRIDE
    sj = jnp.arange(n_cols)[None, :] * SEL_BLOCK
    return ((ci < sj + SEL_BLOCK) & (ci + CMP_BLOCK > sj)).astype(BF16)


NSA_TQ = 128


def nsa_prompt(q, kv, win, kvcmp, gates, B, T):
    d, G, R, tq = HEAD_DIM, NSA_G, NSA_R, NSA_TQ
    nq = T // tq
    ncp = kvcmp.shape[3]
    n_cmp = T // CMP_STRIDE - 1
    n_sel = -(-T // SEL_BLOCK)
    n_top = min(SEL_TOP_N, n_sel)
    wk = min(T, WINDOW + tq)
    key_span = min(T, 4 * tq)
    assert n_sel <= LANE and T % tq == 0 and T % key_span == 0
    scale = d ** -0.5
    overlap = _overlap_matrix(ncp, LANE)
    expand = (jnp.arange(LANE)[:, None] == (jnp.arange(T) // SEL_BLOCK)[None, :]).astype(BF16)

    def kern(q_ref, ks_ref, vs_ref, kw_ref, vw_ref, kc_ref, vc_ref, g_ref, ov_ref, ex_ref, o_ref,
             ksb, vsb, kwb, vwb, kcb, vcb):
        qi = pl.program_id(2)

        @pl.when(qi == 0)
        def _():
            ksb[...] = ks_ref[...].astype(BF16)
            vsb[...] = vs_ref[...].astype(BF16)
            kwb[...] = kw_ref[...].astype(BF16)
            vwb[...] = vw_ref[...].astype(BF16)
            kcb[...] = kc_ref[...].astype(BF16)
            vcb[...] = vc_ref[...].astype(BF16)

        t0 = qi * tq
        qb = q_ref[...]
        qs = jnp.concatenate([qb[:, r * d:(r + 1) * d] for r in range(R)], axis=0).astype(BF16)
        qpos_s = t0 + (lax.broadcasted_iota(I32, (R * tq, 1), 0) & (tq - 1))

        s = _mxu(qs, kcb[...], True) * scale
        n_idx = lax.broadcasted_iota(I32, (1, ncp), 1)
        cmask = (n_idx * CMP_STRIDE + (CMP_BLOCK - 1) <= qpos_s) & (n_idx < n_cmp)
        p = _msoftmax(s, cmask)
        pb = p.astype(BF16)
        o_cmp = _mxu(pb, vcb[...], False)
        imp_r = _mxu(pb, ov_ref[...], False)
        imp = imp_r[0:tq]
        for r in range(1, R):
            imp = imp + imp_r[r * tq:(r + 1) * tq]

        blk = lax.broadcasted_iota(I32, (tq, LANE), 1)
        qpos = t0 + lax.broadcasted_iota(I32, (tq, LANE), 0)
        cur = qpos // SEL_BLOCK
        valid = blk <= cur
        forced = (blk == 0) | (blk == cur) | (blk == cur - 1)
        score = jnp.where(forced, FORCE_SCORE, jnp.where(valid, imp, NEG))
        cnt = jnp.zeros((tq, LANE), I32)
        for i in range(n_sel):
            col = score[:, i:i + 1]
            beats = (col > score) | ((col == score) & (blk > i))
            cnt = cnt + beats.astype(I32)
        sel = ((cnt < n_top) & valid).astype(BF16)

        kstart = pl.multiple_of(jnp.clip(t0 - WINDOW, 0, T - wk), tq)
        kwin = kwb[pl.ds(kstart, wk), :]
        vwin = vwb[pl.ds(kstart, wk), :]
        s_w = _mxu(qs, kwin, True) * scale
        kp = kstart + lax.broadcasted_iota(I32, (1, wk), 1)
        wmask = (kp <= qpos_s) & (kp > qpos_s - WINDOW)
        p_w = _msoftmax_nonempty(s_w, wmask)
        o_win = _mxu(p_w.astype(BF16), vwin, False)

        gt = _sigmoid(g_ref[...])

        def selected(nkeys):
            selx = _mxu(sel, ex_ref[:, 0:nkeys], False)
            kpos = lax.broadcasted_iota(I32, (tq, nkeys), 1)
            qpos_t = t0 + lax.broadcasted_iota(I32, (tq, nkeys), 0)
            smask = (selx > 0.5) & (kpos <= qpos_t)
            for r in range(R):
                s_r = _mxu(qs[r * tq:(r + 1) * tq], ksb[0:nkeys, :], True) * scale
                p_r = _msoftmax_nonempty(s_r, smask)
                o_slc = _mxu(p_r.astype(BF16), vsb[0:nkeys, :], False)
                o_r = (gt[:, 3 * r:3 * r + 1] * o_cmp[r * tq:(r + 1) * tq]
                       + gt[:, 3 * r + 1:3 * r + 2] * o_slc
                       + gt[:, 3 * r + 2:3 * r + 3] * o_win[r * tq:(r + 1) * tq])
                o_ref[:, r * d:(r + 1) * d] = o_r.astype(BF16)

        n_span = pl.cdiv(t0 + tq, key_span)
        for n in range(1, T // key_span + 1):
            pl.when(n_span == n)(functools.partial(selected, n * key_span))

    seq = lambda blkcol: pl.BlockSpec((T, d), lambda b, g, qi: (b, blkcol(g)))
    cmp_spec = lambda s: pl.BlockSpec((None, None, None, ncp, d), lambda b, g, qi: (s, b, g, 0, 0))
    return pl.pallas_call(
        kern, grid=(B, G, nq),
        in_specs=[pl.BlockSpec((tq, R * d), lambda b, g, qi: (b * nq + qi, g)),
                  seq(lambda g: 2 * G + g), seq(lambda g: 3 * G + g),
                  seq(lambda g: g), seq(lambda g: G + g),
                  cmp_spec(0), cmp_spec(1),
                  pl.BlockSpec((tq, LANE), lambda b, g, qi: (b * nq + qi, g)),
                  pl.BlockSpec((ncp, LANE), lambda b, g, qi: (0, 0)),
                  pl.BlockSpec((LANE, T), lambda b, g, qi: (0, 0))],
        out_specs=pl.BlockSpec((tq, R * d), lambda b, g, qi: (b * nq + qi, g)),
        out_shape=jax.ShapeDtypeStruct((B * T, G * R * d), BF16),
        scratch_shapes=[pltpu.VMEM((T, d), BF16)] * 4 + [pltpu.VMEM((ncp, d), BF16)] * 2,
        compiler_params=_cp(("parallel", "parallel", "arbitrary")), name="nsa_prompt",
    )(q, kv, kv, win, win, kvcmp, kvcmp, gates, overlap, expand)


def nsa_sample_cmp(q8, kvcmp, q_pos):
    B = q8.shape[0]
    d, G, R = HEAD_DIM, NSA_G, NSA_R
    nch = kvcmp.shape[3]
    n_cmp = nch - 1
    n_sel = -(-(q_pos + 1) // SEL_BLOCK)
    nsp = -(-n_sel // LANE) * LANE
    overlap = _overlap_matrix(nch, nsp)
    scale = d ** -0.5

    def kern(q_ref, k_ref, v_ref, ov_ref, o_ref, imp_ref):
        n_idx = lax.broadcasted_iota(I32, (1, nch), 1)
        cmask = (n_idx * CMP_STRIDE + (CMP_BLOCK - 1) <= q_pos) & (n_idx < n_cmp)
        rows = []
        for g in range(G):
            s = _dot(q_ref[g], k_ref[g], nt=True) * scale
            pb = _msoftmax(s, cmask).astype(BF16)
            o_ref[g] = _dot(pb, v_ref[g])
            rows.append(jnp.sum(_mxu(pb, ov_ref[...], False)[0:R], axis=0, keepdims=True))
        imp_ref[...] = jnp.concatenate(rows + [jnp.zeros((SUBLANE - G, nsp), F32)], axis=0)

    qspec = pl.BlockSpec((None, G, SUBLANE, d), lambda b: (b, 0, 0, 0))
    cspec = lambda s: pl.BlockSpec((None, None, G, nch, d), lambda b: (s, b, 0, 0, 0))
    return pl.pallas_call(
        kern, grid=(B,),
        in_specs=[qspec, cspec(0), cspec(1), pl.BlockSpec((nch, nsp), lambda b: (0, 0))],
        out_specs=[qspec, pl.BlockSpec((None, SUBLANE, nsp), lambda b: (b, 0, 0))],
        out_shape=[jax.ShapeDtypeStruct((B, G, SUBLANE, d), F32),
                   jax.ShapeDtypeStruct((B, SUBLANE, nsp), F32)],
        compiler_params=_cp(("parallel",)), name="nsa_sample_cmp",
    )(q8, kvcmp, kvcmp, overlap)


def nsa_sample_topk(imp, q_pos):
    nr, nsp = imp.shape
    assert nr == LANE
    n_sel = -(-(q_pos + 1) // SEL_BLOCK)
    n_top = min(SEL_TOP_N, n_sel)
    cur = q_pos // SEL_BLOCK

    def kern(imp_ref, o_ref, sct):
        blk = lax.broadcasted_iota(I32, (nr, nsp), 1)
        valid = blk <= cur
        forced = (blk == 0) | (blk == cur) | (blk == cur - 1)
        score = jnp.where(forced, FORCE_SCORE, jnp.where(valid, imp_ref[...], NEG))
        eye = (lax.broadcasted_iota(I32, (nsp, nsp), 0) == lax.broadcasted_iota(I32, (nsp, nsp), 1)).astype(BF16)
        h, m, l = _split3(score)
        sct[...] = (_mxu(eye, h, True) + _mxu(eye, m, True)) + _mxu(eye, l, True)
        sc = sct[...]
        jidx = lax.broadcasted_iota(I32, (nsp, nr), 0)

        def body(i, cnt):
            row = sct[pl.ds(i, 1), :]
            beats = (row > sc) | ((row == sc) & (jidx > i))
            return cnt + beats.astype(I32)

        cnt = lax.fori_loop(0, n_sel, body, jnp.zeros((nsp, nr), I32))
        for p in range(n_top):
            o_ref[p:p + 1, :] = jnp.sum(jnp.where(cnt == p, jidx, 0), axis=0, keepdims=True)

    return pl.pallas_call(
        kern, grid=(1,),
        in_specs=[pl.BlockSpec((nr, nsp), lambda i: (0, 0))],
        out_specs=pl.BlockSpec((n_top, nr), lambda i: (0, 0)),
        out_shape=jax.ShapeDtypeStruct((n_top, nr), I32),
        scratch_shapes=[pltpu.VMEM((nsp, nr), F32)],
        compiler_params=_cp(("arbitrary",)), name="nsa_sample_topk",
    )(imp)


def nsa_sample_attend(q8, o_cmp, gates, idx, page_ids, pages, new_kv, cwin, new_win, e, q_pos):
    B = q8.shape[0]
    d, G, R = HEAD_DIM, NSA_G, NSA_R
    n_top = idx.shape[0] // (B * G)
    n_pages = page_ids.shape[0] // B
    n_cache_blocks = n_pages * (PAGE_SIZE // SEL_BLOCK)
    bpp = PAGE_SIZE // SEL_BLOCK
    w0 = cwin.shape[2]
    scale = d ** -0.5
    nk = n_top * SEL_BLOCK

    def kern(idx_ref, pid_ref, q_ref, oc_ref, g_ref, pages_ref, nkv_ref, cw_ref, nw_ref, o_ref,
             kvbuf, sem):
        b = pl.program_id(0)

        def slot_copy(slot, j):
            page = pid_ref[b * n_pages + j // bpp]
            r0 = (j % bpp) * SEL_BLOCK
            return pltpu.make_async_copy(
                pages_ref.at[page, pl.ds(r0, SEL_BLOCK), pl.ds(2, 2)], kvbuf.at[slot], sem.at[0])

        for slot in range(G * n_top):
            j = idx_ref[b * G * n_top + slot]

            @pl.when(j < n_cache_blocks)
            def _():
                slot_copy(slot, j).start()

            @pl.when(j >= n_cache_blocks)
            def _():
                kvbuf[slot] = jnp.zeros((SEL_BLOCK, 2, G, d), F32)

            @pl.when(j * SEL_BLOCK == q_pos)
            def _():
                kvbuf[slot, 0] = nkv_ref[2:4]

        for slot in range(G * n_top):
            j = idx_ref[b * G * n_top + slot]

            @pl.when(j < n_cache_blocks)
            def _():
                slot_copy(slot, j).wait()

        lane_k = lax.broadcasted_iota(I32, (1, nk), 1)
        row8 = lax.broadcasted_iota(I32, (SUBLANE, 1), 0)
        for g in range(G):
            qg = q_ref[g]
            kk = jnp.concatenate([kvbuf[g * n_top + p, :, 0, g, :] for p in range(n_top)], axis=0)
            vv = jnp.concatenate([kvbuf[g * n_top + p, :, 1, g, :] for p in range(n_top)], axis=0)
            kpos = lane_k & (SEL_BLOCK - 1)
            for p in range(n_top):
                j = idx_ref[(b * G + g) * n_top + p]
                kpos = kpos + jnp.where(lane_k // SEL_BLOCK == p, j * SEL_BLOCK, 0)
            s = _dot(qg, kk, nt=True) * scale
            p_s = _msoftmax(s, kpos <= q_pos)
            o_slc = _dot(p_s, vv)
            kw = cw_ref[:, 0, g, :]
            vw = cw_ref[:, 1, g, :]
            s_w = _dot(qg, kw, nt=True) * scale
            wpos = (q_pos - w0) + lax.broadcasted_iota(I32, (1, w0), 1)
            wmask = (wpos >= 0) & (wpos <= q_pos) & (wpos > q_pos - WINDOW)
            s_w = jnp.where(wmask, s_w, NEG)
            s_n = jnp.sum(_bf16r(qg) * _bf16r(nw_ref[0, g:g + 1, :]), axis=-1, keepdims=True) * scale
            m = jnp.maximum(jnp.max(s_w, axis=-1, keepdims=True), s_n)
            p_w = jnp.where(wmask, jnp.exp(s_w - m), 0.0)
            p_n = jnp.exp(s_n - m)
            den = jnp.sum(p_w, axis=-1, keepdims=True) + p_n
            o_win = _dot(p_w / den, vw) + _bf16r(p_n / den) * _bf16r(nw_ref[1, g:g + 1, :])
            gt = _sigmoid(g_ref[g])
            gc = [jnp.zeros((SUBLANE, 1), F32)] * 3
            for r in range(R):
                for c in range(3):
                    gc[c] = jnp.where(row8 == r, gt[:, 3 * r + c:3 * r + c + 1], gc[c])
            o_ref[g] = gc[0] * oc_ref[g] + gc[1] * o_slc + gc[2] * o_win

    qspec = pl.BlockSpec((None, G, SUBLANE, d), lambda b, i, p: (b, 0, 0, 0))
    gs = pltpu.PrefetchScalarGridSpec(
        num_scalar_prefetch=2, grid=(B,),
        in_specs=[qspec, qspec,
                  pl.BlockSpec((None, G, 1, LANE), lambda b, i, p: (b, 0, 0, 0)),
                  pl.BlockSpec(memory_space=pl.ANY),
                  pl.BlockSpec((None, N_KV_STREAMS, G, d), lambda b, i, p: (b, 0, 0, 0)),
                  pl.BlockSpec((None, None, w0, 2, G, d), lambda b, i, p: (e, b, 0, 0, 0, 0)),
                  pl.BlockSpec((None, 2, G, d), lambda b, i, p: (b, 0, 0, 0))],
        out_specs=qspec,
        scratch_shapes=[pltpu.VMEM((G * n_top, SEL_BLOCK, 2, G, d), F32),
                        pltpu.SemaphoreType.DMA((1,))],
    )
    return pl.pallas_call(
        kern, grid_spec=gs, out_shape=jax.ShapeDtypeStruct((B, G, SUBLANE, d), F32),
        compiler_params=_cp(("arbitrary",)), name="nsa_sample_attend",
    )(idx, page_ids, q8, o_cmp, gates, pages, new_kv, cwin, new_win)


def rglru(u, gb, conv0, h0, conv_w, conv_b, lru_wa, lru_ba, lru_wx, lru_bx, lru_lambda, e, B, T):
    W = u.shape[1]
    cw = RNN_BLOCK_W
    nblk = W // cw
    tc = min(T, 256)
    nt = T // tc
    assert T % tc == 0 and tc % SUBLANE == 0

    def kern(u_ref, gb_ref, c0_ref, h0_ref, cw_ref, cb_ref, wa_ref, ba_ref, wx_ref, bx_ref, lam_ref,
             y_ref, h_ref, prev, hprev):
        ti = pl.program_id(2)

        @pl.when(ti == 0)
        def _():
            prev[...] = c0_ref[...]
            hprev[...] = h0_ref[...]

        uu = u_ref[...]
        ext = _bf16r(jnp.concatenate([prev[...], uu], axis=0))
        prev[...] = uu[tc - SUBLANE:tc]
        cwr = _bf16r(cw_ref[...])
        xc = cwr[0:1, :] * ext[5:5 + tc]
        for j in range(1, CONV_WIDTH):
            xc = xc + cwr[j:j + 1, :] * ext[5 + j:5 + j + tc]
        xc = xc + cb_ref[...]
        r = _sigmoid(_dot(xc, wa_ref[...]) + ba_ref[...])
        i = _sigmoid(_dot(xc, wx_ref[...]) + bx_ref[...])
        nl = -lam_ref[...]
        softplus = jnp.maximum(nl, 0.0) + jnp.log1p(jnp.exp(-jnp.abs(nl)))
        log_a = -LRU_C * r * softplus
        a = jnp.exp(log_a)
        bt = jnp.sqrt(-jnp.tanh(log_a) * (a * a + 1.0)) * (i * xc)
        rows = lax.broadcasted_iota(I32, (tc, 1), 0)
        step = 1
        while step < tc:
            keep = rows >= step
            bt = jnp.where(keep, a * pltpu.roll(bt, step, 0) + bt, bt)
            a = jnp.where(keep, a * pltpu.roll(a, step, 0), a)
            step *= 2
        h = a * hprev[...] + bt
        hprev[...] = h[tc - 1:tc]
        h_ref[...] = h
        y_ref[...] = (h * _gelu(gb_ref[...])).astype(BF16)

    tile = pl.BlockSpec((tc, cw), lambda b, c, t: (b * nt + t, c))
    vec = lambda: pl.BlockSpec((None, 1, cw), lambda b, c, t: (e, 0, c))
    mat = lambda: pl.BlockSpec((None, None, cw, cw), lambda b, c, t: (e, c, 0, 0))
    bvec = lambda: pl.BlockSpec((None, None, 1, cw), lambda b, c, t: (e, c, 0, 0))
    return pl.pallas_call(
        kern, grid=(B, nblk, nt),
        in_specs=[tile, tile,
                  pl.BlockSpec((None, SUBLANE, cw), lambda b, c, t: (b, 0, c)),
                  pl.BlockSpec((None, 1, cw), lambda b, c, t: (b, 0, c)),
                  pl.BlockSpec((None, CONV_WIDTH, cw), lambda b, c, t: (e, 0, c)),
                  vec(), mat(), bvec(), mat(), bvec(), vec()],
        out_specs=[tile, tile],
        out_shape=[jax.ShapeDtypeStruct((B * T, W), BF16), jax.ShapeDtypeStruct((B * T, W), F32)],
        scratch_shapes=[pltpu.VMEM((SUBLANE, cw), F32), pltpu.VMEM((1, cw), F32)],
        compiler_params=_cp(("parallel", "parallel", "arbitrary")), name="rglru",
    )(u, gb, conv0, h0, conv_w, conv_b[:, None, :], lru_wa, lru_ba[:, :, None, :], lru_wx,
      lru_bx[:, :, None, :], lru_lambda[:, None, :])


def pool_mixer(x, buf16, pool_w, pool_scale, o, B, T, t0):
    D = x.shape[1]
    ng = len(POOL_WINDOWS)
    gw = D // ng
    tt = min(T, 256)
    nt = T // tt
    halo = 2 * SUBLANE
    assert T % tt == 0 and tt >= halo and POOL_WINDOWS == (2, 4, 8, 16)

    def kern(x_ref, buf_ref, w_ref, sc_ref, o_ref, prev):
        g, ti = pl.program_id(0), pl.program_id(2)

        @pl.when(ti == 0)
        def _():
            prev[...] = buf_ref[...]

        xx = x_ref[...]
        ext = jnp.concatenate([prev[...], xx], axis=0)
        prev[...] = xx[tt - halo:tt]
        s2 = ext + pltpu.roll(ext, 1, 0)
        s4 = s2 + pltpu.roll(s2, 2, 0)
        s8 = s4 + pltpu.roll(s4, 4, 0)
        s16 = s8 + pltpu.roll(s8, 8, 0)
        sw = jnp.where(g == 0, s2, jnp.where(g == 1, s4, jnp.where(g == 2, s8, s16)))[halo:]
        wlen = jnp.left_shift(2, g)
        pos = t0 + ti * tt + lax.broadcasted_iota(I32, (tt, 1), 0)
        cnt = jnp.minimum(wlen, pos + 1).astype(F32)
        pooled = sw / cnt - xx
        o_ref[...] = _dot(pooled, w_ref[...]) * sc_ref[...]

    return pl.pallas_call(
        kern, grid=(ng, B, nt),
        in_specs=[pl.BlockSpec((tt, gw), lambda g, b, t: (b * nt + t, g)),
                  pl.BlockSpec((None, halo, gw), lambda g, b, t: (b, 0, g)),
                  pl.BlockSpec((None, None, gw, gw), lambda g, b, t: (o, g, 0, 0)),
                  pl.BlockSpec((None, 1, gw), lambda g, b, t: (o, 0, g))],
        out_specs=pl.BlockSpec((tt, gw), lambda g, b, t: (b * nt + t, g)),
        out_shape=jax.ShapeDtypeStruct((B * T, D), F32),
        scratch_shapes=[pltpu.VMEM((halo, gw), F32)],
        compiler_params=_cp(("parallel", "parallel", "arbitrary")), name="pool_mixer",
    )(x, buf16, pool_w, pool_scale[:, None, :])


def mem_attn(q, mkv, B, T, kv_off=0):
    MW = q.shape[1]
    M = mkv.shape[1]
    dh = MW // MEM_HEADS
    tq = min(T, 256)
    nq = T // tq
    scale = dh ** -0.5

    def kern(q_ref, kv_ref, o_ref):
        for h in range(MEM_HEADS):
            sl = slice(h * dh, (h + 1) * dh)
            s = _dot(q_ref[:, sl], kv_ref[:, 0, sl], nt=True) * scale
            m = jnp.max(s, axis=-1, keepdims=True)
            p = jnp.exp(s - m)
            p = p / jnp.sum(p, axis=-1, keepdims=True)
            o_ref[:, sl] = _dot(p, kv_ref[:, 1, sl]).astype(BF16)

    return pl.pallas_call(
        kern, grid=(B, nq),
        in_specs=[pl.BlockSpec((tq, MW), lambda b, i: (b * nq + i, 0)),
                  pl.BlockSpec((None, M, 2, MW), lambda b, i: (kv_off + b, 0, 0, 0))],
        out_specs=pl.BlockSpec((tq, MW), lambda b, i: (b * nq + i, 0)),
        out_shape=jax.ShapeDtypeStruct((B * T, MW), BF16),
        compiler_params=_cp(("parallel", "parallel")), name="mem_attn",
    )(q, mkv)


ROUTER_ROWS = 40


def moe_router(x, w_t, bias, tm=512):
    N, D = x.shape
    tm = min(tm, N)
    assert N % tm == 0 and tm % LANE == 0

    def kern(x_ref, w_ref, b_ref, e_ref, g_ref):
        lt = _dot(w_ref[...], x_ref[...], nt=True) + b_ref[...]
        c = [lt[i:i + 1] for i in range(MOE_GROUPS)]
        m = functools.reduce(jnp.maximum, c)
        ex = [jnp.exp(ci - m) for ci in c]
        tot = functools.reduce(jnp.add, ex)
        pc = [ei / tot for ei in ex]
        pg = functools.reduce(jnp.maximum, pc)
        grp = jnp.full(pg.shape, MOE_GROUPS - 1, I32)
        for i in range(MOE_GROUPS - 2, -1, -1):
            grp = jnp.where(pc[i] == pg, i, grp)
        lf = []
        for j in range(MOE_EPG):
            v = lt[MOE_GROUPS + j:MOE_GROUPS + j + 1]
            for gi in range(1, MOE_GROUPS):
                r0 = MOE_GROUPS + gi * MOE_EPG + j
                v = jnp.where(grp == gi, lt[r0:r0 + 1], v)
            lf.append(v)
        m = functools.reduce(jnp.maximum, lf)
        ex = [jnp.exp(v - m) for v in lf]
        tot = functools.reduce(jnp.add, ex)
        pf = [ei / tot for ei in ex]

        def first_max(vals):
            best = functools.reduce(jnp.maximum, vals)
            arg = jnp.full(best.shape, len(vals) - 1, I32)
            for j in range(len(vals) - 2, -1, -1):
                arg = jnp.where(vals[j] == best, j, arg)
            return best, arg

        p1, j1 = first_max(pf)
        p2, j2 = first_max([jnp.where(j1 == j, -1.0, pf[j]) for j in range(MOE_EPG)])
        den = p1 + p2
        e_ref[0:1, :] = grp * MOE_EPG + j1
        e_ref[1:2, :] = grp * MOE_EPG + j2
        g_ref[0:1, :] = pg * p1 / den
        g_ref[1:2, :] = pg * p2 / den

    out = pl.BlockSpec((MOE_TOP_K, tm), lambda i: (0, i))
    return pl.pallas_call(
        kern, grid=(N // tm,),
        in_specs=[pl.BlockSpec((tm, D), lambda i: (i, 0)),
                  pl.BlockSpec((ROUTER_ROWS, D), lambda i: (0, 0)),
                  pl.BlockSpec((ROUTER_ROWS, 1), lambda i: (0, 0))],
        out_specs=[out, out],
        out_shape=[jax.ShapeDtypeStruct((MOE_TOP_K, N), I32), jax.ShapeDtypeStruct((MOE_TOP_K, N), F32)],
        compiler_params=_cp(("parallel",)), name="moe_router",
    )(x, w_t, bias)


MOE_SUB = 2
MOE_TF = 256
MOE_VMEM = 60 * 1024 * 1024


def moe_experts_grouped(x3, tok, aid, nvalid, gexp, nsub, nused, w_gate, w_up, w_down, layer, rows, n_assign):
    _, nc, _ = x3.shape
    D = nc * LANE
    SB, tf = MOE_SUB, MOE_TF
    n_groups = tok.shape[0] // (SB * rows)
    nf = w_gate.shape[3] // tf
    assert nf >= 2 and SB == 2

    def kern(tok_ref, aid_ref, nv_ref, gexp_ref, nsub_ref, nused_ref, x3_ref, wg_ref, wu_ref, wd_ref, y_ref,
             xbuf, xb, acc, ybuf, sem_in, sem_out):
        g, f = pl.program_id(0), pl.program_id(1)
        nused = nused_ref[0]
        active = g < nused
        ns = nsub_ref[g]

        def gather_start(grp, j):
            base = (grp * SB + j) * rows

            def body(r, carry):
                pltpu.make_async_copy(x3_ref.at[tok_ref[base + r]], xbuf.at[r], sem_in.at[0]).start()
                return carry
            lax.fori_loop(0, rows, body, 0, unroll=SUBLANE)

        def gather_wait():
            pltpu.make_async_copy(x3_ref.at[pl.ds(0, rows)], xbuf, sem_in.at[0]).wait()

        def scatter_start(grp, j):
            base = (grp * SB + j) * rows

            def body(r, carry):
                pltpu.make_async_copy(ybuf.at[j, r], y_ref.at[aid_ref[base + r]], sem_out.at[j]).start()
                return carry
            lax.fori_loop(0, nv_ref[grp * SB + j], body, 0)

        def scatter_wait(grp, j):
            nv = nv_ref[grp * SB + j]

            @pl.when(nv > 0)
            def _():
                pltpu.make_async_copy(ybuf.at[j, pl.ds(0, nv)], y_ref.at[pl.ds(0, nv)], sem_out.at[j]).wait()

        @pl.when(active & (f == 0))
        def _():
            pl.when(g == 0)(lambda: gather_start(0, 0))
            for j in range(SB):
                @pl.when(j < ns)
                def _():
                    if j > 0:
                        gather_start(g, j)
                    gather_wait()
                    xb[j * rows:(j + 1) * rows, :] = xbuf[...].reshape(rows, D).astype(BF16)
            acc[...] = jnp.zeros_like(acc)

        pl.when((f == nf - 2) & (g + 1 < nused))(lambda: gather_start(g + 1, 0))

        @pl.when(active)
        def _():
            wg, wu, wd = wg_ref[...].astype(BF16), wu_ref[...].astype(BF16), wd_ref[...].astype(BF16)
            for j in range(SB):
                @pl.when(j < ns)
                def _():
                    x = xb[j * rows:(j + 1) * rows, :]
                    hg = _mxu(x, wg, False)
                    hu = _mxu(x, wu, False)
                    hid = (hg * _sigmoid(hg) * hu).astype(BF16)
                    acc[j * rows:(j + 1) * rows, :] += _mxu(hid, wd, False)

        @pl.when(active & (f == nf - 1))
        def _():
            for j in range(SB):
                pl.when((g >= 1) & (j < nsub_ref[jnp.maximum(g - 1, 0)]))(
                    functools.partial(scatter_wait, jnp.maximum(g - 1, 0), j))

                @pl.when(j < ns)
                def _():
                    ybuf[j] = _token_major(acc[j * rows:(j + 1) * rows, :])
                    scatter_start(g, j)

            @pl.when(g == nused - 1)
            def _():
                for j in range(SB):
                    pl.when(j < ns)(functools.partial(scatter_wait, g, j))

    def live(g, nu):
        return jnp.maximum(jnp.minimum(g, nu[0] - 1), 0)

    def feff(g, f, nu):
        return jnp.where(g < nu[0], f, nf - 1)

    def wspec(shape, col):
        def imap(g, f, t, a, nv, ge, nsb, nu):
            e, ff = ge[live(g, nu)], feff(g, f, nu)
            return (layer, e, 0, ff) if col else (layer, e, ff, 0)
        return pl.BlockSpec(shape, imap)

    gs = pltpu.PrefetchScalarGridSpec(
        num_scalar_prefetch=6, grid=(n_groups, nf),
        in_specs=[pl.BlockSpec(memory_space=pl.ANY),
                  wspec((None, None, D, tf), True), wspec((None, None, D, tf), True),
                  wspec((None, None, tf, D), False)],
        out_specs=pl.BlockSpec(memory_space=pl.ANY),
        scratch_shapes=[pltpu.VMEM((rows, nc, LANE), F32), pltpu.VMEM((SB * rows, D), BF16),
                        pltpu.VMEM((SB * rows, D), F32), pltpu.VMEM((SB, rows, nc, LANE), F32),
                        pltpu.SemaphoreType.DMA((1,)), pltpu.SemaphoreType.DMA((SB,))],
    )
    return pl.pallas_call(
        kern, grid_spec=gs, out_shape=jax.ShapeDtypeStruct((n_assign, nc, LANE), F32),
        compiler_params=_cp(("arbitrary", "arbitrary"), MOE_VMEM), name="moe_experts",
    )(tok, aid, nvalid, gexp, nsub, nused, x3, w_gate, w_up, w_down)


def moe_combine_ln(x, ytok, gate, ln_g, ln_b, idx, alpha, n_tok, tm=128):
    D = x.shape[1]
    nc = D // LANE
    tm = min(tm, n_tok)

    def kern(x_ref, y_ref, gt_ref, g_ref, b_ref, o_ref, ob_ref):
        gt = _bf16r(gt_ref[...])
        mix = sum(gt[:, k:k + 1] * _bf16r(y_ref[:, k].reshape(tm, D)) for k in range(MOE_TOP_K))
        y = _layer_norm(alpha * x_ref[...] + mix, g_ref[...], b_ref[...])
        o_ref[...] = y
        ob_ref[...] = y.astype(BF16)

    row = pl.BlockSpec((tm, D), lambda i: (i, 0))
    par = pl.BlockSpec((None, 1, D), lambda i: (idx, 0, 0))
    return pl.pallas_call(
        kern, grid=(n_tok // tm,),
        in_specs=[row, pl.BlockSpec((tm, MOE_TOP_K, nc, LANE), lambda i: (i, 0, 0, 0)),
                  pl.BlockSpec((tm, MOE_TOP_K), lambda i: (i, 0)), par, par],
        out_specs=[row, row],
        out_shape=[jax.ShapeDtypeStruct((n_tok, D), F32), jax.ShapeDtypeStruct((n_tok, D), BF16)],
        compiler_params=_cp(("parallel",)), name="moe_combine_ln",
    )(x, ytok, gate, ln_g, ln_b)


def hier_moe_ln(x, x3, n_tok, w_rt, b_rt, w_gate, w_up, w_down, layer, ln_g, ln_b, alpha, rows):
    N, D = x.shape
    xr = x if N % LANE == 0 else jnp.pad(x, ((0, LANE - N % LANE), (0, 0)))
    eidx, gate = moe_router(xr, w_rt, b_rt)
    eidx, gate = eidx[:, :n_tok], gate[:, :n_tok]
    A = n_tok * MOE_TOP_K
    e = eidx.T.reshape(A)
    onehot = (e[:, None] == jnp.arange(MOE_EXPERTS, dtype=I32)[None, :]).astype(I32)
    csum = jnp.cumsum(onehot, axis=0)
    counts = csum[-1]
    pos = jnp.sum(csum * onehot, axis=1) - 1
    grows = MOE_SUB * rows
    padded = (counts + grows - 1) // grows * grows
    pad_end = jnp.cumsum(padded)
    pad_start = pad_end - padded
    dest = (jnp.sum(onehot * pad_start[None, :], axis=1) + pos).astype(I32)
    n_groups = min(A // grows + MOE_EXPERTS, A)
    gexp = jnp.minimum(jnp.searchsorted(pad_end, jnp.arange(n_groups, dtype=I32) * grows, side="right"),
                       MOE_EXPERTS - 1).astype(I32)
    nused = (pad_end[-1:] // grows).astype(I32)
    valid_end = (pad_start + counts)[gexp]
    nvalid = jnp.clip(jnp.repeat(valid_end, MOE_SUB) - jnp.arange(n_groups * MOE_SUB, dtype=I32) * rows,
                      0, rows).astype(I32)
    nsub = jnp.sum((nvalid > 0).reshape(n_groups, MOE_SUB), axis=1).astype(I32)
    aid = jnp.zeros((n_groups * grows,), I32).at[dest].set(jnp.arange(A, dtype=I32))
    ytok = moe_experts_grouped(x3, aid // MOE_TOP_K, aid, nvalid, gexp, nsub, nused, w_gate, w_up, w_down,
                               layer, rows, A)
    ytok = ytok.reshape(n_tok, MOE_TOP_K, D // LANE, LANE)
    return moe_combine_ln(x, ytok, gate.T, ln_g, ln_b, layer * 3 + 2, alpha, n_tok)


def _rope_tables(pos):
    half = HEAD_DIM // 2
    inv_freq = ROPE_THETA ** (-jnp.arange(half, dtype=F32) / half)
    ang = pos.astype(F32)[:, None] * inv_freq[None, :]
    cos, sin = jnp.cos(ang), jnp.sin(ang)
    return jnp.concatenate([cos, cos], axis=1), jnp.concatenate([-sin, sin], axis=1)


def _even_projections(x, pos_rows, w_in, w_rest, w_gates, e, tm, q_dtype):
    G, d = NSA_G, HEAD_DIM
    q_cols = G * NSA_R * d
    kv_cols = N_KV_STREAMS * G * d
    cos, sin = _rope_tables(pos_rows)
    tn = G * d
    mm = functools.partial(matmul, x, tm=tm)
    q = mm(w_in, layer=e, col_off=0, n_out=q_cols, tn=tn, rope=(cos, sin, tuple(range(q_cols // tn))),
           out_dtype=q_dtype)
    kv = mm(w_in, layer=e, col_off=q_cols, n_out=kv_cols, tn=tn, rope=(cos, sin, (0, 2)))
    win = mm(w_in, layer=e, col_off=q_cols + kv_cols, n_out=2 * G * d, tn=tn, rope=(cos, sin, (0,)))
    gates = mm(w_gates, tn=G * LANE)
    rw = w_rest.shape[1] // 2
    u = mm(w_rest, n_out=rw)
    gb = mm(w_rest, col_off=rw, n_out=rw)
    return q, kv, win, gates, u, gb


def _pad_rows(a, n):
    return jnp.pad(a, ((0, n - a.shape[0]),) + ((0, 0),) * (a.ndim - 1))


def kernel(x_prompt, x_sample, mem_prompt, cache_nsa_kv, cache_nsa_win, state_rglru_h, state_rglru_conv,
           state_pool, cache_mem_kv, page_table, w_in, w_out, cmp_w1, cmp_w2, cmp_pe, conv_w, conv_b,
           lru_wa, lru_ba, lru_wx, lru_bx, lru_lambda, pool_w, pool_scale, mem_wq, mem_wk, mem_wv, mem_wo,
           ln_g, ln_b, moe_w_coarse, moe_b_coarse, moe_w_fine, moe_b_fine, moe_w_gate, moe_w_up, moe_w_down):
    B, T, D = x_prompt.shape
    DB = x_sample.shape[0]
    depth = ln_g.shape[0]
    n_pages = page_table.shape[1]
    past_len = n_pages * PAGE_SIZE
    n_phys = cache_nsa_kv.shape[1]
    G, R, d = NSA_G, NSA_R, HEAD_DIM
    alpha = (2 * depth) ** 0.25
    q_cols, kv_cols, win_cols, gate_cols = G * R * d, N_KV_STREAMS * G * d, 2 * G * d, 3 * G * R
    rest_off = q_cols + kv_cols + win_cols + gate_cols
    RW = (w_in.shape[2] - rest_off) // 2
    MW = mem_wq.shape[2]
    M = mem_prompt.shape[1]
    SR = SAMPLE_ROWS
    TS = SAMPLE_ROWS

    xp = x_prompt.reshape(B * T, D)
    xpb = xp.astype(BF16)
    xs = _pad_rows(x_sample.reshape(DB, D), SR)
    lng = ln_g.reshape(depth * 3, 1, D)
    lnb = ln_b.reshape(depth * 3, 1, D)
    memf = mem_prompt.reshape(B * M, D)

    kv_p, kv_s, win_p, win_s, h_p, h_s, cv_p, cv_s, pl_p, pl_s, mem_p = ([] for _ in range(11))
    for layer in range(depth):
        if layer % 2 == 0:
            e = layer // 2
            wg = w_in[e, :, q_cols + kv_cols + win_cols:rest_off].reshape(D, G, 3 * R)
            wg = jnp.pad(wg, ((0, 0), (0, 0), (0, LANE - 3 * R))).reshape(D, G * LANE)
            w_rest = w_in[e, :, rest_off:]

            q, kv, win, gates, u, gb = _even_projections(
                xpb, jnp.arange(T), w_in, w_rest, wg, e, 1024, BF16)
            pages_p = kv.reshape(B * T // PAGE_SIZE, PAGE_SIZE, N_KV_STREAMS, G, d)
            new_kv = pages_p.reshape(B, T, N_KV_STREAMS, G, d)
            ab = compress_ab(pages_p, jnp.arange(B * T // PAGE_SIZE, dtype=I32), cmp_w1, cmp_pe, e, B,
                             T // PAGE_SIZE)
            kvcmp = compress_fin(ab, cmp_w2, e)
            o_nsa = nsa_prompt(q, kv, win, kvcmp, gates, B, T)
            y_rnn, h_all = rglru(u, gb, jnp.zeros((B, SUBLANE, RW), F32), jnp.zeros((B, 1, RW), F32),
                                 conv_w, conv_b, lru_wa, lru_ba, lru_wx, lru_bx, lru_lambda, e, B, T)
            mix_p = matmul(o_nsa, w_out, layer=e, x2=y_rnn)
            keep = min(WINDOW, T)
            kv_p.append(new_kv)
            win_p.append(win.reshape(B, T, 2, G, d)[:, T - keep:])
            h_p.append(h_all.reshape(B, T, RW)[:, T - 1])
            cv_p.append(u.reshape(B, T, RW)[:, T - (CONV_WIDTH - 1):])

            q, kv, win, gates, u, gb = _even_projections(
                xs, jnp.full((SR,), past_len), w_in, w_rest, wg, e, SR, F32)
            new_kv_s = kv[:DB].reshape(DB, N_KV_STREAMS, G, d)
            new_win_s = win[:DB].reshape(DB, 2, G, d)
            pages_s = cache_nsa_kv.reshape(cache_nsa_kv.shape[0] * n_phys, PAGE_SIZE, N_KV_STREAMS, G, d)
            pid = (page_table.reshape(DB * n_pages) + e * n_phys).astype(I32)
            ab = compress_ab(pages_s, pid, cmp_w1, cmp_pe, e, DB, n_pages)
            kvcmp = compress_fin(ab, cmp_w2, e)
            q8 = jnp.pad(q[:DB].reshape(DB, G, R, d), ((0, 0), (0, 0), (0, SUBLANE - R), (0, 0)))
            o_cmp, imp = nsa_sample_cmp(q8, kvcmp, past_len)
            idx_t = nsa_sample_topk(_pad_rows(imp.reshape(DB * SUBLANE, -1), LANE), past_len)
            n_top = idx_t.shape[0]
            idx = idx_t[:, :DB * SUBLANE].T.reshape(DB, SUBLANE, n_top)[:, :G].reshape(DB * G * n_top)
            o8 = nsa_sample_attend(q8, o_cmp, gates[:DB].reshape(DB, G, 1, LANE), idx, pid, pages_s,
                                   new_kv_s, cache_nsa_win, new_win_s, e, past_len)
            o_nsa = _pad_rows(o8[:, :, :R].reshape(DB, G * R * d), SR)
            conv0 = state_rglru_conv[e]
            u_t = jnp.pad(u[:DB, None, :], ((0, 0), (0, TS - 1), (0, 0))).reshape(DB * TS, RW)
            gb_t = jnp.pad(gb[:DB, None, :], ((0, 0), (0, TS - 1), (0, 0))).reshape(DB * TS, RW)
            c0 = jnp.pad(conv0, ((0, 0), (SUBLANE - (CONV_WIDTH - 1), 0), (0, 0)))
            y_t, h_t = rglru(u_t, gb_t, c0, state_rglru_h[e][:, None, :], conv_w, conv_b, lru_wa, lru_ba,
                             lru_wx, lru_bx, lru_lambda, e, DB, TS)
            y_rnn = _pad_rows(y_t.reshape(DB, TS, RW)[:, 0], SR)
            mix_s = matmul(o_nsa, w_out, layer=e, x2=y_rnn)
            keep = min(WINDOW, past_len + 1)
            win_all = jnp.concatenate([cache_nsa_win[e], new_win_s[:, None]], axis=1)
            kv_s.append(new_kv_s[:, None])
            win_s.append(win_all[:, win_all.shape[1] - keep:])
            h_s.append(h_t.reshape(DB, TS, RW)[:, 0])
            cv_s.append(jnp.concatenate([conv0, u[:DB, None, :]], axis=1)[:, 1:])
        else:
            o = layer // 2
            mix_p = pool_mixer(xp, jnp.zeros((B, 2 * SUBLANE, D), F32), pool_w, pool_scale, o, B, T, 0)
            pl_p.append(xp.reshape(B, T, D)[:, T - POOL_BUF:])
            tsp = 2 * SUBLANE
            x_t = jnp.pad(xs[:DB, None, :], ((0, 0), (0, tsp - 1), (0, 0))).reshape(DB * tsp, D)
            buf = jnp.pad(state_pool[o], ((0, 0), (tsp - POOL_BUF, 0), (0, 0)))
            mix_t = pool_mixer(x_t, buf, pool_w, pool_scale, o, DB, tsp, past_len)
            mix_s = _pad_rows(mix_t.reshape(DB, tsp, D)[:, 0], SR)
            pl_s.append(jnp.concatenate([state_pool[o], xs[:DB, None, :]], axis=1)[:, 1:])

        xp, xpb = add_ln(xp, mix_p, lng, lnb, layer * 3, alpha)
        xs, _ = add_ln(xs, mix_s, lng, lnb, layer * 3, alpha)

        w_kv = jnp.concatenate([mem_wk[layer], mem_wv[layer]], axis=1)
        mkv = matmul(memf, w_kv).reshape(B, M, 2, MW)
        mem_p.append(mkv)
        qm = matmul(xpb, mem_wq, layer=layer, out_dtype=BF16)
        att = mem_attn(qm, mkv, B, T)
        xp, xpb, xp3 = add_ln(xp, matmul(att, mem_wo, layer=layer), lng, lnb, layer * 3 + 1, alpha,
                              token_major=True)
        qm = matmul(xs, mem_wq, layer=layer)
        qm_t = jnp.pad(qm[:DB, None, :], ((0, 0), (0, TS - 1), (0, 0))).reshape(DB * TS, MW)
        att_t = mem_attn(qm_t, cache_mem_kv.reshape((-1,) + cache_mem_kv.shape[2:]), DB, TS,
                         kv_off=layer * DB)
        att = _pad_rows(att_t.reshape(DB, TS, MW)[:, 0], SR)
        xs, _, xs3 = add_ln(xs, matmul(att, mem_wo, layer=layer), lng, lnb, layer * 3 + 1, alpha,
                            token_major=True)

        w_rt = _pad_rows(jnp.concatenate([moe_w_coarse[layer], moe_w_fine[layer]], axis=1).T, ROUTER_ROWS)
        b_rt = _pad_rows(jnp.concatenate([moe_b_coarse[layer], moe_b_fine[layer]])[:, None], ROUTER_ROWS)
        xp, xpb = hier_moe_ln(xp, xp3, B * T, w_rt, b_rt, moe_w_gate, moe_w_up, moe_w_down, layer, lng, lnb,
                              alpha, rows=256)
        xs = _pad_rows(hier_moe_ln(xs, xs3, DB, w_rt, b_rt, moe_w_gate, moe_w_up, moe_w_down, layer, lng, lnb,
                                   alpha, rows=SUBLANE)[0], SR)

    return (xp.reshape(B, T, D), xs[:DB].reshape(DB, 1, D), jnp.stack(kv_p), jnp.stack(kv_s),
            jnp.stack(win_p), jnp.stack(win_s), jnp.stack(h_p), jnp.stack(h_s), jnp.stack(cv_p),
            jnp.stack(cv_s), jnp.stack(pl_p), jnp.stack(pl_s), jnp.stack(mem_p))
```

```python
import functools

import jax
import jax.numpy as jnp
from jax import lax
from jax.experimental import pallas as pl
from jax.experimental.pallas import tpu as pltpu

F32 = jnp.float32
BF16 = jnp.bfloat16
I32 = jnp.int32

HEAD_DIM = 128
NSA_G = 4
NSA_R = 4
N_KV_STREAMS = 4
CMP_BLOCK = 32
CMP_STRIDE = 16
SEL_BLOCK = 64
SEL_TOP_N = 16
WINDOW = 512
FORCE_SCORE = 1e9
NEG = -1e30
ROPE_THETA = 10000.0
CONV_WIDTH = 4
LRU_C = 8.0
RNN_BLOCK_W = 128
POOL_WINDOWS = (2, 4, 8, 16)
POOL_BUF = 15
MEM_HEADS = 4
MOE_GROUPS = 4
MOE_EPG = 8
MOE_EXPERTS = MOE_GROUPS * MOE_EPG
MOE_TOP_K = 2
LN_EPS = 1e-5
PAGE_SIZE = 128

LANE = 128
SUBLANE = 8
VMEM_LIMIT = 56 * 1024 * 1024
SAMPLE_ROWS = 16


def _cp(sem, vmem=VMEM_LIMIT):
    return pltpu.CompilerParams(dimension_semantics=sem, vmem_limit_bytes=vmem)


def _split3(x):
    h = x.astype(BF16)
    r = x - h.astype(F32)
    m = r.astype(BF16)
    l = (r - m.astype(F32)).astype(BF16)
    return h, m, l


def _mxu(a, b, nt):
    if nt:
        return lax.dot_general(a, b, (((1,), (1,)), ((), ())), preferred_element_type=F32)
    return jnp.dot(a, b, preferred_element_type=F32)


def _dot(a, b, nt=False):
    return _mxu(a.astype(BF16), b.astype(BF16), nt)


def _bf16r(x):
    return x.astype(BF16).astype(F32)


def _sigmoid(x):
    return 1.0 / (1.0 + jnp.exp(-x))


def _gelu(x):
    return 0.5 * x * (1.0 + jnp.tanh(0.7978845608028654 * (x + 0.044715 * (x * x * x))))


def _msoftmax(s, mask):
    s = jnp.where(mask, s, NEG)
    m = jnp.max(s, axis=-1, keepdims=True)
    p = jnp.where(mask, jnp.exp(s - m), 0.0)
    return p / jnp.maximum(jnp.sum(p, axis=-1, keepdims=True), 1e-30)


def _msoftmax_nonempty(s, mask):
    s = jnp.where(mask, s, NEG)
    p = jnp.exp(s - jnp.max(s, axis=-1, keepdims=True))
    return p * (1.0 / jnp.sum(p, axis=-1, keepdims=True))


def _layer_norm(v, g, b):
    mu = jnp.mean(v, axis=-1, keepdims=True)
    c = v - mu
    var = jnp.mean(c * c, axis=-1, keepdims=True)
    return c * lax.rsqrt(var + LN_EPS) * g + b


def matmul(x, w, *, layer=0, col_off=0, n_out=None, x2=None, tm=1024, tn=512, rope=None, out_dtype=F32):
    if w.ndim == 2:
        w = w[None]
    M, K1 = x.shape
    K2 = 0 if x2 is None else x2.shape[1]
    assert w.shape[1] == K1 + K2
    n_out = w.shape[2] - col_off if n_out is None else n_out
    xs_in = [x] if x2 is None else [x, x2]
    if any(a.dtype != BF16 for a in xs_in):
        tm = min(tm, 512)
    tm, tn = min(tm, M), min(tn, n_out)
    assert M % tm == 0 and n_out % tn == 0 and col_off % tn == 0
    joff = col_off // tn

    in_specs = [pl.BlockSpec((tm, a.shape[1]), lambda i, j: (i, 0)) for a in xs_in]
    in_specs.append(pl.BlockSpec((None, K1 + K2, tn), lambda i, j: (layer, 0, j + joff)))
    args = xs_in + [w]
    rope_blocks = ()
    if rope is not None:
        cos, sin, rope_blocks = rope
        period = cos.shape[0] // tm
        assert cos.shape[0] % tm == 0
        for t in (cos, sin):
            in_specs.append(pl.BlockSpec((tm, LANE), lambda i, j: (i % period, 0)))
            args.append(t)
    cast = [a.dtype != BF16 for a in xs_in]
    scratch = [pltpu.VMEM((tm, a.shape[1]), BF16) for a, c in zip(xs_in, cast) if c]

    def kern(*refs):
        refs = list(refs)
        x_refs = [refs.pop(0) for _ in xs_in]
        w_ref = refs.pop(0)
        cos_ref, sin_ref = (refs.pop(0), refs.pop(0)) if rope is not None else (None, None)
        o_ref = refs.pop(0)
        j = pl.program_id(1)
        xb_refs = [refs.pop(0) if c else xr for xr, c in zip(x_refs, cast)]

        if any(cast):
            @pl.when(j == 0)
            def _():
                for xr, xb, c in zip(x_refs, xb_refs, cast):
                    if c:
                        xb[...] = xr[...].astype(BF16)

        acc = _mxu(xb_refs[0][...], w_ref[0:K1, :].astype(BF16), False)
        if x2 is not None:
            acc = acc + _mxu(xb_refs[1][...], w_ref[K1:K1 + K2, :].astype(BF16), False)
        if rope is None:
            o_ref[...] = acc.astype(out_dtype)
            return
        rot = functools.reduce(jnp.logical_or, [j == c for c in rope_blocks])

        @pl.when(rot)
        def _():
            c, s = cos_ref[...], sin_ref[...]
            for h in range(tn // LANE):
                seg = acc[:, h * LANE:(h + 1) * LANE]
                o_ref[:, h * LANE:(h + 1) * LANE] = (seg * c + pltpu.roll(seg, LANE // 2, 1) * s).astype(out_dtype)

        @pl.when(jnp.logical_not(rot))
        def _():
            o_ref[...] = acc.astype(out_dtype)

    return pl.pallas_call(
        kern,
        grid=(M // tm, n_out // tn),
        in_specs=in_specs,
        out_specs=pl.BlockSpec((tm, tn), lambda i, j: (i, j)),
        out_shape=jax.ShapeDtypeStruct((M, n_out), out_dtype),
        scratch_shapes=scratch,
        compiler_params=_cp(("parallel", "arbitrary")),
        name="matmul",
    )(*args)


def _token_major(v):
    return v.reshape(v.shape[0], v.shape[1] // LANE, LANE)


def add_ln(x, f, ln_g, ln_b, idx, alpha, tm=256, token_major=False):
    M, D = x.shape
    tm = min(tm, M)

    def kern(x_ref, f_ref, g_ref, b_ref, o_ref, ob_ref, *o3_ref):
        y = _layer_norm(alpha * x_ref[...] + f_ref[...], g_ref[...], b_ref[...])
        o_ref[...] = y
        ob_ref[...] = y.astype(BF16)
        if token_major:
            o3_ref[0][...] = _token_major(y)

    row = pl.BlockSpec((tm, D), lambda i: (i, 0))
    par = pl.BlockSpec((None, 1, D), lambda i: (idx, 0, 0))
    out_specs = [row, row]
    out_shape = [jax.ShapeDtypeStruct((M, D), F32), jax.ShapeDtypeStruct((M, D), BF16)]
    if token_major:
        out_specs.append(pl.BlockSpec((tm, D // LANE, LANE), lambda i: (i, 0, 0)))
        out_shape.append(jax.ShapeDtypeStruct((M, D // LANE, LANE), F32))
    return pl.pallas_call(
        kern, grid=(M // tm,), in_specs=[row, row, par, par], out_specs=out_specs,
        out_shape=out_shape, compiler_params=_cp(("parallel",)), name="add_ln",
    )(x, f, ln_g, ln_b)


CMP_PAGES = 8


def compress_ab(pages, page_ids, cmp_w1, cmp_pe, e, n_batch, pages_per_seq):
    P = CMP_PAGES
    assert pages_per_seq % P == 0
    cpp = PAGE_SIZE // CMP_STRIDE
    steps = pages_per_seq // P
    d, G = HEAD_DIM, NSA_G
    n_chunks = pages_per_seq * cpp
    rows = P * cpp * G

    def kern(pid_ref, *refs):
        page_refs, w_ref, pe_ref, o_ref = refs[:P], refs[P], refs[P + 1], refs[P + 2]
        for s in range(2):
            acc_a = jnp.zeros((rows, d), F32)
            acc_b = jnp.zeros((rows, d), F32)

            def chunk_rows(l):
                return jnp.concatenate(
                    [page_refs[i][pl.ds(l, cpp, stride=CMP_STRIDE), s].reshape(cpp * G, d) for i in range(P)],
                    axis=0)

            def pair(l, off):
                lhs = jnp.concatenate([rows2[0] + pe_ref[s, off + l:off + l + 1, :],
                                       rows2[1] + pe_ref[s, off + l + 1:off + l + 2, :]], axis=1)
                return _dot(lhs, w_ref[s, off + l:off + l + 2].reshape(2 * d, d))

            for l in range(0, CMP_STRIDE, 2):
                rows2 = (chunk_rows(l), chunk_rows(l + 1))
                acc_a = acc_a + pair(l, 0)
                acc_b = acc_b + pair(l, CMP_STRIDE)
            o_ref[:, s * 2 * d:s * 2 * d + d] = acc_a
            o_ref[:, s * 2 * d + d:(s + 1) * 2 * d] = acc_b

    def page_spec(i):
        return pl.BlockSpec((None, PAGE_SIZE, 2, G, d),
                            lambda b, st, pid: (pid[(b * steps + st) * P + i], 0, 0, 0, 0))

    gs = pltpu.PrefetchScalarGridSpec(
        num_scalar_prefetch=1,
        grid=(n_batch, steps),
        in_specs=[page_spec(i) for i in range(P)]
        + [pl.BlockSpec((None, 2, CMP_BLOCK, d, d), lambda b, st, pid: (e, 0, 0, 0, 0)),
           pl.BlockSpec((None, 2, CMP_BLOCK, d), lambda b, st, pid: (e, 0, 0, 0))],
        out_specs=pl.BlockSpec((None, rows, 2 * 2 * d), lambda b, st, pid: (b, st, 0)),
    )
    return pl.pallas_call(
        kern, grid_spec=gs,
        out_shape=jax.ShapeDtypeStruct((n_batch, n_chunks * G, 2 * 2 * d), F32),
        compiler_params=_cp(("parallel", "parallel")), name="compress_ab",
    )(page_ids, *([pages] * P), cmp_w1, cmp_pe)


def compress_fin(ab, cmp_w2, e):
    nb, nrow, _ = ab.shape
    d, G = HEAD_DIM, NSA_G
    nch = nrow // G

    def kern(ab_ref, w2_ref, o_ref):
        h = ab_ref[:, 0:d] + pltpu.roll(ab_ref[:, d:2 * d], nrow - G, 0)
        res = _dot(_gelu(h), w2_ref[...]).reshape(nch, G, d)
        for g in range(G):
            o_ref[g] = res[:, g, :]

    return pl.pallas_call(
        kern, grid=(nb, 2),
        in_specs=[pl.BlockSpec((None, nrow, 2 * d), lambda b, s: (b, 0, s)),
                  pl.BlockSpec((None, None, d, d), lambda b, s: (e, s, 0, 0))],
        out_specs=pl.BlockSpec((None, None, NSA_G, nch, d), lambda b, s: (s, b, 0, 0, 0)),
        out_shape=jax.ShapeDtypeStruct((2, nb, NSA_G, nch, d), F32),
        compiler_params=_cp(("parallel", "parallel")), name="compress_fin",
    )(ab, cmp_w2)


def _overlap_matrix(n_rows, n_cols):
    ci = jnp.arange(n_rows)[:, None] * CMP_STRIDE
    sj = jnp.arange(n_cols)[None, :] * SEL_BLOCK
    return ((ci < sj + SEL_BLOCK) & (ci + CMP_BLOCK > sj)).astype(BF16)


NSA_TQ = 128


def nsa_prompt(q, kv, win, kvcmp, gates, B, T):
    d, G, R, tq = HEAD_DIM, NSA_G, NSA_R, NSA_TQ
    nq = T // tq
    ncp = kvcmp.shape[3]
    n_cmp = T // CMP_STRIDE - 1
    n_sel = -(-T // SEL_BLOCK)
    n_top = min(SEL_TOP_N, n_sel)
    wk = min(T, WINDOW + tq)
    key_span = min(T, 2 * tq)
    assert n_sel <= LANE and T % tq == 0 and T % key_span == 0
    scale = d ** -0.5
    overlap = _overlap_matrix(ncp, LANE)
    expand = (jnp.arange(LANE)[:, None] == (jnp.arange(T) // SEL_BLOCK)[None, :]).astype(BF16)

    def kern(q_ref, ks_ref, vs_ref, kw_ref, vw_ref, kc_ref, vc_ref, g_ref, ov_ref, ex_ref, o_ref,
             ksb, vsb, kwb, vwb, kcb, vcb):
        qi = pl.program_id(2)

        @pl.when(qi == 0)
        def _():
            ksb[...] = ks_ref[...].astype(BF16)
            vsb[...] = vs_ref[...].astype(BF16)
            kwb[...] = kw_ref[...].astype(BF16)
            vwb[...] = vw_ref[...].astype(BF16)
            kcb[...] = kc_ref[...].astype(BF16)
            vcb[...] = vc_ref[...].astype(BF16)

        t0 = qi * tq
        qb = q_ref[...]
        qs = jnp.concatenate([qb[:, r * d:(r + 1) * d] for r in range(R)], axis=0).astype(BF16)
        qpos_s = t0 + (lax.broadcasted_iota(I32, (R * tq, 1), 0) & (tq - 1))

        s = _mxu(qs, kcb[...], True) * scale
        n_idx = lax.broadcasted_iota(I32, (1, ncp), 1)
        cmask = (n_idx * CMP_STRIDE + (CMP_BLOCK - 1) <= qpos_s) & (n_idx < n_cmp)
        p = _msoftmax(s, cmask)
        pb = p.astype(BF16)
        o_cmp = _mxu(pb, vcb[...], False)
        imp_r = _mxu(pb, ov_ref[...], False)
        imp = imp_r[0:tq]
        for r in range(1, R):
            imp = imp + imp_r[r * tq:(r + 1) * tq]

        blk = lax.broadcasted_iota(I32, (tq, LANE), 1)
        qpos = t0 + lax.broadcasted_iota(I32, (tq, LANE), 0)
        cur = qpos // SEL_BLOCK
        valid = blk <= cur
        forced = (blk == 0) | (blk == cur) | (blk == cur - 1)
        score = jnp.where(forced, FORCE_SCORE, jnp.where(valid, imp, NEG))
        cnt = jnp.zeros((tq, LANE), I32)
        for i in range(n_sel):
            col = score[:, i:i + 1]
            beats = (col > score) | ((col == score) & (blk > i))
            cnt = cnt + beats.astype(I32)
        sel = ((cnt < n_top) & valid).astype(BF16)

        kstart = pl.multiple_of(jnp.clip(t0 - WINDOW, 0, T - wk), tq)
        kwin = kwb[pl.ds(kstart, wk), :]
        vwin = vwb[pl.ds(kstart, wk), :]
        s_w = _mxu(qs, kwin, True) * scale
        kp = kstart + lax.broadcasted_iota(I32, (1, wk), 1)
        wmask = (kp <= qpos_s) & (kp > qpos_s - WINDOW)
        p_w = _msoftmax_nonempty(s_w, wmask)
        o_win = _mxu(p_w.astype(BF16), vwin, False)

        gt = _sigmoid(g_ref[...])

        def selected(nkeys):
            selx = _mxu(sel, ex_ref[:, 0:nkeys], False)
            kpos = lax.broadcasted_iota(I32, (tq, nkeys), 1)
            qpos_t = t0 + lax.broadcasted_iota(I32, (tq, nkeys), 0)
            smask = (selx > 0.5) & (kpos <= qpos_t)
            for r in range(R):
                s_r = _mxu(qs[r * tq:(r + 1) * tq], ksb[0:nkeys, :], True) * scale
                p_r = _msoftmax_nonempty(s_r, smask)
                o_slc = _mxu(p_r.astype(BF16), vsb[0:nkeys, :], False)
                o_r = (gt[:, 3 * r:3 * r + 1] * o_cmp[r * tq:(r + 1) * tq]
                       + gt[:, 3 * r + 1:3 * r + 2] * o_slc
                       + gt[:, 3 * r + 2:3 * r + 3] * o_win[r * tq:(r + 1) * tq])
                o_ref[:, r * d:(r + 1) * d] = o_r.astype(BF16)

        n_span = pl.cdiv(t0 + tq, key_span)
        for n in range(1, T // key_span + 1):
            pl.when(n_span == n)(functools.partial(selected, n * key_span))

    seq = lambda blkcol: pl.BlockSpec((T, d), lambda b, g, qi: (b, blkcol(g)))
    cmp_spec = lambda s: pl.BlockSpec((None, None, None, ncp, d), lambda b, g, qi: (s, b, g, 0, 0))
    return pl.pallas_call(
        kern, grid=(B, G, nq),
        in_specs=[pl.BlockSpec((tq, R * d), lambda b, g, qi: (b * nq + qi, g)),
                  seq(lambda g: 2 * G + g), seq(lambda g: 3 * G + g),
                  seq(lambda g: g), seq(lambda g: G + g),
                  cmp_spec(0), cmp_spec(1),
                  pl.BlockSpec((tq, LANE), lambda b, g, qi: (b * nq + qi, g)),
                  pl.BlockSpec((ncp, LANE), lambda b, g, qi: (0, 0)),
                  pl.BlockSpec((LANE, T), lambda b, g, qi: (0, 0))],
        out_specs=pl.BlockSpec((tq, R * d), lambda b, g, qi: (b * nq + qi, g)),
        out_shape=jax.ShapeDtypeStruct((B * T, G * R * d), BF16),
        scratch_shapes=[pltpu.VMEM((T, d), BF16)] * 4 + [pltpu.VMEM((ncp, d), BF16)] * 2,
        compiler_params=_cp(("parallel", "parallel", "arbitrary")), name="nsa_prompt",
    )(q, kv, kv, win, win, kvcmp, kvcmp, gates, overlap, expand)


def nsa_sample_cmp(q8, kvcmp, q_pos):
    B = q8.shape[0]
    d, G, R = HEAD_DIM, NSA_G, NSA_R
    nch = kvcmp.shape[3]
    n_cmp = nch - 1
    n_sel = -(-(q_pos + 1) // SEL_BLOCK)
    nsp = -(-n_sel // LANE) * LANE
    overlap = _overlap_matrix(nch, nsp)
    scale = d ** -0.5

    def kern(q_ref, k_ref, v_ref, ov_ref, o_ref, imp_ref):
        n_idx = lax.broadcasted_iota(I32, (1, nch), 1)
        cmask = (n_idx * CMP_STRIDE + (CMP_BLOCK - 1) <= q_pos) & (n_idx < n_cmp)
        rows = []
        for g in range(G):
            s = _dot(q_ref[g], k_ref[g], nt=True) * scale
            pb = _msoftmax(s, cmask).astype(BF16)
            o_ref[g] = _dot(pb, v_ref[g])
            rows.append(jnp.sum(_mxu(pb, ov_ref[...], False)[0:R], axis=0, keepdims=True))
        imp_ref[...] = jnp.concatenate(rows + [jnp.zeros((SUBLANE - G, nsp), F32)], axis=0)

    qspec = pl.BlockSpec((None, G, SUBLANE, d), lambda b: (b, 0, 0, 0))
    cspec = lambda s: pl.BlockSpec((None, None, G, nch, d), lambda b: (s, b, 0, 0, 0))
    return pl.pallas_call(
        kern, grid=(B,),
        in_specs=[qspec, cspec(0), cspec(1), pl.BlockSpec((nch, nsp), lambda b: (0, 0))],
        out_specs=[qspec, pl.BlockSpec((None, SUBLANE, nsp), lambda b: (b, 0, 0))],
        out_shape=[jax.ShapeDtypeStruct((B, G, SUBLANE, d), F32),
                   jax.ShapeDtypeStruct((B, SUBLANE, nsp), F32)],
        compiler_params=_cp(("parallel",)), name="nsa_sample_cmp",
    )(q8, kvcmp, kvcmp, overlap)


def nsa_sample_topk(imp, q_pos):
    nr, nsp = imp.shape
    assert nr == LANE
    n_sel = -(-(q_pos + 1) // SEL_BLOCK)
    n_top = min(SEL_TOP_N, n_sel)
    cur = q_pos // SEL_BLOCK

    def kern(imp_ref, o_ref, sct):
        blk = lax.broadcasted_iota(I32, (nr, nsp), 1)
        valid = blk <= cur
        forced = (blk == 0) | (blk == cur) | (blk == cur - 1)
        score = jnp.where(forced, FORCE_SCORE, jnp.where(valid, imp_ref[...], NEG))
        eye = (lax.broadcasted_iota(I32, (nsp, nsp), 0) == lax.broadcasted_iota(I32, (nsp, nsp), 1)).astype(BF16)
        h, m, l = _split3(score)
        sct[...] = (_mxu(eye, h, True) + _mxu(eye, m, True)) + _mxu(eye, l, True)
        sc = sct[...]
        jidx = lax.broadcasted_iota(I32, (nsp, nr), 0)

        def body(i, cnt):
            row = sct[pl.ds(i, 1), :]
            beats = (row > sc) | ((row == sc) & (jidx > i))
            return cnt + beats.astype(I32)

        cnt = lax.fori_loop(0, n_sel, body, jnp.zeros((nsp, nr), I32))
        for p in range(n_top):
            o_ref[p:p + 1, :] = jnp.sum(jnp.where(cnt == p, jidx, 0), axis=0, keepdims=True)

    return pl.pallas_call(
        kern, grid=(1,),
        in_specs=[pl.BlockSpec((nr, nsp), lambda i: (0, 0))],
        out_specs=pl.BlockSpec((n_top, nr), lambda i: (0, 0)),
        out_shape=jax.ShapeDtypeStruct((n_top, nr), I32),
        scratch_shapes=[pltpu.VMEM((nsp, nr), F32)],
        compiler_params=_cp(("arbitrary",)), name="nsa_sample_topk",
    )(imp)


def nsa_sample_attend(q8, o_cmp, gates, idx, page_ids, pages, new_kv, cwin, new_win, e, q_pos):
    B = q8.shape[0]
    d, G, R = HEAD_DIM, NSA_G, NSA_R
    n_top = idx.shape[0] // (B * G)
    n_pages = page_ids.shape[0] // B
    n_cache_blocks = n_pages * (PAGE_SIZE // SEL_BLOCK)
    bpp = PAGE_SIZE // SEL_BLOCK
    w0 = cwin.shape[2]
    scale = d ** -0.5
    nk = n_top * SEL_BLOCK

    def kern(idx_ref, pid_ref, q_ref, oc_ref, g_ref, pages_ref, nkv_ref, cw_ref, nw_ref, o_ref,
             kvbuf, sem):
        b = pl.program_id(0)

        def slot_copy(slot, j):
            page = pid_ref[b * n_pages + j // bpp]
            r0 = (j % bpp) * SEL_BLOCK
            return pltpu.make_async_copy(
                pages_ref.at[page, pl.ds(r0, SEL_BLOCK), pl.ds(2, 2)], kvbuf.at[slot], sem.at[0])

        for slot in range(G * n_top):
            j = idx_ref[b * G * n_top + slot]

            @pl.when(j < n_cache_blocks)
            def _():
                slot_copy(slot, j).start()

            @pl.when(j >= n_cache_blocks)
            def _():
                kvbuf[slot] = jnp.zeros((SEL_BLOCK, 2, G, d), F32)

            @pl.when(j * SEL_BLOCK == q_pos)
            def _():
                kvbuf[slot, 0] = nkv_ref[2:4]

        for slot in range(G * n_top):
            j = idx_ref[b * G * n_top + slot]

            @pl.when(j < n_cache_blocks)
            def _():
                slot_copy(slot, j).wait()

        lane_k = lax.broadcasted_iota(I32, (1, nk), 1)
        row8 = lax.broadcasted_iota(I32, (SUBLANE, 1), 0)
        for g in range(G):
            qg = q_ref[g]
            kk = jnp.concatenate([kvbuf[g * n_top + p, :, 0, g, :] for p in range(n_top)], axis=0)
            vv = jnp.concatenate([kvbuf[g * n_top + p, :, 1, g, :] for p in range(n_top)], axis=0)
            kpos = lane_k & (SEL_BLOCK - 1)
            for p in range(n_top):
                j = idx_ref[(b * G + g) * n_top + p]
                kpos = kpos + jnp.where(lane_k // SEL_BLOCK == p, j * SEL_BLOCK, 0)
            s = _dot(qg, kk, nt=True) * scale
            p_s = _msoftmax(s, kpos <= q_pos)
            o_slc = _dot(p_s, vv)
            kw = cw_ref[:, 0, g, :]
            vw = cw_ref[:, 1, g, :]
            s_w = _dot(qg, kw, nt=True) * scale
            wpos = (q_pos - w0) + lax.broadcasted_iota(I32, (1, w0), 1)
            wmask = (wpos >= 0) & (wpos <= q_pos) & (wpos > q_pos - WINDOW)
            s_w = jnp.where(wmask, s_w, NEG)
            s_n = jnp.sum(_bf16r(qg) * _bf16r(nw_ref[0, g:g + 1, :]), axis=-1, keepdims=True) * scale
            m = jnp.maximum(jnp.max(s_w, axis=-1, keepdims=True), s_n)
            p_w = jnp.where(wmask, jnp.exp(s_w - m), 0.0)
            p_n = jnp.exp(s_n - m)
            den = jnp.sum(p_w, axis=-1, keepdims=True) + p_n
            o_win = _dot(p_w / den, vw) + _bf16r(p_n / den) * _bf16r(nw_ref[1, g:g + 1, :])
            gt = _sigmoid(g_ref[g])
            gc = [jnp.zeros((SUBLANE, 1), F32)] * 3
            for r in range(R):
                for c in range(3):
                    gc[c] = jnp.where(row8 == r, gt[:, 3 * r + c:3 * r + c + 1], gc[c])
            o_ref[g] = gc[0] * oc_ref[g] + gc[1] * o_slc + gc[2] * o_win

    qspec = pl.BlockSpec((None, G, SUBLANE, d), lambda b, i, p: (b, 0, 0, 0))
    gs = pltpu.PrefetchScalarGridSpec(
        num_scalar_prefetch=2, grid=(B,),
        in_specs=[qspec, qspec,
                  pl.BlockSpec((None, G, 1, LANE), lambda b, i, p: (b, 0, 0, 0)),
                  pl.BlockSpec(memory_space=pl.ANY),
                  pl.BlockSpec((None, N_KV_STREAMS, G, d), lambda b, i, p: (b, 0, 0, 0)),
                  pl.BlockSpec((None, None, w0, 2, G, d), lambda b, i, p: (e, b, 0, 0, 0, 0)),
                  pl.BlockSpec((None, 2, G, d), lambda b, i, p: (b, 0, 0, 0))],
        out_specs=qspec,
        scratch_shapes=[pltpu.VMEM((G * n_top, SEL_BLOCK, 2, G, d), F32),
                        pltpu.SemaphoreType.DMA((1,))],
    )
    return pl.pallas_call(
        kern, grid_spec=gs, out_shape=jax.ShapeDtypeStruct((B, G, SUBLANE, d), F32),
        compiler_params=_cp(("arbitrary",)), name="nsa_sample_attend",
    )(idx, page_ids, q8, o_cmp, gates, pages, new_kv, cwin, new_win)


def rglru(u, gb, conv0, h0, conv_w, conv_b, lru_wa, lru_ba, lru_wx, lru_bx, lru_lambda, e, B, T):
    W = u.shape[1]
    cw = RNN_BLOCK_W
    nblk = W // cw
    tc = min(T, 256)
    nt = T // tc
    assert T % tc == 0 and tc % SUBLANE == 0

    def kern(u_ref, gb_ref, c0_ref, h0_ref, cw_ref, cb_ref, wa_ref, ba_ref, wx_ref, bx_ref, lam_ref,
             y_ref, h_ref, prev, hprev):
        ti = pl.program_id(2)

        @pl.when(ti == 0)
        def _():
            prev[...] = c0_ref[...]
            hprev[...] = h0_ref[...]

        uu = u_ref[...]
        ext = _bf16r(jnp.concatenate([prev[...], uu], axis=0))
        prev[...] = uu[tc - SUBLANE:tc]
        cwr = _bf16r(cw_ref[...])
        tap0 = SUBLANE - (CONV_WIDTH - 1)
        xc = cwr[0:1, :] * ext[tap0:tap0 + tc]
        for j in range(1, CONV_WIDTH):
            xc = xc + cwr[j:j + 1, :] * ext[tap0 + j:tap0 + j + tc]
        xc = xc + cb_ref[...]
        r = _sigmoid(_dot(xc, wa_ref[...]) + ba_ref[...])
        i = _sigmoid(_dot(xc, wx_ref[...]) + bx_ref[...])
        nl = -lam_ref[...]
        softplus = jnp.maximum(nl, 0.0) + jnp.log1p(jnp.exp(-jnp.abs(nl)))
        log_a = -LRU_C * r * softplus
        a = jnp.exp(log_a)
        bt = jnp.sqrt(-jnp.tanh(log_a) * (a * a + 1.0)) * (i * xc)
        rows = lax.broadcasted_iota(I32, (tc, 1), 0)
        step = 1
        while step < tc:
            keep = rows >= step
            bt = jnp.where(keep, a * pltpu.roll(bt, step, 0) + bt, bt)
            a = jnp.where(keep, a * pltpu.roll(a, step, 0), a)
            step *= 2
        h = a * hprev[...] + bt
        hprev[...] = h[tc - 1:tc]
        h_ref[...] = h
        y_ref[...] = (h * _gelu(gb_ref[...])).astype(BF16)

    tile = pl.BlockSpec((tc, cw), lambda b, c, t: (b * nt + t, c))
    vec = lambda: pl.BlockSpec((None, 1, cw), lambda b, c, t: (e, 0, c))
    mat = lambda: pl.BlockSpec((None, None, cw, cw), lambda b, c, t: (e, c, 0, 0))
    bvec = lambda: pl.BlockSpec((None, None, 1, cw), lambda b, c, t: (e, c, 0, 0))
    return pl.pallas_call(
        kern, grid=(B, nblk, nt),
        in_specs=[tile, tile,
                  pl.BlockSpec((None, SUBLANE, cw), lambda b, c, t: (b, 0, c)),
                  pl.BlockSpec((None, 1, cw), lambda b, c, t: (b, 0, c)),
                  pl.BlockSpec((None, CONV_WIDTH, cw), lambda b, c, t: (e, 0, c)),
                  vec(), mat(), bvec(), mat(), bvec(), vec()],
        out_specs=[tile, tile],
        out_shape=[jax.ShapeDtypeStruct((B * T, W), BF16), jax.ShapeDtypeStruct((B * T, W), F32)],
        scratch_shapes=[pltpu.VMEM((SUBLANE, cw), F32), pltpu.VMEM((1, cw), F32)],
        compiler_params=_cp(("parallel", "parallel", "arbitrary")), name="rglru",
    )(u, gb, conv0, h0, conv_w, conv_b[:, None, :], lru_wa, lru_ba[:, :, None, :], lru_wx,
      lru_bx[:, :, None, :], lru_lambda[:, None, :])


def pool_mixer(x, buf16, pool_w, pool_scale, o, B, T, t0):
    D = x.shape[1]
    ng = len(POOL_WINDOWS)
    gw = D // ng
    tt = min(T, 256)
    nt = T // tt
    halo = 2 * SUBLANE
    assert T % tt == 0 and tt >= halo and POOL_WINDOWS == (2, 4, 8, 16)

    def kern(x_ref, buf_ref, w_ref, sc_ref, o_ref, prev):
        g, ti = pl.program_id(0), pl.program_id(2)

        @pl.when(ti == 0)
        def _():
            prev[...] = buf_ref[...]

        xx = x_ref[...]
        ext = jnp.concatenate([prev[...], xx], axis=0)
        prev[...] = xx[tt - halo:tt]
        s2 = ext + pltpu.roll(ext, 1, 0)
        s4 = s2 + pltpu.roll(s2, 2, 0)
        s8 = s4 + pltpu.roll(s4, 4, 0)
        s16 = s8 + pltpu.roll(s8, 8, 0)
        sw = jnp.where(g == 0, s2, jnp.where(g == 1, s4, jnp.where(g == 2, s8, s16)))[halo:]
        wlen = jnp.left_shift(2, g)
        pos = t0 + ti * tt + lax.broadcasted_iota(I32, (tt, 1), 0)
        cnt = jnp.minimum(wlen, pos + 1).astype(F32)
        pooled = sw / cnt - xx
        o_ref[...] = _dot(pooled, w_ref[...]) * sc_ref[...]

    return pl.pallas_call(
        kern, grid=(ng, B, nt),
        in_specs=[pl.BlockSpec((tt, gw), lambda g, b, t: (b * nt + t, g)),
                  pl.BlockSpec((None, halo, gw), lambda g, b, t: (b, 0, g)),
                  pl.BlockSpec((None, None, gw, gw), lambda g, b, t: (o, g, 0, 0)),
                  pl.BlockSpec((None, 1, gw), lambda g, b, t: (o, 0, g))],
        out_specs=pl.BlockSpec((tt, gw), lambda g, b, t: (b * nt + t, g)),
        out_shape=jax.ShapeDtypeStruct((B * T, D), F32),
        scratch_shapes=[pltpu.VMEM((halo, gw), F32)],
        compiler_params=_cp(("parallel", "parallel", "arbitrary")), name="pool_mixer",
    )(x, buf16, pool_w, pool_scale[:, None, :])


def mem_attn(q, mkv, B, T, kv_off=0):
    MW = q.shape[1]
    M = mkv.shape[1]
    dh = MW // MEM_HEADS
    tq = min(T, 256)
    nq = T // tq
    scale = dh ** -0.5

    def kern(q_ref, kv_ref, o_ref):
        for h in range(MEM_HEADS):
            sl = slice(h * dh, (h + 1) * dh)
            s = _dot(q_ref[:, sl], kv_ref[:, 0, sl], nt=True) * scale
            m = jnp.max(s, axis=-1, keepdims=True)
            p = jnp.exp(s - m)
            p = p / jnp.sum(p, axis=-1, keepdims=True)
            o_ref[:, sl] = _dot(p, kv_ref[:, 1, sl]).astype(BF16)

    return pl.pallas_call(
        kern, grid=(B, nq),
        in_specs=[pl.BlockSpec((tq, MW), lambda b, i: (b * nq + i, 0)),
                  pl.BlockSpec((None, M, 2, MW), lambda b, i: (kv_off + b, 0, 0, 0))],
        out_specs=pl.BlockSpec((tq, MW), lambda b, i: (b * nq + i, 0)),
        out_shape=jax.ShapeDtypeStruct((B * T, MW), BF16),
        compiler_params=_cp(("parallel", "parallel")), name="mem_attn",
    )(q, mkv)


ROUTER_ROWS = 40


def moe_router(x, w_t, bias, tm=512):
    N, D = x.shape
    tm = min(tm, N)
    assert N % tm == 0 and tm % LANE == 0

    def kern(x_ref, w_ref, b_ref, e_ref, g_ref):
        lt = _dot(w_ref[...], x_ref[...], nt=True) + b_ref[...]
        c = [lt[i:i + 1] for i in range(MOE_GROUPS)]
        m = functools.reduce(jnp.maximum, c)
        ex = [jnp.exp(ci - m) for ci in c]
        tot = functools.reduce(jnp.add, ex)
        pc = [ei / tot for ei in ex]
        pg = functools.reduce(jnp.maximum, pc)
        grp = jnp.full(pg.shape, MOE_GROUPS - 1, I32)
        for i in range(MOE_GROUPS - 2, -1, -1):
            grp = jnp.where(pc[i] == pg, i, grp)
        lf = []
        for j in range(MOE_EPG):
            v = lt[MOE_GROUPS + j:MOE_GROUPS + j + 1]
            for gi in range(1, MOE_GROUPS):
                r0 = MOE_GROUPS + gi * MOE_EPG + j
                v = jnp.where(grp == gi, lt[r0:r0 + 1], v)
            lf.append(v)
        m = functools.reduce(jnp.maximum, lf)
        ex = [jnp.exp(v - m) for v in lf]
        tot = functools.reduce(jnp.add, ex)
        pf = [ei / tot for ei in ex]

        def first_max(vals):
            best = functools.reduce(jnp.maximum, vals)
            arg = jnp.full(best.shape, len(vals) - 1, I32)
            for j in range(len(vals) - 2, -1, -1):
                arg = jnp.where(vals[j] == best, j, arg)
            return best, arg

        p1, j1 = first_max(pf)
        p2, j2 = first_max([jnp.where(j1 == j, -1.0, pf[j]) for j in range(MOE_EPG)])
        den = p1 + p2
        e_ref[0:1, :] = grp * MOE_EPG + j1
        e_ref[1:2, :] = grp * MOE_EPG + j2
        g_ref[0:1, :] = pg * p1 / den
        g_ref[1:2, :] = pg * p2 / den

    out = pl.BlockSpec((MOE_TOP_K, tm), lambda i: (0, i))
    return pl.pallas_call(
        kern, grid=(N // tm,),
        in_specs=[pl.BlockSpec((tm, D), lambda i: (i, 0)),
                  pl.BlockSpec((ROUTER_ROWS, D), lambda i: (0, 0)),
                  pl.BlockSpec((ROUTER_ROWS, 1), lambda i: (0, 0))],
        out_specs=[out, out],
        out_shape=[jax.ShapeDtypeStruct((MOE_TOP_K, N), I32), jax.ShapeDtypeStruct((MOE_TOP_K, N), F32)],
        compiler_params=_cp(("parallel",)), name="moe_router",
    )(x, w_t, bias)


MOE_SUB = 2
MOE_TF = 256
MOE_VMEM = 60 * 1024 * 1024


def moe_experts_grouped(x3, tok, aid, nvalid, gexp, nsub, nused, w_gate, w_up, w_down, layer, rows, n_assign):
    _, nc, _ = x3.shape
    D = nc * LANE
    SB, tf = MOE_SUB, MOE_TF
    n_groups = tok.shape[0] // (SB * rows)
    nf = w_gate.shape[3] // tf
    assert nf >= 2 and SB == 2

    def kern(tok_ref, aid_ref, nv_ref, gexp_ref, nsub_ref, nused_ref, x3_ref, wg_ref, wu_ref, wd_ref, y_ref,
             xbuf, xb, acc, ybuf, sem_in, sem_out):
        g, f = pl.program_id(0), pl.program_id(1)
        nused = nused_ref[0]
        active = g < nused
        ns = nsub_ref[g]

        def gather_start(grp, j):
            base = (grp * SB + j) * rows

            def body(r, carry):
                pltpu.make_async_copy(x3_ref.at[tok_ref[base + r]], xbuf.at[r], sem_in.at[0]).start()
                return carry
            lax.fori_loop(0, rows, body, 0, unroll=SUBLANE)

        def gather_wait():
            pltpu.make_async_copy(x3_ref.at[pl.ds(0, rows)], xbuf, sem_in.at[0]).wait()

        def scatter_start(grp, j):
            base = (grp * SB + j) * rows

            def body(r, carry):
                pltpu.make_async_copy(ybuf.at[j, r], y_ref.at[aid_ref[base + r]], sem_out.at[j]).start()
                return carry
            lax.fori_loop(0, nv_ref[grp * SB + j], body, 0)

        def scatter_wait(grp, j):
            nv = nv_ref[grp * SB + j]

            @pl.when(nv > 0)
            def _():
                pltpu.make_async_copy(ybuf.at[j, pl.ds(0, nv)], y_ref.at[pl.ds(0, nv)], sem_out.at[j]).wait()

        @pl.when(active & (f == 0))
        def _():
            pl.when(g == 0)(lambda: gather_start(0, 0))
            for j in range(SB):
                @pl.when(j < ns)
                def _():
                    if j > 0:
                        gather_start(g, j)
                    gather_wait()
                    xb[j * rows:(j + 1) * rows, :] = xbuf[...].reshape(rows, D).astype(BF16)
            acc[...] = jnp.zeros_like(acc)

        pl.when((f == nf - 2) & (g + 1 < nused))(lambda: gather_start(g + 1, 0))

        @pl.when(active)
        def _():
            wg, wu, wd = wg_ref[...].astype(BF16), wu_ref[...].astype(BF16), wd_ref[...].astype(BF16)
            for j in range(SB):
                @pl.when(j < ns)
                def _():
                    x = xb[j * rows:(j + 1) * rows, :]
                    hg = _mxu(x, wg, False)
                    hu = _mxu(x, wu, False)
                    hid = (hg * _sigmoid(hg) * hu).astype(BF16)
                    acc[j * rows:(j + 1) * rows, :] += _mxu(hid, wd, False)

        @pl.when(active & (f == nf - 1))
        def _():
            for j in range(SB):
                pl.when((g >= 1) & (j < nsub_ref[jnp.maximum(g - 1, 0)]))(
                    functools.partial(scatter_wait, jnp.maximum(g - 1, 0), j))

                @pl.when(j < ns)
                def _():
                    ybuf[j] = _token_major(acc[j * rows:(j + 1) * rows, :])
                    scatter_start(g, j)

            @pl.when(g == nused - 1)
            def _():
                for j in range(SB):
                    pl.when(j < ns)(functools.partial(scatter_wait, g, j))

    def live(g, nu):
        return jnp.maximum(jnp.minimum(g, nu[0] - 1), 0)

    def feff(g, f, nu):
        return jnp.where(g < nu[0], f, nf - 1)

    def wspec(shape, col):
        def imap(g, f, t, a, nv, ge, nsb, nu):
            e, ff = ge[live(g, nu)], feff(g, f, nu)
            return (layer, e, 0, ff) if col else (layer, e, ff, 0)
        return pl.BlockSpec(shape, imap)

    gs = pltpu.PrefetchScalarGridSpec(
        num_scalar_prefetch=6, grid=(n_groups, nf),
        in_specs=[pl.BlockSpec(memory_space=pl.ANY),
                  wspec((None, None, D, tf), True), wspec((None, None, D, tf), True),
                  wspec((None, None, tf, D), False)],
        out_specs=pl.BlockSpec(memory_space=pl.ANY),
        scratch_shapes=[pltpu.VMEM((rows, nc, LANE), F32), pltpu.VMEM((SB * rows, D), BF16),
                        pltpu.VMEM((SB * rows, D), F32), pltpu.VMEM((SB, rows, nc, LANE), F32),
                        pltpu.SemaphoreType.DMA((1,)), pltpu.SemaphoreType.DMA((SB,))],
    )
    return pl.pallas_call(
        kern, grid_spec=gs, out_shape=jax.ShapeDtypeStruct((n_assign, nc, LANE), F32),
        compiler_params=_cp(("arbitrary", "arbitrary"), MOE_VMEM), name="moe_experts",
    )(tok, aid, nvalid, gexp, nsub, nused, x3, w_gate, w_up, w_down)


def moe_combine_ln(x, ytok, gate, ln_g, ln_b, idx, alpha, n_tok, tm=128):
    D = x.shape[1]
    nc = D // LANE
    tm = min(tm, n_tok)

    def kern(x_ref, y_ref, gt_ref, g_ref, b_ref, o_ref, ob_ref):
        gt = _bf16r(gt_ref[...])
        mix = sum(gt[:, k:k + 1] * _bf16r(y_ref[:, k].reshape(tm, D)) for k in range(MOE_TOP_K))
        y = _layer_norm(alpha * x_ref[...] + mix, g_ref[...], b_ref[...])
        o_ref[...] = y
        ob_ref[...] = y.astype(BF16)

    row = pl.BlockSpec((tm, D), lambda i: (i, 0))
    par = pl.BlockSpec((None, 1, D), lambda i: (idx, 0, 0))
    return pl.pallas_call(
        kern, grid=(n_tok // tm,),
        in_specs=[row, pl.BlockSpec((tm, MOE_TOP_K, nc, LANE), lambda i: (i, 0, 0, 0)),
                  pl.BlockSpec((tm, MOE_TOP_K), lambda i: (i, 0)), par, par],
        out_specs=[row, row],
        out_shape=[jax.ShapeDtypeStruct((n_tok, D), F32), jax.ShapeDtypeStruct((n_tok, D), BF16)],
        compiler_params=_cp(("parallel",)), name="moe_combine_ln",
    )(x, ytok, gate, ln_g, ln_b)


def hier_moe_ln(x, x3, n_tok, w_rt, b_rt, w_gate, w_up, w_down, layer, ln_g, ln_b, alpha, rows):
    N, D = x.shape
    xr = x if N % LANE == 0 else jnp.pad(x, ((0, LANE - N % LANE), (0, 0)))
    eidx, gate = moe_router(xr, w_rt, b_rt)
    eidx, gate = eidx[:, :n_tok], gate[:, :n_tok]
    A = n_tok * MOE_TOP_K
    e = eidx.T.reshape(A)
    onehot = (e[:, None] == jnp.arange(MOE_EXPERTS, dtype=I32)[None, :]).astype(I32)
    csum = jnp.cumsum(onehot, axis=0)
    counts = csum[-1]
    pos = jnp.sum(csum * onehot, axis=1) - 1
    grows = MOE_SUB * rows
    padded = (counts + grows - 1) // grows * grows
    pad_end = jnp.cumsum(padded)
    pad_start = pad_end - padded
    dest = (jnp.sum(onehot * pad_start[None, :], axis=1) + pos).astype(I32)
    n_groups = min(A // grows + MOE_EXPERTS, A)
    gexp = jnp.minimum(jnp.searchsorted(pad_end, jnp.arange(n_groups, dtype=I32) * grows, side="right"),
                       MOE_EXPERTS - 1).astype(I32)
    nused = (pad_end[-1:] // grows).astype(I32)
    valid_end = (pad_start + counts)[gexp]
    nvalid = jnp.clip(jnp.repeat(valid_end, MOE_SUB) - jnp.arange(n_groups * MOE_SUB, dtype=I32) * rows,
                      0, rows).astype(I32)
    nsub = jnp.sum((nvalid > 0).reshape(n_groups, MOE_SUB), axis=1).astype(I32)
    aid = jnp.zeros((n_groups * grows,), I32).at[dest].set(jnp.arange(A, dtype=I32))
    ytok = moe_experts_grouped(x3, aid // MOE_TOP_K, aid, nvalid, gexp, nsub, nused, w_gate, w_up, w_down,
                               layer, rows, A)
    ytok = ytok.reshape(n_tok, MOE_TOP_K, D // LANE, LANE)
    return moe_combine_ln(x, ytok, gate.T, ln_g, ln_b, layer * 3 + 2, alpha, n_tok)


def _rope_tables(pos):
    half = HEAD_DIM // 2
    inv_freq = ROPE_THETA ** (-jnp.arange(half, dtype=F32) / half)
    ang = pos.astype(F32)[:, None] * inv_freq[None, :]
    cos, sin = jnp.cos(ang), jnp.sin(ang)
    return jnp.concatenate([cos, cos], axis=1), jnp.concatenate([-sin, sin], axis=1)


def _even_projections(x, pos_rows, w_in, w_rest, w_gates, e, tm, q_dtype):
    G, d = NSA_G, HEAD_DIM
    q_cols = G * NSA_R * d
    kv_cols = N_KV_STREAMS * G * d
    cos, sin = _rope_tables(pos_rows)
    tn = G * d
    mm = functools.partial(matmul, x, tm=tm)
    q = mm(w_in, layer=e, col_off=0, n_out=q_cols, tn=tn, rope=(cos, sin, tuple(range(q_cols // tn))),
           out_dtype=q_dtype)
    kv = mm(w_in, layer=e, col_off=q_cols, n_out=kv_cols, tn=tn, rope=(cos, sin, (0, 2)))
    win = mm(w_in, layer=e, col_off=q_cols + kv_cols, n_out=2 * G * d, tn=tn, rope=(cos, sin, (0,)))
    gates = mm(w_gates, tn=G * LANE)
    rw = w_rest.shape[1] // 2
    u = mm(w_rest, n_out=rw)
    gb = mm(w_rest, col_off=rw, n_out=rw)
    return q, kv, win, gates, u, gb


def _pad_rows(a, n):
    return jnp.pad(a, ((0, n - a.shape[0]),) + ((0, 0),) * (a.ndim - 1))


def kernel(x_prompt, x_sample, mem_prompt, cache_nsa_kv, cache_nsa_win, state_rglru_h, state_rglru_conv,
           state_pool, cache_mem_kv, page_table, w_in, w_out, cmp_w1, cmp_w2, cmp_pe, conv_w, conv_b,
           lru_wa, lru_ba, lru_wx, lru_bx, lru_lambda, pool_w, pool_scale, mem_wq, mem_wk, mem_wv, mem_wo,
           ln_g, ln_b, moe_w_coarse, moe_b_coarse, moe_w_fine, moe_b_fine, moe_w_gate, moe_w_up, moe_w_down):
    B, T, D = x_prompt.shape
    DB = x_sample.shape[0]
    depth = ln_g.shape[0]
    n_pages = page_table.shape[1]
    past_len = n_pages * PAGE_SIZE
    n_phys = cache_nsa_kv.shape[1]
    G, R, d = NSA_G, NSA_R, HEAD_DIM
    alpha = (2 * depth) ** 0.25
    q_cols, kv_cols, win_cols, gate_cols = G * R * d, N_KV_STREAMS * G * d, 2 * G * d, 3 * G * R
    rest_off = q_cols + kv_cols + win_cols + gate_cols
    RW = (w_in.shape[2] - rest_off) // 2
    MW = mem_wq.shape[2]
    M = mem_prompt.shape[1]
    SR = SAMPLE_ROWS
    TS = SAMPLE_ROWS

    xp = x_prompt.reshape(B * T, D)
    xpb = xp.astype(BF16)
    xs = _pad_rows(x_sample.reshape(DB, D), SR)
    lng = ln_g.reshape(depth * 3, 1, D)
    lnb = ln_b.reshape(depth * 3, 1, D)
    memf = mem_prompt.reshape(B * M, D)

    kv_p, kv_s, win_p, win_s, h_p, h_s, cv_p, cv_s, pl_p, pl_s, mem_p = ([] for _ in range(11))
    for layer in range(depth):
        if layer % 2 == 0:
            e = layer // 2
            wg = w_in[e, :, q_cols + kv_cols + win_cols:rest_off].reshape(D, G, 3 * R)
            wg = jnp.pad(wg, ((0, 0), (0, 0), (0, LANE - 3 * R))).reshape(D, G * LANE)
            w_rest = w_in[e, :, rest_off:]

            q, kv, win, gates, u, gb = _even_projections(
                xpb, jnp.arange(T), w_in, w_rest, wg, e, 1024, BF16)
            pages_p = kv.reshape(B * T // PAGE_SIZE, PAGE_SIZE, N_KV_STREAMS, G, d)
            new_kv = pages_p.reshape(B, T, N_KV_STREAMS, G, d)
            ab = compress_ab(pages_p, jnp.arange(B * T // PAGE_SIZE, dtype=I32), cmp_w1, cmp_pe, e, B,
                             T // PAGE_SIZE)
            kvcmp = compress_fin(ab, cmp_w2, e)
            o_nsa = nsa_prompt(q, kv, win, kvcmp, gates, B, T)
            y_rnn, h_all = rglru(u, gb, jnp.zeros((B, SUBLANE, RW), F32), jnp.zeros((B, 1, RW), F32),
                                 conv_w, conv_b, lru_wa, lru_ba, lru_wx, lru_bx, lru_lambda, e, B, T)
            mix_p = matmul(o_nsa, w_out, layer=e, x2=y_rnn)
            keep = min(WINDOW, T)
            kv_p.append(new_kv)
            win_p.append(win.reshape(B, T, 2, G, d)[:, T - keep:])
            h_p.append(h_all.reshape(B, T, RW)[:, T - 1])
            cv_p.append(u.reshape(B, T, RW)[:, T - (CONV_WIDTH - 1):])

            q, kv, win, gates, u, gb = _even_projections(
                xs, jnp.full((SR,), past_len), w_in, w_rest, wg, e, SR, F32)
            new_kv_s = kv[:DB].reshape(DB, N_KV_STREAMS, G, d)
            new_win_s = win[:DB].reshape(DB, 2, G, d)
            pages_s = cache_nsa_kv.reshape(cache_nsa_kv.shape[0] * n_phys, PAGE_SIZE, N_KV_STREAMS, G, d)
            pid = (page_table.reshape(DB * n_pages) + e * n_phys).astype(I32)
            ab = compress_ab(pages_s, pid, cmp_w1, cmp_pe, e, DB, n_pages)
            kvcmp = compress_fin(ab, cmp_w2, e)
            q8 = jnp.pad(q[:DB].reshape(DB, G, R, d), ((0, 0), (0, 0), (0, SUBLANE - R), (0, 0)))
            o_cmp, imp = nsa_sample_cmp(q8, kvcmp, past_len)
            idx_t = nsa_sample_topk(_pad_rows(imp.reshape(DB * SUBLANE, -1), LANE), past_len)
            n_top = idx_t.shape[0]
            idx = idx_t[:, :DB * SUBLANE].T.reshape(DB, SUBLANE, n_top)[:, :G].reshape(DB * G * n_top)
            o8 = nsa_sample_attend(q8, o_cmp, gates[:DB].reshape(DB, G, 1, LANE), idx, pid, pages_s,
                                   new_kv_s, cache_nsa_win, new_win_s, e, past_len)
            o_nsa = _pad_rows(o8[:, :, :R].reshape(DB, G * R * d), SR)
            conv0 = state_rglru_conv[e]
            u_t = jnp.pad(u[:DB, None, :], ((0, 0), (0, TS - 1), (0, 0))).reshape(DB * TS, RW)
            gb_t = jnp.pad(gb[:DB, None, :], ((0, 0), (0, TS - 1), (0, 0))).reshape(DB * TS, RW)
            c0 = jnp.pad(conv0, ((0, 0), (SUBLANE - (CONV_WIDTH - 1), 0), (0, 0)))
            y_t, h_t = rglru(u_t, gb_t, c0, state_rglru_h[e][:, None, :], conv_w, conv_b, lru_wa, lru_ba,
                             lru_wx, lru_bx, lru_lambda, e, DB, TS)
            y_rnn = _pad_rows(y_t.reshape(DB, TS, RW)[:, 0], SR)
            mix_s = matmul(o_nsa, w_out, layer=e, x2=y_rnn)
            keep = min(WINDOW, past_len + 1)
            win_all = jnp.concatenate([cache_nsa_win[e], new_win_s[:, None]], axis=1)
            kv_s.append(new_kv_s[:, None])
            win_s.append(win_all[:, win_all.shape[1] - keep:])
            h_s.append(h_t.reshape(DB, TS, RW)[:, 0])
            cv_s.append(jnp.concatenate([conv0, u[:DB, None, :]], axis=1)[:, 1:])
        else:
            o = layer // 2
            mix_p = pool_mixer(xp, jnp.zeros((B, 2 * SUBLANE, D), F32), pool_w, pool_scale, o, B, T, 0)
            pl_p.append(xp.reshape(B, T, D)[:, T - POOL_BUF:])
            tsp = 2 * SUBLANE
            x_t = jnp.pad(xs[:DB, None, :], ((0, 0), (0, tsp - 1), (0, 0))).reshape(DB * tsp, D)
            buf = jnp.pad(state_pool[o], ((0, 0), (tsp - POOL_BUF, 0), (0, 0)))
            mix_t = pool_mixer(x_t, buf, pool_w, pool_scale, o, DB, tsp, past_len)
            mix_s = _pad_rows(mix_t.reshape(DB, tsp, D)[:, 0], SR)
            pl_s.append(jnp.concatenate([state_pool[o], xs[:DB, None, :]], axis=1)[:, 1:])

        xp, xpb = add_ln(xp, mix_p, lng, lnb, layer * 3, alpha)
        xs, _ = add_ln(xs, mix_s, lng, lnb, layer * 3, alpha)

        w_kv = jnp.concatenate([mem_wk[layer], mem_wv[layer]], axis=1)
        mkv = matmul(memf, w_kv).reshape(B, M, 2, MW)
        mem_p.append(mkv)
        qm = matmul(xpb, mem_wq, layer=layer, out_dtype=BF16)
        att = mem_attn(qm, mkv, B, T)
        xp, xpb, xp3 = add_ln(xp, matmul(att, mem_wo, layer=layer), lng, lnb, layer * 3 + 1, alpha,
                              token_major=True)
        qm = matmul(xs, mem_wq, layer=layer)
        qm_t = jnp.pad(qm[:DB, None, :], ((0, 0), (0, TS - 1), (0, 0))).reshape(DB * TS, MW)
        att_t = mem_attn(qm_t, cache_mem_kv.reshape((-1,) + cache_mem_kv.shape[2:]), DB, TS,
                         kv_off=layer * DB)
        att = _pad_rows(att_t.reshape(DB, TS, MW)[:, 0], SR)
        xs, _, xs3 = add_ln(xs, matmul(att, mem_wo, layer=layer), lng, lnb, layer * 3 + 1, alpha,
                            token_major=True)

        w_rt = _pad_rows(jnp.concatenate([moe_w_coarse[layer], moe_w_fine[layer]], axis=1).T, ROUTER_ROWS)
        b_rt = _pad_rows(jnp.concatenate([moe_b_coarse[layer], moe_b_fine[layer]])[:, None], ROUTER_ROWS)
        xp, xpb = hier_moe_ln(xp, xp3, B * T, w_rt, b_rt, moe_w_gate, moe_w_up, moe_w_down, layer, lng, lnb,
                              alpha, rows=256)
        xs = _pad_rows(hier_moe_ln(xs, xs3, DB, w_rt, b_rt, moe_w_gate, moe_w_up, moe_w_down, layer, lng, lnb,
                                   alpha, rows=SUBLANE)[0], SR)

    return (xp.reshape(B, T, D), xs[:DB].reshape(DB, 1, D), jnp.stack(kv_p), jnp.stack(kv_s),
            jnp.stack(win_p), jnp.stack(win_s), jnp.stack(h_p), jnp.stack(h_s), jnp.stack(cv_p),
            jnp.stack(cv_s), jnp.stack(pl_p), jnp.stack(pl_s), jnp.stack(mem_p))
```

```python
import functools

import jax
import jax.numpy as jnp
from jax import lax
from jax.experimental import pallas as pl
from jax.experimental.pallas import tpu as pltpu

F32 = jnp.float32
BF16 = jnp.bfloat16
I32 = jnp.int32

HEAD_DIM = 128
NSA_G = 4
NSA_R = 4
N_KV_STREAMS = 4
CMP_BLOCK = 32
CMP_STRIDE = 16
SEL_BLOCK = 64
SEL_TOP_N = 16
WINDOW = 512
FORCE_SCORE = 1e9
NEG = -1e30
ROPE_THETA = 10000.0
CONV_WIDTH = 4
LRU_C = 8.0
RNN_BLOCK_W = 128
POOL_WINDOWS = (2, 4, 8, 16)
POOL_BUF = 15
MEM_HEADS = 4
MOE_GROUPS = 4
MOE_EPG = 8
MOE_EXPERTS = MOE_GROUPS * MOE_EPG
MOE_TOP_K = 2
LN_EPS = 1e-5
PAGE_SIZE = 128

LANE = 128
SUBLANE = 8
VMEM_LIMIT = 56 * 1024 * 1024
SAMPLE_ROWS = 16


def _cp(sem, vmem=VMEM_LIMIT):
    return pltpu.CompilerParams(dimension_semantics=sem, vmem_limit_bytes=vmem)


def _split3(x):
    h = x.astype(BF16)
    r = x - h.astype(F32)
    m = r.astype(BF16)
    l = (r - m.astype(F32)).astype(BF16)
    return h, m, l


def _mxu(a, b, nt):
    if nt:
        return lax.dot_general(a, b, (((1,), (1,)), ((), ())), preferred_element_type=F32)
    return jnp.dot(a, b, preferred_element_type=F32)


def _dot(a, b, nt=False):
    return _mxu(a.astype(BF16), b.astype(BF16), nt)


def _bf16r(x):
    return x.astype(BF16).astype(F32)


def _sigmoid(x):
    return 1.0 / (1.0 + jnp.exp(-x))


def _gelu(x):
    return 0.5 * x * (1.0 + jnp.tanh(0.7978845608028654 * (x + 0.044715 * (x * x * x))))


def _msoftmax(s, mask):
    s = jnp.where(mask, s, NEG)
    m = jnp.max(s, axis=-1, keepdims=True)
    p = jnp.where(mask, jnp.exp(s - m), 0.0)
    return p / jnp.maximum(jnp.sum(p, axis=-1, keepdims=True), 1e-30)


def _msoftmax_nonempty(s, mask):
    s = jnp.where(mask, s, NEG)
    p = jnp.exp(s - jnp.max(s, axis=-1, keepdims=True))
    return p * (1.0 / jnp.sum(p, axis=-1, keepdims=True))


def _layer_norm(v, g, b):
    mu = jnp.mean(v, axis=-1, keepdims=True)
    c = v - mu
    var = jnp.mean(c * c, axis=-1, keepdims=True)
    return c * lax.rsqrt(var + LN_EPS) * g + b


def matmul(x, w, *, layer=0, col_off=0, n_out=None, x2=None, tm=1024, tn=512, rope=None, out_dtype=F32):
    if w.ndim == 2:
        w = w[None]
    M, K1 = x.shape
    K2 = 0 if x2 is None else x2.shape[1]
    assert w.shape[1] == K1 + K2
    n_out = w.shape[2] - col_off if n_out is None else n_out
    xs_in = [x] if x2 is None else [x, x2]
    if any(a.dtype != BF16 for a in xs_in):
        tm = min(tm, 512)
    tm, tn = min(tm, M), min(tn, n_out)
    assert M % tm == 0 and n_out % tn == 0 and col_off % tn == 0
    joff = col_off // tn

    in_specs = [pl.BlockSpec((tm, a.shape[1]), lambda i, j: (i, 0)) for a in xs_in]
    in_specs.append(pl.BlockSpec((None, K1 + K2, tn), lambda i, j: (layer, 0, j + joff)))
    args = xs_in + [w]
    rope_blocks = ()
    if rope is not None:
        cos, sin, rope_blocks = rope
        period = cos.shape[0] // tm
        assert cos.shape[0] % tm == 0
        for t in (cos, sin):
            in_specs.append(pl.BlockSpec((tm, LANE), lambda i, j: (i % period, 0)))
            args.append(t)
    cast = [a.dtype != BF16 for a in xs_in]
    scratch = [pltpu.VMEM((tm, a.shape[1]), BF16) for a, c in zip(xs_in, cast) if c]

    def kern(*refs):
        refs = list(refs)
        x_refs = [refs.pop(0) for _ in xs_in]
        w_ref = refs.pop(0)
        cos_ref, sin_ref = (refs.pop(0), refs.pop(0)) if rope is not None else (None, None)
        o_ref = refs.pop(0)
        j = pl.program_id(1)
        xb_refs = [refs.pop(0) if c else xr for xr, c in zip(x_refs, cast)]

        if any(cast):
            @pl.when(j == 0)
            def _():
                for xr, xb, c in zip(x_refs, xb_refs, cast):
                    if c:
                        xb[...] = xr[...].astype(BF16)

        acc = _mxu(xb_refs[0][...], w_ref[0:K1, :].astype(BF16), False)
        if x2 is not None:
            acc = acc + _mxu(xb_refs[1][...], w_ref[K1:K1 + K2, :].astype(BF16), False)
        if rope is None:
            o_ref[...] = acc.astype(out_dtype)
            return
        rot = functools.reduce(jnp.logical_or, [j == c for c in rope_blocks])

        @pl.when(rot)
        def _():
            c, s = cos_ref[...], sin_ref[...]
            for h in range(tn // LANE):
                seg = acc[:, h * LANE:(h + 1) * LANE]
                o_ref[:, h * LANE:(h + 1) * LANE] = (seg * c + pltpu.roll(seg, LANE // 2, 1) * s).astype(out_dtype)

        @pl.when(jnp.logical_not(rot))
        def _():
            o_ref[...] = acc.astype(out_dtype)

    return pl.pallas_call(
        kern,
        grid=(M // tm, n_out // tn),
        in_specs=in_specs,
        out_specs=pl.BlockSpec((tm, tn), lambda i, j: (i, j)),
        out_shape=jax.ShapeDtypeStruct((M, n_out), out_dtype),
        scratch_shapes=scratch,
        compiler_params=_cp(("parallel", "arbitrary")),
        name="matmul",
    )(*args)


def _token_major(v):
    return v.reshape(v.shape[0], v.shape[1] // LANE, LANE)


def add_ln(x, f, ln_g, ln_b, idx, alpha, tm=256, token_major=False):
    M, D = x.shape
    tm = min(tm, M)

    def kern(x_ref, f_ref, g_ref, b_ref, o_ref, ob_ref, *o3_ref):
        y = _layer_norm(alpha * x_ref[...] + f_ref[...], g_ref[...], b_ref[...])
        o_ref[...] = y
        ob_ref[...] = y.astype(BF16)
        if token_major:
            o3_ref[0][...] = _token_major(y)

    row = pl.BlockSpec((tm, D), lambda i: (i, 0))
    par = pl.BlockSpec((None, 1, D), lambda i: (idx, 0, 0))
    out_specs = [row, row]
    out_shape = [jax.ShapeDtypeStruct((M, D), F32), jax.ShapeDtypeStruct((M, D), BF16)]
    if token_major:
        out_specs.append(pl.BlockSpec((tm, D // LANE, LANE), lambda i: (i, 0, 0)))
        out_shape.append(jax.ShapeDtypeStruct((M, D // LANE, LANE), F32))
    return pl.pallas_call(
        kern, grid=(M // tm,), in_specs=[row, row, par, par], out_specs=out_specs,
        out_shape=out_shape, compiler_params=_cp(("parallel",)), name="add_ln",
    )(x, f, ln_g, ln_b)


CMP_PAGES = 8


def compress_ab(pages, page_ids, cmp_w1, cmp_pe, e, n_batch, pages_per_seq):
    P = CMP_PAGES
    assert pages_per_seq % P == 0
    cpp = PAGE_SIZE // CMP_STRIDE
    steps = pages_per_seq // P
    d, G = HEAD_DIM, NSA_G
    n_chunks = pages_per_seq * cpp
    rows = P * cpp * G

    def kern(pid_ref, *refs):
        page_refs, w_ref, pe_ref, o_ref = refs[:P], refs[P], refs[P + 1], refs[P + 2]
        for s in range(2):
            acc_a = jnp.zeros((rows, d), F32)
            acc_b = jnp.zeros((rows, d), F32)

            def chunk_rows(l):
                return jnp.concatenate(
                    [page_refs[i][pl.ds(l, cpp, stride=CMP_STRIDE), s].reshape(cpp * G, d) for i in range(P)],
                    axis=0)

            def pair(l, off):
                lhs = jnp.concatenate([rows2[0] + pe_ref[s, off + l:off + l + 1, :],
                                       rows2[1] + pe_ref[s, off + l + 1:off + l + 2, :]], axis=1)
                return _dot(lhs, w_ref[s, off + l:off + l + 2].reshape(2 * d, d))

            for l in range(0, CMP_STRIDE, 2):
                rows2 = (chunk_rows(l), chunk_rows(l + 1))
                acc_a = acc_a + pair(l, 0)
                acc_b = acc_b + pair(l, CMP_STRIDE)
            o_ref[:, s * 2 * d:s * 2 * d + d] = acc_a
            o_ref[:, s * 2 * d + d:(s + 1) * 2 * d] = acc_b

    def page_spec(i):
        return pl.BlockSpec((None, PAGE_SIZE, 2, G, d),
                            lambda b, st, pid: (pid[(b * steps + st) * P + i], 0, 0, 0, 0))

    gs = pltpu.PrefetchScalarGridSpec(
        num_scalar_prefetch=1,
        grid=(n_batch, steps),
        in_specs=[page_spec(i) for i in range(P)]
        + [pl.BlockSpec((None, 2, CMP_BLOCK, d, d), lambda b, st, pid: (e, 0, 0, 0, 0)),
           pl.BlockSpec((None, 2, CMP_BLOCK, d), lambda b, st, pid: (e, 0, 0, 0))],
        out_specs=pl.BlockSpec((None, rows, 2 * 2 * d), lambda b, st, pid: (b, st, 0)),
    )
    return pl.pallas_call(
        kern, grid_spec=gs,
        out_shape=jax.ShapeDtypeStruct((n_batch, n_chunks * G, 2 * 2 * d), F32),
        compiler_params=_cp(("parallel", "parallel")), name="compress_ab",
    )(page_ids, *([pages] * P), cmp_w1, cmp_pe)


def compress_fin(ab, cmp_w2, e):
    nb, nrow, _ = ab.shape
    d, G = HEAD_DIM, NSA_G
    nch = nrow // G

    def kern(ab_ref, w2_ref, o_ref):
        h = ab_ref[:, 0:d] + pltpu.roll(ab_ref[:, d:2 * d], nrow - G, 0)
        res = _dot(_gelu(h), w2_ref[...]).reshape(nch, G, d)
        for g in range(G):
            o_ref[g] = res[:, g, :]

    return pl.pallas_call(
        kern, grid=(nb, 2),
        in_specs=[pl.BlockSpec((None, nrow, 2 * d), lambda b, s: (b, 0, s)),
                  pl.BlockSpec((None, None, d, d), lambda b, s: (e, s, 0, 0))],
        out_specs=pl.BlockSpec((None, None, NSA_G, nch, d), lambda b, s: (s, b, 0, 0, 0)),
        out_shape=jax.ShapeDtypeStruct((2, nb, NSA_G, nch, d), F32),
        compiler_params=_cp(("parallel", "parallel")), name="compress_fin",
    )(ab, cmp_w2)


def _overlap_matrix(n_rows, n_cols):
    ci = jnp.arange(n_rows)[:, None] * CMP_STRIDE
    sj = jnp.arange(n_cols)[None, :] * SEL_BLOCK
    return ((ci < sj + SEL_BLOCK) & (ci + CMP_BLOCK > sj)).astype(BF16)


NSA_TQ = 128


def nsa_prompt(q, kv, win, kvcmp, gates, B, T):
    d, G, R, tq = HEAD_DIM, NSA_G, NSA_R, NSA_TQ
    nq = T // tq
    ncp = kvcmp.shape[3]
    n_cmp = T // CMP_STRIDE - 1
    n_sel = -(-T // SEL_BLOCK)
    n_top = min(SEL_TOP_N, n_sel)
    wk = min(T, WINDOW + tq)
    key_span = min(T, 2 * tq)
    assert n_sel <= LANE and T % tq == 0 and T % key_span == 0
    scale = d ** -0.5
    overlap = _overlap_matrix(ncp, LANE)
    expand = (jnp.arange(LANE)[:, None] == (jnp.arange(T) // SEL_BLOCK)[None, :]).astype(BF16)

    def kern(q_ref, ks_ref, vs_ref, kw_ref, vw_ref, kc_ref, vc_ref, g_ref, ov_ref, ex_ref, o_ref,
             ksb, vsb, kwb, vwb, kcb, vcb):
        qi = pl.program_id(2)

        @pl.when(qi == 0)
        def _():
            ksb[...] = ks_ref[...].astype(BF16)
            vsb[...] = vs_ref[...].astype(BF16)
            kwb[...] = kw_ref[...].astype(BF16)
            vwb[...] = vw_ref[...].astype(BF16)
            kcb[...] = kc_ref[...].astype(BF16)
            vcb[...] = vc_ref[...].astype(BF16)

        t0 = qi * tq
        qb = q_ref[...]
        qs = jnp.concatenate([qb[:, r * d:(r + 1) * d] for r in range(R)], axis=0).astype(BF16)
        qpos_s = t0 + (lax.broadcasted_iota(I32, (R * tq, 1), 0) & (tq - 1))

        s = _mxu(qs, kcb[...], True) * scale
        n_idx = lax.broadcasted_iota(I32, (1, ncp), 1)
        cmask = (n_idx * CMP_STRIDE + (CMP_BLOCK - 1) <= qpos_s) & (n_idx < n_cmp)
        p = _msoftmax(s, cmask)
        pb = p.astype(BF16)
        o_cmp = _mxu(pb, vcb[...], False)
        imp_r = _mxu(pb, ov_ref[...], False)
        imp = imp_r[0:tq]
        for r in range(1, R):
            imp = imp + imp_r[r * tq:(r + 1) * tq]

        blk = lax.broadcasted_iota(I32, (tq, LANE), 1)
        qpos = t0 + lax.broadcasted_iota(I32, (tq, LANE), 0)
        cur = qpos // SEL_BLOCK
        valid = blk <= cur
        forced = (blk == 0) | (blk == cur) | (blk == cur - 1)
        score = jnp.where(forced, FORCE_SCORE, jnp.where(valid, imp, NEG))
        cnt = jnp.zeros((tq, LANE), I32)
        for i in range(n_sel):
            col = score[:, i:i + 1]
            beats = (col > score) | ((col == score) & (blk > i))
            cnt = cnt + beats.astype(I32)
        sel = ((cnt < n_top) & valid).astype(BF16)

        kstart = pl.multiple_of(jnp.clip(t0 - WINDOW, 0, T - wk), tq)
        kwin = kwb[pl.ds(kstart, wk), :]
        vwin = vwb[pl.ds(kstart, wk), :]
        s_w = _mxu(qs, kwin, True) * scale
        kp = kstart + lax.broadcasted_iota(I32, (1, wk), 1)
        wmask = (kp <= qpos_s) & (kp > qpos_s - WINDOW)
        p_w = _msoftmax_nonempty(s_w, wmask)
        o_win = _mxu(p_w.astype(BF16), vwin, False)

        gt = _sigmoid(g_ref[...])

        def selected(nkeys):
            selx = _mxu(sel, ex_ref[:, 0:nkeys], False)
            kpos = lax.broadcasted_iota(I32, (tq, nkeys), 1)
            qpos_t = t0 + lax.broadcasted_iota(I32, (tq, nkeys), 0)
            smask = (selx > 0.5) & (kpos <= qpos_t)
            for r in range(R):
                s_r = _mxu(qs[r * tq:(r + 1) * tq], ksb[0:nkeys, :], True) * scale
                p_r = _msoftmax_nonempty(s_r, smask)
                o_slc = _mxu(p_r.astype(BF16), vsb[0:nkeys, :], False)
                o_r = (gt[:, 3 * r:3 * r + 1] * o_cmp[r * tq:(r + 1) * tq]
                       + gt[:, 3 * r + 1:3 * r + 2] * o_slc
                       + gt[:, 3 * r + 2:3 * r + 3] * o_win[r * tq:(r + 1) * tq])
                o_ref[:, r * d:(r + 1) * d] = o_r.astype(BF16)

        n_span = pl.cdiv(t0 + tq, key_span)
        for n in range(1, T // key_span + 1):
            pl.when(n_span == n)(functools.partial(selected, n * key_span))

    seq = lambda blkcol: pl.BlockSpec((T, d), lambda b, g, qi: (b, blkcol(g)))
    cmp_spec = lambda s: pl.BlockSpec((None, None, None, ncp, d), lambda b, g, qi: (s, b, g, 0, 0))
    return pl.pallas_call(
        kern, grid=(B, G, nq),
        in_specs=[pl.BlockSpec((tq, R * d), lambda b, g, qi: (b * nq + qi, g)),
                  seq(lambda g: 2 * G + g), seq(lambda g: 3 * G + g),
                  seq(lambda g: g), seq(lambda g: G + g),
                  cmp_spec(0), cmp_spec(1),
                  pl.BlockSpec((tq, LANE), lambda b, g, qi: (b * nq + qi, g)),
                  pl.BlockSpec((ncp, LANE), lambda b, g, qi: (0, 0)),
                  pl.BlockSpec((LANE, T), lambda b, g, qi: (0, 0))],
        out_specs=pl.BlockSpec((tq, R * d), lambda b, g, qi: (b * nq + qi, g)),
        out_shape=jax.ShapeDtypeStruct((B * T, G * R * d), BF16),
        scratch_shapes=[pltpu.VMEM((T, d), BF16)] * 4 + [pltpu.VMEM((ncp, d), BF16)] * 2,
        compiler_params=_cp(("parallel", "parallel", "arbitrary")), name="nsa_prompt",
    )(q, kv, kv, win, win, kvcmp, kvcmp, gates, overlap, expand)


def nsa_sample_cmp(q8, kvcmp, q_pos):
    B = q8.shape[0]
    d, G, R = HEAD_DIM, NSA_G, NSA_R
    nch = kvcmp.shape[3]
    n_cmp = nch - 1
    n_sel = -(-(q_pos + 1) // SEL_BLOCK)
    nsp = -(-n_sel // LANE) * LANE
    overlap = _overlap_matrix(nch, nsp)
    scale = d ** -0.5

    def kern(q_ref, k_ref, v_ref, ov_ref, o_ref, imp_ref):
        n_idx = lax.broadcasted_iota(I32, (1, nch), 1)
        cmask = (n_idx * CMP_STRIDE + (CMP_BLOCK - 1) <= q_pos) & (n_idx < n_cmp)
        rows = []
        for g in range(G):
            s = _dot(q_ref[g], k_ref[g], nt=True) * scale
            pb = _msoftmax(s, cmask).astype(BF16)
            o_ref[g] = _dot(pb, v_ref[g])
            rows.append(jnp.sum(_mxu(pb, ov_ref[...], False)[0:R], axis=0, keepdims=True))
        imp_ref[...] = jnp.concatenate(rows + [jnp.zeros((SUBLANE - G, nsp), F32)], axis=0)

    qspec = pl.BlockSpec((None, G, SUBLANE, d), lambda b: (b, 0, 0, 0))
    cspec = lambda s: pl.BlockSpec((None, None, G, nch, d), lambda b: (s, b, 0, 0, 0))
    return pl.pallas_call(
        kern, grid=(B,),
        in_specs=[qspec, cspec(0), cspec(1), pl.BlockSpec((nch, nsp), lambda b: (0, 0))],
        out_specs=[qspec, pl.BlockSpec((None, SUBLANE, nsp), lambda b: (b, 0, 0))],
        out_shape=[jax.ShapeDtypeStruct((B, G, SUBLANE, d), F32),
                   jax.ShapeDtypeStruct((B, SUBLANE, nsp), F32)],
        compiler_params=_cp(("parallel",)), name="nsa_sample_cmp",
    )(q8, kvcmp, kvcmp, overlap)


def nsa_sample_topk(imp, q_pos):
    nr, nsp = imp.shape
    assert nr == LANE
    n_sel = -(-(q_pos + 1) // SEL_BLOCK)
    n_top = min(SEL_TOP_N, n_sel)
    cur = q_pos // SEL_BLOCK

    def kern(imp_ref, o_ref, sct):
        blk = lax.broadcasted_iota(I32, (nr, nsp), 1)
        valid = blk <= cur
        forced = (blk == 0) | (blk == cur) | (blk == cur - 1)
        score = jnp.where(forced, FORCE_SCORE, jnp.where(valid, imp_ref[...], NEG))
        eye = (lax.broadcasted_iota(I32, (nsp, nsp), 0) == lax.broadcasted_iota(I32, (nsp, nsp), 1)).astype(BF16)
        h, m, l = _split3(score)
        sct[...] = (_mxu(eye, h, True) + _mxu(eye, m, True)) + _mxu(eye, l, True)
        sc = sct[...]
        jidx = lax.broadcasted_iota(I32, (nsp, nr), 0)

        def body(i, cnt):
            row = sct[pl.ds(i, 1), :]
            beats = (row > sc) | ((row == sc) & (jidx > i))
            return cnt + beats.astype(I32)

        cnt = lax.fori_loop(0, n_sel, body, jnp.zeros((nsp, nr), I32))
        for p in range(n_top):
            o_ref[p:p + 1, :] = jnp.sum(jnp.where(cnt == p, jidx, 0), axis=0, keepdims=True)

    return pl.pallas_call(
        kern, grid=(1,),
        in_specs=[pl.BlockSpec((nr, nsp), lambda i: (0, 0))],
        out_specs=pl.BlockSpec((n_top, nr), lambda i: (0, 0)),
        out_shape=jax.ShapeDtypeStruct((n_top, nr), I32),
        scratch_shapes=[pltpu.VMEM((nsp, nr), F32)],
        compiler_params=_cp(("arbitrary",)), name="nsa_sample_topk",
    )(imp)


def nsa_sample_attend(q8, o_cmp, gates, idx, page_ids, pages, new_kv, cwin, new_win, e, q_pos):
    B = q8.shape[0]
    d, G, R = HEAD_DIM, NSA_G, NSA_R
    n_top = idx.shape[0] // (B * G)
    n_pages = page_ids.shape[0] // B
    n_cache_blocks = n_pages * (PAGE_SIZE // SEL_BLOCK)
    bpp = PAGE_SIZE // SEL_BLOCK
    w0 = cwin.shape[2]
    scale = d ** -0.5
    nk = n_top * SEL_BLOCK

    def kern(idx_ref, pid_ref, q_ref, oc_ref, g_ref, pages_ref, nkv_ref, cw_ref, nw_ref, o_ref,
             kvbuf, sem):
        b = pl.program_id(0)

        def slot_copy(slot, j):
            page = pid_ref[b * n_pages + j // bpp]
            r0 = (j % bpp) * SEL_BLOCK
            return pltpu.make_async_copy(
                pages_ref.at[page, pl.ds(r0, SEL_BLOCK), pl.ds(2, 2)], kvbuf.at[slot], sem.at[0])

        for slot in range(G * n_top):
            j = idx_ref[b * G * n_top + slot]

            @pl.when(j < n_cache_blocks)
            def _():
                slot_copy(slot, j).start()

            @pl.when(j >= n_cache_blocks)
            def _():
                kvbuf[slot] = jnp.zeros((SEL_BLOCK, 2, G, d), F32)

            @pl.when(j * SEL_BLOCK == q_pos)
            def _():
                kvbuf[slot, 0] = nkv_ref[2:4]

        for slot in range(G * n_top):
            j = idx_ref[b * G * n_top + slot]

            @pl.when(j < n_cache_blocks)
            def _():
                slot_copy(slot, j).wait()

        lane_k = lax.broadcasted_iota(I32, (1, nk), 1)
        row8 = lax.broadcasted_iota(I32, (SUBLANE, 1), 0)
        for g in range(G):
            qg = q_ref[g]
            kk = jnp.concatenate([kvbuf[g * n_top + p, :, 0, g, :] for p in range(n_top)], axis=0)
            vv = jnp.concatenate([kvbuf[g * n_top + p, :, 1, g, :] for p in range(n_top)], axis=0)
            kpos = lane_k & (SEL_BLOCK - 1)
            for p in range(n_top):
                j = idx_ref[(b * G + g) * n_top + p]
                kpos = kpos + jnp.where(lane_k // SEL_BLOCK == p, j * SEL_BLOCK, 0)
            s = _dot(qg, kk, nt=True) * scale
            p_s = _msoftmax(s, kpos <= q_pos)
            o_slc = _dot(p_s, vv)
            kw = cw_ref[:, 0, g, :]
            vw = cw_ref[:, 1, g, :]
            s_w = _dot(qg, kw, nt=True) * scale
            wpos = (q_pos - w0) + lax.broadcasted_iota(I32, (1, w0), 1)
            wmask = (wpos >= 0) & (wpos <= q_pos) & (wpos > q_pos - WINDOW)
            s_w = jnp.where(wmask, s_w, NEG)
            s_n = jnp.sum(_bf16r(qg) * _bf16r(nw_ref[0, g:g + 1, :]), axis=-1, keepdims=True) * scale
            m = jnp.maximum(jnp.max(s_w, axis=-1, keepdims=True), s_n)
            p_w = jnp.where(wmask, jnp.exp(s_w - m), 0.0)
            p_n = jnp.exp(s_n - m)
            den = jnp.sum(p_w, axis=-1, keepdims=True) + p_n
            o_win = _dot(p_w / den, vw) + _bf16r(p_n / den) * _bf16r(nw_ref[1, g:g + 1, :])
            gt = _sigmoid(g_ref[g])
            gc = [jnp.zeros((SUBLANE, 1), F32)] * 3
            for r in range(R):
                for c in range(3):
                    gc[c] = jnp.where(row8 == r, gt[:, 3 * r + c:3 * r + c + 1], gc[c])
            o_ref[g] = gc[0] * oc_ref[g] + gc[1] * o_slc + gc[2] * o_win

    qspec = pl.BlockSpec((None, G, SUBLANE, d), lambda b, i, p: (b, 0, 0, 0))
    gs = pltpu.PrefetchScalarGridSpec(
        num_scalar_prefetch=2, grid=(B,),
        in_specs=[qspec, qspec,
                  pl.BlockSpec((None, G, 1, LANE), lambda b, i, p: (b, 0, 0, 0)),
                  pl.BlockSpec(memory_space=pl.ANY),
                  pl.BlockSpec((None, N_KV_STREAMS, G, d), lambda b, i, p: (b, 0, 0, 0)),
                  pl.BlockSpec((None, None, w0, 2, G, d), lambda b, i, p: (e, b, 0, 0, 0, 0)),
                  pl.BlockSpec((None, 2, G, d), lambda b, i, p: (b, 0, 0, 0))],
        out_specs=qspec,
        scratch_shapes=[pltpu.VMEM((G * n_top, SEL_BLOCK, 2, G, d), F32),
                        pltpu.SemaphoreType.DMA((1,))],
    )
    return pl.pallas_call(
        kern, grid_spec=gs, out_shape=jax.ShapeDtypeStruct((B, G, SUBLANE, d), F32),
        compiler_params=_cp(("arbitrary",)), name="nsa_sample_attend",
    )(idx, page_ids, q8, o_cmp, gates, pages, new_kv, cwin, new_win)


def rglru(u, gb, conv0, h0, conv_w, conv_b, lru_wa, lru_ba, lru_wx, lru_bx, lru_lambda, e, B, T):
    W = u.shape[1]
    cw = RNN_BLOCK_W
    nblk = W // cw
    tc = min(T, 256)
    nt = T // tc
    assert T % tc == 0 and tc % SUBLANE == 0

    def kern(u_ref, gb_ref, c0_ref, h0_ref, cw_ref, cb_ref, wa_ref, ba_ref, wx_ref, bx_ref, lam_ref,
             y_ref, h_ref, prev, hprev):
        ti = pl.program_id(2)

        @pl.when(ti == 0)
        def _():
            prev[...] = c0_ref[...]
            hprev[...] = h0_ref[...]

        uu = u_ref[...]
        ext = _bf16r(jnp.concatenate([prev[...], uu], axis=0))
        prev[...] = uu[tc - SUBLANE:tc]
        cwr = _bf16r(cw_ref[...])
        tap0 = SUBLANE - (CONV_WIDTH - 1)
        xc = cwr[0:1, :] * ext[tap0:tap0 + tc]
        for j in range(1, CONV_WIDTH):
            xc = xc + cwr[j:j + 1, :] * ext[tap0 + j:tap0 + j + tc]
        xc = xc + cb_ref[...]
        r = _sigmoid(_dot(xc, wa_ref[...]) + ba_ref[...])
        i = _sigmoid(_dot(xc, wx_ref[...]) + bx_ref[...])
        nl = -lam_ref[...]
        softplus = jnp.maximum(nl, 0.0) + jnp.log1p(jnp.exp(-jnp.abs(nl)))
        log_a = -LRU_C * r * softplus
        a = jnp.exp(log_a)
        bt = jnp.sqrt(-jnp.tanh(log_a) * (a * a + 1.0)) * (i * xc)
        rows = lax.broadcasted_iota(I32, (tc, 1), 0)
        step = 1
        while step < tc:
            keep = rows >= step
            bt = jnp.where(keep, a * pltpu.roll(bt, step, 0) + bt, bt)
            a = jnp.where(keep, a * pltpu.roll(a, step, 0), a)
            step *= 2
        h = a * hprev[...] + bt
        hprev[...] = h[tc - 1:tc]
        h_ref[...] = h
        y_ref[...] = (h * _gelu(gb_ref[...])).astype(BF16)

    tile = pl.BlockSpec((tc, cw), lambda b, c, t: (b * nt + t, c))
    vec = lambda: pl.BlockSpec((None, 1, cw), lambda b, c, t: (e, 0, c))
    mat = lambda: pl.BlockSpec((None, None, cw, cw), lambda b, c, t: (e, c, 0, 0))
    bvec = lambda: pl.BlockSpec((None, None, 1, cw), lambda b, c, t: (e, c, 0, 0))
    return pl.pallas_call(
        kern, grid=(B, nblk, nt),
        in_specs=[tile, tile,
                  pl.BlockSpec((None, SUBLANE, cw), lambda b, c, t: (b, 0, c)),
                  pl.BlockSpec((None, 1, cw), lambda b, c, t: (b, 0, c)),
                  pl.BlockSpec((None, CONV_WIDTH, cw), lambda b, c, t: (e, 0, c)),
                  vec(), mat(), bvec(), mat(), bvec(), vec()],
        out_specs=[tile, tile],
        out_shape=[jax.ShapeDtypeStruct((B * T, W), BF16), jax.ShapeDtypeStruct((B * T, W), F32)],
        scratch_shapes=[pltpu.VMEM((SUBLANE, cw), F32), pltpu.VMEM((1, cw), F32)],
        compiler_params=_cp(("parallel", "parallel", "arbitrary")), name="rglru",
    )(u, gb, conv0, h0, conv_w, conv_b[:, None, :], lru_wa, lru_ba[:, :, None, :], lru_wx,
      lru_bx[:, :, None, :], lru_lambda[:, None, :])


def pool_mixer(x, buf16, pool_w, pool_scale, o, B, T, t0):
    D = x.shape[1]
    ng = len(POOL_WINDOWS)
    gw = D // ng
    tt = min(T, 256)
    nt = T // tt
    halo = 2 * SUBLANE
    assert T % tt == 0 and tt >= halo and POOL_WINDOWS == (2, 4, 8, 16)

    def kern(x_ref, buf_ref, w_ref, sc_ref, o_ref, prev):
        g, ti = pl.program_id(0), pl.program_id(2)

        @pl.when(ti == 0)
        def _():
            prev[...] = buf_ref[...]

        xx = x_ref[...]
        ext = jnp.concatenate([prev[...], xx], axis=0)
        prev[...] = xx[tt - halo:tt]
        s2 = ext + pltpu.roll(ext, 1, 0)
        s4 = s2 + pltpu.roll(s2, 2, 0)
        s8 = s4 + pltpu.roll(s4, 4, 0)
        s16 = s8 + pltpu.roll(s8, 8, 0)
        sw = jnp.where(g == 0, s2, jnp.where(g == 1, s4, jnp.where(g == 2, s8, s16)))[halo:]
        wlen = jnp.left_shift(2, g)
        pos = t0 + ti * tt + lax.broadcasted_iota(I32, (tt, 1), 0)
        cnt = jnp.minimum(wlen, pos + 1).astype(F32)
        pooled = sw / cnt - xx
        o_ref[...] = _dot(pooled, w_ref[...]) * sc_ref[...]

    return pl.pallas_call(
        kern, grid=(ng, B, nt),
        in_specs=[pl.BlockSpec((tt, gw), lambda g, b, t: (b * nt + t, g)),
                  pl.BlockSpec((None, halo, gw), lambda g, b, t: (b, 0, g)),
                  pl.BlockSpec((None, None, gw, gw), lambda g, b, t: (o, g, 0, 0)),
                  pl.BlockSpec((None, 1, gw), lambda g, b, t: (o, 0, g))],
        out_specs=pl.BlockSpec((tt, gw), lambda g, b, t: (b * nt + t, g)),
        out_shape=jax.ShapeDtypeStruct((B * T, D), F32),
        scratch_shapes=[pltpu.VMEM((halo, gw), F32)],
        compiler_params=_cp(("parallel", "parallel", "arbitrary")), name="pool_mixer",
    )(x, buf16, pool_w, pool_scale[:, None, :])


def mem_attn(q, mkv, B, T, kv_off=0):
    MW = q.shape[1]
    M = mkv.shape[1]
    dh = MW // MEM_HEADS
    tq = min(T, 256)
    nq = T // tq
    scale = dh ** -0.5

    def kern(q_ref, kv_ref, o_ref):
        for h in range(MEM_HEADS):
            sl = slice(h * dh, (h + 1) * dh)
            s = _dot(q_ref[:, sl], kv_ref[:, 0, sl], nt=True) * scale
            m = jnp.max(s, axis=-1, keepdims=True)
            p = jnp.exp(s - m)
            p = p / jnp.sum(p, axis=-1, keepdims=True)
            o_ref[:, sl] = _dot(p, kv_ref[:, 1, sl]).astype(BF16)

    return pl.pallas_call(
        kern, grid=(B, nq),
        in_specs=[pl.BlockSpec((tq, MW), lambda b, i: (b * nq + i, 0)),
                  pl.BlockSpec((None, M, 2, MW), lambda b, i: (kv_off + b, 0, 0, 0))],
        out_specs=pl.BlockSpec((tq, MW), lambda b, i: (b * nq + i, 0)),
        out_shape=jax.ShapeDtypeStruct((B * T, MW), BF16),
        compiler_params=_cp(("parallel", "parallel")), name="mem_attn",
    )(q, mkv)


ROUTER_ROWS = 40


def moe_router(x, w_t, bias, tm=512):
    N, D = x.shape
    tm = min(tm, N)
    assert N % tm == 0 and tm % LANE == 0

    def kern(x_ref, w_ref, b_ref, e_ref, g_ref):
        lt = _dot(w_ref[...], x_ref[...], nt=True) + b_ref[...]
        c = [lt[i:i + 1] for i in range(MOE_GROUPS)]
        m = functools.reduce(jnp.maximum, c)
        ex = [jnp.exp(ci - m) for ci in c]
        tot = functools.reduce(jnp.add, ex)
        pc = [ei / tot for ei in ex]
        pg = functools.reduce(jnp.maximum, pc)
        grp = jnp.full(pg.shape, MOE_GROUPS - 1, I32)
        for i in range(MOE_GROUPS - 2, -1, -1):
            grp = jnp.where(pc[i] == pg, i, grp)
        lf = []
        for j in range(MOE_EPG):
            v = lt[MOE_GROUPS + j:MOE_GROUPS + j + 1]
            for gi in range(1, MOE_GROUPS):
                r0 = MOE_GROUPS + gi * MOE_EPG + j
                v = jnp.where(grp == gi, lt[r0:r0 + 1], v)
            lf.append(v)
        m = functools.reduce(jnp.maximum, lf)
        ex = [jnp.exp(v - m) for v in lf]
        tot = functools.reduce(jnp.add, ex)
        pf = [ei / tot for ei in ex]

        def first_max(vals):
            best = functools.reduce(jnp.maximum, vals)
            arg = jnp.full(best.shape, len(vals) - 1, I32)
            for j in range(len(vals) - 2, -1, -1):
                arg = jnp.where(vals[j] == best, j, arg)
            return best, arg

        p1, j1 = first_max(pf)
        p2, j2 = first_max([jnp.where(j1 == j, -1.0, pf[j]) for j in range(MOE_EPG)])
        den = p1 + p2
        e_ref[0:1, :] = grp * MOE_EPG + j1
        e_ref[1:2, :] = grp * MOE_EPG + j2
        g_ref[0:1, :] = pg * p1 / den
        g_ref[1:2, :] = pg * p2 / den

    out = pl.BlockSpec((MOE_TOP_K, tm), lambda i: (0, i))
    return pl.pallas_call(
        kern, grid=(N // tm,),
        in_specs=[pl.BlockSpec((tm, D), lambda i: (i, 0)),
                  pl.BlockSpec((ROUTER_ROWS, D), lambda i: (0, 0)),
                  pl.BlockSpec((ROUTER_ROWS, 1), lambda i: (0, 0))],
        out_specs=[out, out],
        out_shape=[jax.ShapeDtypeStruct((MOE_TOP_K, N), I32), jax.ShapeDtypeStruct((MOE_TOP_K, N), F32)],
        compiler_params=_cp(("parallel",)), name="moe_router",
    )(x, w_t, bias)


MOE_SUB = 2
MOE_TF = 256
MOE_VMEM = 60 * 1024 * 1024


def moe_experts_grouped(x3, tok, aid, nvalid, gexp, nsub, nused, w_gate, w_up, w_down, layer, rows, n_assign):
    _, nc, _ = x3.shape
    D = nc * LANE
    SB, tf = MOE_SUB, MOE_TF
    n_groups = tok.shape[0] // (SB * rows)
    nf = w_gate.shape[3] // tf
    assert nf >= 2 and SB == 2

    def kern(tok_ref, aid_ref, nv_ref, gexp_ref, nsub_ref, nused_ref, x3_ref, wg_ref, wu_ref, wd_ref, y_ref,
             xbuf, xb, acc, ybuf, sem_in, sem_out):
        g, f = pl.program_id(0), pl.program_id(1)
        nused = nused_ref[0]
        active = g < nused
        ns = nsub_ref[g]

        def gather_start(grp, j):
            base = (grp * SB + j) * rows

            def body(r, carry):
                pltpu.make_async_copy(x3_ref.at[tok_ref[base + r]], xbuf.at[r], sem_in.at[0]).start()
                return carry
            lax.fori_loop(0, rows, body, 0, unroll=SUBLANE)

        def gather_wait():
            pltpu.make_async_copy(x3_ref.at[pl.ds(0, rows)], xbuf, sem_in.at[0]).wait()

        def scatter_start(grp, j):
            base = (grp * SB + j) * rows

            def row(r, priority):
                pltpu.make_async_copy(ybuf.at[j, r], y_ref.at[aid_ref[base + r]], sem_out.at[j]).start(
                    priority=priority)

            def body(i, carry):
                row(2 * i, 0)
                row(2 * i + 1, 1)
                return carry
            nv = nv_ref[grp * SB + j]
            lax.fori_loop(0, nv // 2, body, 0)
            pl.when(nv % 2 == 1)(lambda: row(nv - 1, 0))

        def scatter_wait(grp, j):
            nv = nv_ref[grp * SB + j]

            @pl.when(nv > 0)
            def _():
                pltpu.make_async_copy(ybuf.at[j, pl.ds(0, nv)], y_ref.at[pl.ds(0, nv)], sem_out.at[j]).wait()

        @pl.when(active & (f == 0))
        def _():
            pl.when(g == 0)(lambda: gather_start(0, 0))
            for j in range(SB):
                @pl.when(j < ns)
                def _():
                    if j > 0:
                        gather_start(g, j)
                    gather_wait()
                    xb[j * rows:(j + 1) * rows, :] = xbuf[...].reshape(rows, D).astype(BF16)
            acc[...] = jnp.zeros_like(acc)

        pl.when((f == nf - 2) & (g + 1 < nused))(lambda: gather_start(g + 1, 0))

        @pl.when(active)
        def _():
            wg, wu, wd = wg_ref[...].astype(BF16), wu_ref[...].astype(BF16), wd_ref[...].astype(BF16)
            for j in range(SB):
                @pl.when(j < ns)
                def _():
                    x = xb[j * rows:(j + 1) * rows, :]
                    hg = _mxu(x, wg, False)
                    hu = _mxu(x, wu, False)
                    hid = (hg * _sigmoid(hg) * hu).astype(BF16)
                    acc[j * rows:(j + 1) * rows, :] += _mxu(hid, wd, False)

        @pl.when(active & (f == nf - 1))
        def _():
            for j in range(SB):
                pl.when((g >= 1) & (j < nsub_ref[jnp.maximum(g - 1, 0)]))(
                    functools.partial(scatter_wait, jnp.maximum(g - 1, 0), j))

                @pl.when(j < ns)
                def _():
                    ybuf[j] = _token_major(acc[j * rows:(j + 1) * rows, :])
                    scatter_start(g, j)

            @pl.when(g == nused - 1)
            def _():
                for j in range(SB):
                    pl.when(j < ns)(functools.partial(scatter_wait, g, j))

    def live(g, nu):
        return jnp.maximum(jnp.minimum(g, nu[0] - 1), 0)

    def feff(g, f, nu):
        return jnp.where(g < nu[0], f, nf - 1)

    def wspec(shape, col):
        def imap(g, f, t, a, nv, ge, nsb, nu):
            e, ff = ge[live(g, nu)], feff(g, f, nu)
            return (layer, e, 0, ff) if col else (layer, e, ff, 0)
        return pl.BlockSpec(shape, imap)

    gs = pltpu.PrefetchScalarGridSpec(
        num_scalar_prefetch=6, grid=(n_groups, nf),
        in_specs=[pl.BlockSpec(memory_space=pl.ANY),
                  wspec((None, None, D, tf), True), wspec((None, None, D, tf), True),
                  wspec((None, None, tf, D), False)],
        out_specs=pl.BlockSpec(memory_space=pl.ANY),
        scratch_shapes=[pltpu.VMEM((rows, nc, LANE), F32), pltpu.VMEM((SB * rows, D), BF16),
                        pltpu.VMEM((SB * rows, D), F32), pltpu.VMEM((SB, rows, nc, LANE), F32),
                        pltpu.SemaphoreType.DMA((1,)), pltpu.SemaphoreType.DMA((SB,))],
    )
    return pl.pallas_call(
        kern, grid_spec=gs, out_shape=jax.ShapeDtypeStruct((n_assign, nc, LANE), F32),
        compiler_params=_cp(("arbitrary", "arbitrary"), MOE_VMEM), name="moe_experts",
    )(tok, aid, nvalid, gexp, nsub, nused, x3, w_gate, w_up, w_down)


def moe_combine_ln(x, ytok, gate, ln_g, ln_b, idx, alpha, n_tok, tm=128):
    D = x.shape[1]
    nc = D // LANE
    tm = min(tm, n_tok)

    def kern(x_ref, y_ref, gt_ref, g_ref, b_ref, o_ref, ob_ref):
        gt = _bf16r(gt_ref[...])
        mix = sum(gt[:, k:k + 1] * _bf16r(y_ref[:, k].reshape(tm, D)) for k in range(MOE_TOP_K))
        y = _layer_norm(alpha * x_ref[...] + mix, g_ref[...], b_ref[...])
        o_ref[...] = y
        ob_ref[...] = y.astype(BF16)

    row = pl.BlockSpec((tm, D), lambda i: (i, 0))
    par = pl.BlockSpec((None, 1, D), lambda i: (idx, 0, 0))
    return pl.pallas_call(
        kern, grid=(n_tok // tm,),
        in_specs=[row, pl.BlockSpec((tm, MOE_TOP_K, nc, LANE), lambda i: (i, 0, 0, 0)),
                  pl.BlockSpec((tm, MOE_TOP_K), lambda i: (i, 0)), par, par],
        out_specs=[row, row],
        out_shape=[jax.ShapeDtypeStruct((n_tok, D), F32), jax.ShapeDtypeStruct((n_tok, D), BF16)],
        compiler_params=_cp(("parallel",)), name="moe_combine_ln",
    )(x, ytok, gate, ln_g, ln_b)


def hier_moe_ln(x, x3, n_tok, w_rt, b_rt, w_gate, w_up, w_down, layer, ln_g, ln_b, alpha, rows):
    N, D = x.shape
    xr = x if N % LANE == 0 else jnp.pad(x, ((0, LANE - N % LANE), (0, 0)))
    eidx, gate = moe_router(xr, w_rt, b_rt)
    eidx, gate = eidx[:, :n_tok], gate[:, :n_tok]
    A = n_tok * MOE_TOP_K
    e = eidx.T.reshape(A)
    onehot = (e[:, None] == jnp.arange(MOE_EXPERTS, dtype=I32)[None, :]).astype(I32)
    csum = jnp.cumsum(onehot, axis=0)
    counts = csum[-1]
    pos = jnp.sum(csum * onehot, axis=1) - 1
    grows = MOE_SUB * rows
    padded = (counts + grows - 1) // grows * grows
    pad_end = jnp.cumsum(padded)
    pad_start = pad_end - padded
    dest = (jnp.sum(onehot * pad_start[None, :], axis=1) + pos).astype(I32)
    n_groups = min(A // grows + MOE_EXPERTS, A)
    gexp = jnp.minimum(jnp.searchsorted(pad_end, jnp.arange(n_groups, dtype=I32) * grows, side="right"),
                       MOE_EXPERTS - 1).astype(I32)
    nused = (pad_end[-1:] // grows).astype(I32)
    valid_end = (pad_start + counts)[gexp]
    nvalid = jnp.clip(jnp.repeat(valid_end, MOE_SUB) - jnp.arange(n_groups * MOE_SUB, dtype=I32) * rows,
                      0, rows).astype(I32)
    nsub = jnp.sum((nvalid > 0).reshape(n_groups, MOE_SUB), axis=1).astype(I32)
    aid = jnp.zeros((n_groups * grows,), I32).at[dest].set(jnp.arange(A, dtype=I32))
    ytok = moe_experts_grouped(x3, aid // MOE_TOP_K, aid, nvalid, gexp, nsub, nused, w_gate, w_up, w_down,
                               layer, rows, A)
    ytok = ytok.reshape(n_tok, MOE_TOP_K, D // LANE, LANE)
    return moe_combine_ln(x, ytok, gate.T, ln_g, ln_b, layer * 3 + 2, alpha, n_tok)


def _rope_tables(pos):
    half = HEAD_DIM // 2
    inv_freq = ROPE_THETA ** (-jnp.arange(half, dtype=F32) / half)
    ang = pos.astype(F32)[:, None] * inv_freq[None, :]
    cos, sin = jnp.cos(ang), jnp.sin(ang)
    return jnp.concatenate([cos, cos], axis=1), jnp.concatenate([-sin, sin], axis=1)


def _even_projections(x, pos_rows, w_in, w_rest, w_gates, e, tm, q_dtype):
    G, d = NSA_G, HEAD_DIM
    q_cols = G * NSA_R * d
    kv_cols = N_KV_STREAMS * G * d
    cos, sin = _rope_tables(pos_rows)
    tn = G * d
    mm = functools.partial(matmul, x, tm=tm)
    q = mm(w_in, layer=e, col_off=0, n_out=q_cols, tn=tn, rope=(cos, sin, tuple(range(q_cols // tn))),
           out_dtype=q_dtype)
    kv = mm(w_in, layer=e, col_off=q_cols, n_out=kv_cols, tn=tn, rope=(cos, sin, (0, 2)))
    win = mm(w_in, layer=e, col_off=q_cols + kv_cols, n_out=2 * G * d, tn=tn, rope=(cos, sin, (0,)))
    gates = mm(w_gates, tn=G * LANE)
    rw = w_rest.shape[1] // 2
    u = mm(w_rest, n_out=rw)
    gb = mm(w_rest, col_off=rw, n_out=rw)
    return q, kv, win, gates, u, gb


def _pad_rows(a, n):
    return jnp.pad(a, ((0, n - a.shape[0]),) + ((0, 0),) * (a.ndim - 1))


def kernel(x_prompt, x_sample, mem_prompt, cache_nsa_kv, cache_nsa_win, state_rglru_h, state_rglru_conv,
           state_pool, cache_mem_kv, page_table, w_in, w_out, cmp_w1, cmp_w2, cmp_pe, conv_w, conv_b,
           lru_wa, lru_ba, lru_wx, lru_bx, lru_lambda, pool_w, pool_scale, mem_wq, mem_wk, mem_wv, mem_wo,
           ln_g, ln_b, moe_w_coarse, moe_b_coarse, moe_w_fine, moe_b_fine, moe_w_gate, moe_w_up, moe_w_down):
    B, T, D = x_prompt.shape
    DB = x_sample.shape[0]
    depth = ln_g.shape[0]
    n_pages = page_table.shape[1]
    past_len = n_pages * PAGE_SIZE
    n_phys = cache_nsa_kv.shape[1]
    G, R, d = NSA_G, NSA_R, HEAD_DIM
    alpha = (2 * depth) ** 0.25
    q_cols, kv_cols, win_cols, gate_cols = G * R * d, N_KV_STREAMS * G * d, 2 * G * d, 3 * G * R
    rest_off = q_cols + kv_cols + win_cols + gate_cols
    RW = (w_in.shape[2] - rest_off) // 2
    MW = mem_wq.shape[2]
    M = mem_prompt.shape[1]
    SR = SAMPLE_ROWS
    TS = SAMPLE_ROWS

    xp = x_prompt.reshape(B * T, D)
    xpb = xp.astype(BF16)
    xs = _pad_rows(x_sample.reshape(DB, D), SR)
    lng = ln_g.reshape(depth * 3, 1, D)
    lnb = ln_b.reshape(depth * 3, 1, D)
    memf = mem_prompt.reshape(B * M, D)

    kv_p, kv_s, win_p, win_s, h_p, h_s, cv_p, cv_s, pl_p, pl_s, mem_p = ([] for _ in range(11))
    for layer in range(depth):
        if layer % 2 == 0:
            e = layer // 2
            wg = w_in[e, :, q_cols + kv_cols + win_cols:rest_off].reshape(D, G, 3 * R)
            wg = jnp.pad(wg, ((0, 0), (0, 0), (0, LANE - 3 * R))).reshape(D, G * LANE)
            w_rest = w_in[e, :, rest_off:]

            q, kv, win, gates, u, gb = _even_projections(
                xpb, jnp.arange(T), w_in, w_rest, wg, e, 1024, BF16)
            pages_p = kv.reshape(B * T // PAGE_SIZE, PAGE_SIZE, N_KV_STREAMS, G, d)
            new_kv = pages_p.reshape(B, T, N_KV_STREAMS, G, d)
            ab = compress_ab(pages_p, jnp.arange(B * T // PAGE_SIZE, dtype=I32), cmp_w1, cmp_pe, e, B,
                             T // PAGE_SIZE)
            kvcmp = compress_fin(ab, cmp_w2, e)
            o_nsa = nsa_prompt(q, kv, win, kvcmp, gates, B, T)
            y_rnn, h_all = rglru(u, gb, jnp.zeros((B, SUBLANE, RW), F32), jnp.zeros((B, 1, RW), F32),
                                 conv_w, conv_b, lru_wa, lru_ba, lru_wx, lru_bx, lru_lambda, e, B, T)
            mix_p = matmul(o_nsa, w_out, layer=e, x2=y_rnn)
            keep = min(WINDOW, T)
            kv_p.append(new_kv)
            win_p.append(win.reshape(B, T, 2, G, d)[:, T - keep:])
            h_p.append(h_all.reshape(B, T, RW)[:, T - 1])
            cv_p.append(u.reshape(B, T, RW)[:, T - (CONV_WIDTH - 1):])

            q, kv, win, gates, u, gb = _even_projections(
                xs, jnp.full((SR,), past_len), w_in, w_rest, wg, e, SR, F32)
            new_kv_s = kv[:DB].reshape(DB, N_KV_STREAMS, G, d)
            new_win_s = win[:DB].reshape(DB, 2, G, d)
            pages_s = cache_nsa_kv.reshape(cache_nsa_kv.shape[0] * n_phys, PAGE_SIZE, N_KV_STREAMS, G, d)
            pid = (page_table.reshape(DB * n_pages) + e * n_phys).astype(I32)
            ab = compress_ab(pages_s, pid, cmp_w1, cmp_pe, e, DB, n_pages)
            kvcmp = compress_fin(ab, cmp_w2, e)
            q8 = jnp.pad(q[:DB].reshape(DB, G, R, d), ((0, 0), (0, 0), (0, SUBLANE - R), (0, 0)))
            o_cmp, imp = nsa_sample_cmp(q8, kvcmp, past_len)
            idx_t = nsa_sample_topk(_pad_rows(imp.reshape(DB * SUBLANE, -1), LANE), past_len)
            n_top = idx_t.shape[0]
            idx = idx_t[:, :DB * SUBLANE].T.reshape(DB, SUBLANE, n_top)[:, :G].reshape(DB * G * n_top)
            o8 = nsa_sample_attend(q8, o_cmp, gates[:DB].reshape(DB, G, 1, LANE), idx, pid, pages_s,
                                   new_kv_s, cache_nsa_win, new_win_s, e, past_len)
            o_nsa = _pad_rows(o8[:, :, :R].reshape(DB, G * R * d), SR)
            conv0 = state_rglru_conv[e]
            u_t = jnp.pad(u[:DB, None, :], ((0, 0), (0, TS - 1), (0, 0))).reshape(DB * TS, RW)
            gb_t = jnp.pad(gb[:DB, None, :], ((0, 0), (0, TS - 1), (0, 0))).reshape(DB * TS, RW)
            c0 = jnp.pad(conv0, ((0, 0), (SUBLANE - (CONV_WIDTH - 1), 0), (0, 0)))
            y_t, h_t = rglru(u_t, gb_t, c0, state_rglru_h[e][:, None, :], conv_w, conv_b, lru_wa, lru_ba,
                             lru_wx, lru_bx, lru_lambda, e, DB, TS)
            y_rnn = _pad_rows(y_t.reshape(DB, TS, RW)[:, 0], SR)
            mix_s = matmul(o_nsa, w_out, layer=e, x2=y_rnn)
            keep = min(WINDOW, past_len + 1)
            win_all = jnp.concatenate([cache_nsa_win[e], new_win_s[:, None]], axis=1)
            kv_s.append(new_kv_s[:, None])
            win_s.append(win_all[:, win_all.shape[1] - keep:])
            h_s.append(h_t.reshape(DB, TS, RW)[:, 0])
            cv_s.append(jnp.concatenate([conv0, u[:DB, None, :]], axis=1)[:, 1:])
        else:
            o = layer // 2
            mix_p = pool_mixer(xp, jnp.zeros((B, 2 * SUBLANE, D), F32), pool_w, pool_scale, o, B, T, 0)
            pl_p.append(xp.reshape(B, T, D)[:, T - POOL_BUF:])
            tsp = 2 * SUBLANE
            x_t = jnp.pad(xs[:DB, None, :], ((0, 0), (0, tsp - 1), (0, 0))).reshape(DB * tsp, D)
            buf = jnp.pad(state_pool[o], ((0, 0), (tsp - POOL_BUF, 0), (0, 0)))
            mix_t = pool_mixer(x_t, buf, pool_w, pool_scale, o, DB, tsp, past_len)
            mix_s = _pad_rows(mix_t.reshape(DB, tsp, D)[:, 0], SR)
            pl_s.append(jnp.concatenate([state_pool[o], xs[:DB, None, :]], axis=1)[:, 1:])

        xp, xpb = add_ln(xp, mix_p, lng, lnb, layer * 3, alpha)
        xs, _ = add_ln(xs, mix_s, lng, lnb, layer * 3, alpha)

        w_kv = jnp.concatenate([mem_wk[layer], mem_wv[layer]], axis=1)
        mkv = matmul(memf, w_kv).reshape(B, M, 2, MW)
        mem_p.append(mkv)
        qm = matmul(xpb, mem_wq, layer=layer, out_dtype=BF16)
        att = mem_attn(qm, mkv, B, T)
        xp, xpb, xp3 = add_ln(xp, matmul(att, mem_wo, layer=layer), lng, lnb, layer * 3 + 1, alpha,
                              token_major=True)
        qm = matmul(xs, mem_wq, layer=layer)
        qm_t = jnp.pad(qm[:DB, None, :], ((0, 0), (0, TS - 1), (0, 0))).reshape(DB * TS, MW)
        att_t = mem_attn(qm_t, cache_mem_kv.reshape((-1,) + cache_mem_kv.shape[2:]), DB, TS,
                         kv_off=layer * DB)
        att = _pad_rows(att_t.reshape(DB, TS, MW)[:, 0], SR)
        xs, _, xs3 = add_ln(xs, matmul(att, mem_wo, layer=layer), lng, lnb, layer * 3 + 1, alpha,
                            token_major=True)

        w_rt = _pad_rows(jnp.concatenate([moe_w_coarse[layer], moe_w_fine[layer]], axis=1).T, ROUTER_ROWS)
        b_rt = _pad_rows(jnp.concatenate([moe_b_coarse[layer], moe_b_fine[layer]])[:, None], ROUTER_ROWS)
        xp, xpb = hier_moe_ln(xp, xp3, B * T, w_rt, b_rt, moe_w_gate, moe_w_up, moe_w_down, layer, lng, lnb,
                              alpha, rows=256)
        xs = _pad_rows(hier_moe_ln(xs, xs3, DB, w_rt, b_rt, moe_w_gate, moe_w_up, moe_w_down, layer, lng, lnb,
                                   alpha, rows=SUBLANE)[0], SR)

    return (xp.reshape(B, T, D), xs[:DB].reshape(DB, 1, D), jnp.stack(kv_p), jnp.stack(kv_s),
            jnp.stack(win_p), jnp.stack(win_s), jnp.stack(h_p), jnp.stack(h_s), jnp.stack(cv_p),
            jnp.stack(cv_s), jnp.stack(pl_p), jnp.stack(pl_s), jnp.stack(mem_p))
```
